```python
import jax, jax.numpy as jnp
from jax import lax
import numpy as np

D_MODEL = 1024
BATCH = 4
SEQ = 4096
DEPTH = 1
DEC_BATCH = 128
DEC_SEQ = 8
PAST_LEN = 2048
PAGE_SIZE = 128

MIX_WIDTH = D_MODEL
SB_HEADS = 8
SB_HEAD_DIM = (MIX_WIDTH // 2) // SB_HEADS
SB_WIDTH = SB_HEADS * SB_HEAD_DIM
SB_BLOCK = 128
SB_BIAS_HI = -5.0
SB_BIAS_LO = -9.0
DN_HEADS = 4
DN_HEAD_DIM = (MIX_WIDTH - SB_WIDTH) // DN_HEADS
DN_WIDTH = DN_HEADS * DN_HEAD_DIM
DN_CONV = 4
DN_CHUNK = 64
D_FF = 256 * ((8 * D_MODEL // 3 + 255) // 256)
FFN_CONV = 3
NORM_EPS = 1e-6
N_MOD = 6
IN_COLS = 3 * SB_WIDTH + 3 * DN_WIDTH + 2 * DN_HEADS + DN_WIDTH

kernel_name = 'hybrid_stickbreak_gdn_convffn_adaln_step'


def rms_norm(x, g):
    xf = x.astype(jnp.float32)
    y = xf * lax.rsqrt(jnp.mean(xf * xf, axis=-1, keepdims=True) + NORM_EPS)
    return (y * g.astype(jnp.float32)).astype(x.dtype)


def l2_norm(x):
    xf = x.astype(jnp.float32)
    return xf * lax.rsqrt(jnp.sum(xf * xf, axis=-1, keepdims=True) + NORM_EPS)


def causal_dwconv(x, buf, w):
    width = w.shape[0]
    t = x.shape[1]
    xp = jnp.concatenate([buf.astype(x.dtype), x], axis=1)
    y = w[0] * xp[:, 0:t]
    for i in range(1, width):
        y = y + w[i] * xp[:, i:i + t]
    return y, xp[:, t:]


def sb_block(q, k, v, bias, q_pos, k_pos):
    f32 = jnp.float32
    z = jnp.einsum('bqhd,bkhd->bhqk', q.astype(f32), k.astype(f32)) * (SB_HEAD_DIM ** -0.5)
    z = z + bias.astype(f32)[None, :, None, None]
    mask = k_pos[None, :] < q_pos[:, None]
    log_beta = jax.nn.log_sigmoid(z)
    log_one_minus = jnp.where(mask, jax.nn.log_sigmoid(-z), 0.0)
    later = lax.cumsum(log_one_minus, axis=3, reverse=True) - log_one_minus
    att = jnp.where(mask, jnp.exp(log_beta + later), 0.0)
    out = jnp.einsum('bhqk,bkhd->bqhd', att, v.astype(f32))
    return out.astype(q.dtype)


def sb_attention(q, k, v, bias, q_offset):
    tq = q.shape[1]
    outs = []
    for start in range(0, tq, SB_BLOCK):
        end = min(start + SB_BLOCK, tq)
        kend = q_offset + end
        q_pos = q_offset + jnp.arange(start, end)
        k_pos = jnp.arange(kend)
        outs.append(sb_block(q[:, start:end], k[:, :kend], v[:, :kend], bias, q_pos, k_pos))
    return jnp.concatenate(outs, axis=1)


def gated_delta_rule(q, k, v, beta, log_a, s0):
    f32 = jnp.float32
    bsz, t, h, dk = q.shape
    dv = v.shape[-1]
    c = min(DN_CHUNK, t)
    pad = (-t) % c
    n = (t + pad) // c

    def to_chunks(a):
        a = a.astype(f32)
        if pad:
            a = jnp.pad(a, [(0, 0), (0, pad)] + [(0, 0)] * (a.ndim - 2))
        a = a.reshape((bsz, n, c) + a.shape[2:])
        return a.transpose(1, 0, 3, 2, 4) if a.ndim == 5 else a.transpose(1, 0, 3, 2)

    qc, kc, vc = to_chunks(q), to_chunks(k), to_chunks(v)
    bc, ac = to_chunks(beta), to_chunks(log_a)
    g = jnp.cumsum(ac, axis=-1)
    idx = jnp.arange(c)
    lower_incl = idx[:, None] >= idx[None, :]
    strict = idx[:, None] > idx[None, :]
    decay = jnp.exp(jnp.where(lower_incl, g[..., :, None] - g[..., None, :], -jnp.inf))
    kb = kc * bc[..., None]
    a_mat = jnp.where(strict, jnp.einsum('nbhid,nbhjd->nbhij', kb, kc) * decay, 0.0)
    rhs = jnp.concatenate([vc * bc[..., None], kb * jnp.exp(g)[..., None]], axis=-1)
    sol = lax.linalg.triangular_solve(a_mat + jnp.eye(c, dtype=f32), rhs,
                                      left_side=True, lower=True, unit_diagonal=True)
    u_val, k_cum = sol[..., :dv], sol[..., dv:]
    intra = jnp.einsum('nbhid,nbhjd->nbhij', qc, kc) * decay
    q_dec = qc * jnp.exp(g)[..., None]
    k_dec = kc * jnp.exp(g[..., -1:] - g)[..., None]
    g_last = jnp.exp(g[..., -1])

    def step(s, inp):
        u, kcum, intra_c, qd, kd, gl = inp
        v_new = u - jnp.einsum('bhcd,bhde->bhce', kcum, s)
        o = jnp.einsum('bhcd,bhde->bhce', qd, s) + jnp.einsum('bhij,bhje->bhie', intra_c, v_new)
        s = s * gl[..., None, None] + jnp.einsum('bhcd,bhce->bhde', kd, v_new)
        return s, o

    s_fin, o = lax.scan(step, s0.astype(f32), (u_val, k_cum, intra, q_dec, k_dec, g_last))
    o = o.transpose(1, 0, 3, 2, 4).reshape(bsz, n * c, h, dv)[:, :t]
    return o, s_fin


def hybrid_layer(x, c, past_k, past_v, s0, dn_buf, ffn_buf,
                 w_ada, b_ada, g_attn_norm, w_in, g_q, g_k, sb_bias, g_sb_out, w_dn_conv,
                 a_log, dt_bias, g_dn_out, w_out, g_ffn_norm, w_up, w_ffn_conv, w_down):
    bsz, t, _ = x.shape
    mod = (jax.nn.silu(c) @ w_ada + b_ada)[:, None, :]
    sh1, sc1, gt1, sh2, sc2, gt2 = jnp.split(mod, N_MOD, axis=-1)

    h = rms_norm(x, g_attn_norm) * (1.0 + sc1) + sh1
    proj = h @ w_in
    o1 = 3 * SB_WIDTH
    o2 = o1 + 3 * DN_WIDTH
    o3 = o2 + DN_HEADS
    o4 = o3 + DN_HEADS
    q_sb, k_sb, v_sb = jnp.split(proj[..., :o1], 3, axis=-1)
    qkv_dn = proj[..., o1:o2]
    b_raw = proj[..., o2:o3]
    a_raw = proj[..., o3:o4]
    z_gate = proj[..., o4:]

    hs = (bsz, t, SB_HEADS, SB_HEAD_DIM)
    q = rms_norm(q_sb.reshape(hs), g_q)
    k = rms_norm(k_sb.reshape(hs), g_k)
    v = v_sb.reshape(hs)
    if past_k is None:
        k_all, v_all, q_offset = k, v, 0
    else:
        k_all = jnp.concatenate([past_k.astype(k.dtype), k], axis=1)
        v_all = jnp.concatenate([past_v.astype(v.dtype), v], axis=1)
        q_offset = past_k.shape[1]
    o_sb = rms_norm(sb_attention(q, k_all, v_all, sb_bias, q_offset), g_sb_out)

    qkv_c, new_dn_buf = causal_dwconv(qkv_dn, dn_buf, w_dn_conv)
    qkv_c = jax.nn.silu(qkv_c)
    qd, kd, vd = jnp.split(qkv_c, 3, axis=-1)
    hd = (bsz, t, DN_HEADS, DN_HEAD_DIM)
    qd = l2_norm(qd.reshape(hd)) * (DN_HEAD_DIM ** -0.5)
    kd = l2_norm(kd.reshape(hd))
    vd = vd.reshape(hd)
    beta = jax.nn.sigmoid(b_raw.astype(jnp.float32))
    log_a = -jnp.exp(a_log.astype(jnp.float32)) * jax.nn.softplus(
        a_raw.astype(jnp.float32) + dt_bias.astype(jnp.float32))
    o_dn, s_new = gated_delta_rule(qd, kd, vd, beta, log_a, s0)
    zg = jax.nn.silu(z_gate.astype(jnp.float32)).reshape(hd)
    o_dn = (rms_norm(o_dn, g_dn_out) * zg).astype(x.dtype)

    mixed = jnp.concatenate([o_sb.reshape(bsz, t, SB_WIDTH), o_dn.reshape(bsz, t, DN_WIDTH)], axis=-1)
    x = x + gt1 * (mixed @ w_out)

    h2 = rms_norm(x, g_ffn_norm) * (1.0 + sc2) + sh2
    up, new_ffn_buf = causal_dwconv(h2 @ w_up, ffn_buf, w_ffn_conv)
    u, gate = jnp.split(up, 2, axis=-1)
    x = x + gt2 * ((jax.nn.silu(gate) * u) @ w_down)
    return x, k, v, s_new.astype(s0.dtype), new_dn_buf, new_ffn_buf


def setup_inputs(seed: int = 0) -> dict:
    key = jax.random.key(seed)
    ks = jax.random.split(key, 32)
    f32 = jnp.float32
    n_pages = PAST_LEN // PAGE_SIZE
    n_used = DEC_BATCH * n_pages
    n_phys = n_used + n_used // 4

    def nrm(k, shape, scale):
        return jax.random.normal(k, shape, f32) * scale

    page_table = jax.random.permutation(ks[0], n_phys)[:n_used].reshape(DEC_BATCH, n_pages).astype(jnp.int32)
    dt = jnp.exp(jax.random.uniform(ks[1], (DEPTH, DN_HEADS), f32, np.log(1e-3), np.log(1e-1)))
    sb_bias = jnp.linspace(SB_BIAS_HI, SB_BIAS_LO, SB_HEADS, dtype=f32)[None, :] + nrm(ks[26], (DEPTH, SB_HEADS), 0.1)
    return {
        'x_prompt': nrm(ks[2], (BATCH, SEQ, D_MODEL), 1.0),
        'x_sample': nrm(ks[3], (DEC_BATCH, DEC_SEQ, D_MODEL), 1.0),
        'c_prompt': nrm(ks[4], (BATCH, D_MODEL), 1.0),
        'c_sample': nrm(ks[5], (DEC_BATCH, D_MODEL), 1.0),
        'cache_k': nrm(ks[6], (DEPTH, n_phys, PAGE_SIZE, SB_HEADS, SB_HEAD_DIM), 1.0),
        'cache_v': nrm(ks[7], (DEPTH, n_phys, PAGE_SIZE, SB_HEADS, SB_HEAD_DIM), 1.0),
        'page_table': page_table,
        'state_delta': nrm(ks[8], (DEPTH, DEC_BATCH, DN_HEADS, DN_HEAD_DIM, DN_HEAD_DIM), 0.05),
        'state_dn_conv': nrm(ks[9], (DEPTH, DEC_BATCH, DN_CONV - 1, 3 * DN_WIDTH), 1.0),
        'state_ffn_conv': nrm(ks[10], (DEPTH, DEC_BATCH, FFN_CONV - 1, 2 * D_FF), 1.0),
        'w_ada': nrm(ks[11], (DEPTH, D_MODEL, N_MOD * D_MODEL), 0.5 * D_MODEL ** -0.5),
        'b_ada': nrm(ks[12], (DEPTH, N_MOD * D_MODEL), 0.01),
        'g_attn_norm': 1.0 + nrm(ks[13], (DEPTH, D_MODEL), 0.02),
        'w_in': nrm(ks[14], (DEPTH, D_MODEL, IN_COLS), D_MODEL ** -0.5),
        'g_q': 1.0 + nrm(ks[15], (DEPTH, SB_HEAD_DIM), 0.02),
        'g_k': 1.0 + nrm(ks[16], (DEPTH, SB_HEAD_DIM), 0.02),
        'sb_bias': sb_bias,
        'g_sb_out': 1.0 + nrm(ks[17], (DEPTH, SB_HEAD_DIM), 0.02),
        'w_dn_conv': nrm(ks[18], (DEPTH, DN_CONV, 3 * DN_WIDTH), DN_CONV ** -0.5),
        'a_log': jnp.log(jax.random.uniform(ks[19], (DEPTH, DN_HEADS), f32, 1.0, 16.0)),
        'dt_bias': dt + jnp.log(-jnp.expm1(-dt)),
        'g_dn_out': 1.0 + nrm(ks[20], (DEPTH, DN_HEAD_DIM), 0.02),
        'w_out': nrm(ks[21], (DEPTH, MIX_WIDTH, D_MODEL), MIX_WIDTH ** -0.5),
        'g_ffn_norm': 1.0 + nrm(ks[22], (DEPTH, D_MODEL), 0.02),
        'w_up': nrm(ks[23], (DEPTH, D_MODEL, 2 * D_FF), D_MODEL ** -0.5),
        'w_ffn_conv': nrm(ks[24], (DEPTH, FFN_CONV, 2 * D_FF), FFN_CONV ** -0.5),
        'w_down': nrm(ks[25], (DEPTH, D_FF, D_MODEL), D_FF ** -0.5),
    }


def reference(x_prompt, x_sample, c_prompt, c_sample, cache_k, cache_v, page_table,
              state_delta, state_dn_conv, state_ffn_conv,
              w_ada, b_ada, g_attn_norm, w_in, g_q, g_k, sb_bias, g_sb_out, w_dn_conv,
              a_log, dt_bias, g_dn_out, w_out, g_ffn_norm, w_up, w_ffn_conv, w_down):
    bp = x_prompt.shape[0]
    bs = x_sample.shape[0]
    past_len = page_table.shape[1] * PAGE_SIZE
    yp, ys = x_prompt, x_sample
    kp_l, vp_l, ks_l, vs_l, sp_l, ss_l, dcp_l, dcs_l, fcp_l, fcs_l = ([] for _ in range(10))
    for l in range(DEPTH):
        p = (w_ada[l], b_ada[l], g_attn_norm[l], w_in[l], g_q[l], g_k[l], sb_bias[l], g_sb_out[l],
             w_dn_conv[l], a_log[l], dt_bias[l], g_dn_out[l], w_out[l], g_ffn_norm[l], w_up[l],
             w_ffn_conv[l], w_down[l])
        s0 = jnp.zeros((bp, DN_HEADS, DN_HEAD_DIM, DN_HEAD_DIM), state_delta.dtype)
        dn0 = jnp.zeros((bp, DN_CONV - 1, 3 * DN_WIDTH), x_prompt.dtype)
        ff0 = jnp.zeros((bp, FFN_CONV - 1, 2 * D_FF), x_prompt.dtype)
        yp, kp, vp, sp, dcp, fcp = hybrid_layer(yp, c_prompt, None, None, s0, dn0, ff0, *p)
        past_k = cache_k[l][page_table].reshape(bs, past_len, SB_HEADS, SB_HEAD_DIM)
        past_v = cache_v[l][page_table].reshape(bs, past_len, SB_HEADS, SB_HEAD_DIM)
        ys, ksm, vsm, ss, dcs, fcs = hybrid_layer(ys, c_sample, past_k, past_v, state_delta[l],
                                                  state_dn_conv[l], state_ffn_conv[l], *p)
        kp_l.append(kp); vp_l.append(vp); ks_l.append(ksm); vs_l.append(vsm)
        sp_l.append(sp); ss_l.append(ss); dcp_l.append(dcp); dcs_l.append(dcs)
        fcp_l.append(fcp); fcs_l.append(fcs)
    return (yp, ys, jnp.stack(kp_l), jnp.stack(vp_l), jnp.stack(ks_l), jnp.stack(vs_l),
            jnp.stack(sp_l), jnp.stack(ss_l), jnp.stack(dcp_l), jnp.stack(dcs_l),
            jnp.stack(fcp_l), jnp.stack(fcs_l))
```

```python
import functools

import jax
import jax.numpy as jnp
from jax import lax
from jax.experimental import pallas as pl
from jax.experimental.pallas import tpu as pltpu

F32 = jnp.float32
BF16 = jnp.bfloat16

NORM_EPS = 1e-6
LANES = 128
SUBLANES = 8
VMEM_LIMIT = 56 * 1024 * 1024

SB_HEADS = 8
SB_DIM = 64
SB_WIDTH = SB_HEADS * SB_DIM
DN_HEADS = 4
DN_DIM = 128
DN_WIDTH = DN_HEADS * DN_DIM
DN_BLOCK = 128
PAGES_PER_STEP = 8


def _dot(a, b):
    return jnp.dot(a, b, preferred_element_type=F32)


def _dot_nt(a, b):
    return lax.dot_general(a, b, (((1,), (1,)), ((), ())), preferred_element_type=F32)


def _dot_tn(a, b):
    return lax.dot_general(a, b, (((0,), (0,)), ((), ())), preferred_element_type=F32)


def _silu(x):
    return x * jax.nn.sigmoid(x)


def _softplus_parts(z):
    l = jnp.log(1.0 + jnp.exp(-jnp.abs(z)))
    return jnp.maximum(z, 0.0) + l, jnp.minimum(z, 0.0) - l


def _rms_rows(x, g):
    ms = jnp.mean(x * x, axis=-1, keepdims=True)
    return x * lax.rsqrt(ms + NORM_EPS) * g


def _half_lane_rms(blk, lo):
    sq = blk * blk
    s_lo = jnp.sum(jnp.where(lo, sq, 0.0), axis=-1, keepdims=True)
    s_hi = jnp.sum(jnp.where(lo, 0.0, sq), axis=-1, keepdims=True)
    ms = jnp.where(lo, s_lo, s_hi) * (1.0 / SB_DIM)
    return blk * lax.rsqrt(ms + NORM_EPS)


def _cparams(sem):
    return pltpu.CompilerParams(dimension_semantics=sem, vmem_limit_bytes=VMEM_LIMIT)


def _resident(shape):
    nd = len(shape)
    return pl.BlockSpec(shape, lambda *_: (0,) * nd, pipeline_mode=pl.Buffered(1))


def _ada_kernel(c_ref, w_ref, b_ref, o_ref):
    a = _silu(c_ref[...]).astype(BF16)
    o_ref[...] = _dot(a, w_ref[...].astype(BF16)) + b_ref[...]


def _ada(c, w, b):
    m, d = c.shape
    n = w.shape[1]
    tn = 512
    return pl.pallas_call(
        _ada_kernel,
        grid=(n // tn,),
        in_specs=[pl.BlockSpec((m, d), lambda j: (0, 0)),
                  pl.BlockSpec((d, tn), lambda j: (0, j)),
                  pl.BlockSpec((1, tn), lambda j: (0, j))],
        out_specs=pl.BlockSpec((m, tn), lambda j: (0, j)),
        out_shape=jax.ShapeDtypeStruct((m, n), F32),
        compiler_params=_cparams(("parallel",)),
        name="adaln_mod",
    )(c, w, b.reshape(1, n))


C_SB = 0
C_DN = 3 * SB_WIDTH
C_Z = C_DN + 3 * DN_WIDTH
C_G = C_Z + DN_WIDTH
C_END = C_G + LANES


def _causal_conv(cur, taps, prev_rows, grouped, bufp):
    w = taps.shape[0]
    tm = cur.shape[0]
    y = taps[w - 1:w, :] * cur
    if grouped:
        rowm = lax.broadcasted_iota(jnp.int32, (tm, 1), 0) & (SUBLANES - 1)
        for s in range(1, w):
            r = pltpu.roll(cur, s, 0)
            back = (w - 1) - s
            bs = bufp if back == 0 else pltpu.roll(bufp, tm - back, 0)
            y = y + taps[w - 1 - s:w - s, :] * jnp.where(rowm >= s, r, bs)
        return y
    row8 = lax.broadcasted_iota(jnp.int32, (SUBLANES, 1), 0)
    yh = taps[w - 1:w, :] * cur[0:SUBLANES]
    for s in range(1, w):
        r = pltpu.roll(cur, s, 0)
        y = y + taps[w - 1 - s:w - s, :] * r
        head = jnp.where(row8 < s, pltpu.roll(prev_rows, s, 0), r[0:SUBLANES])
        yh = yh + taps[w - 1 - s:w - s, :] * head
    return jnp.concatenate([yh, y[SUBLANES:]], axis=0)


def _inproj_kernel(*refs, grouped):
    if grouped:
        (x_ref, sc_ref, sh_ref, gn_ref, w_ref, gq_ref, gk_ref, wc_ref, al_ref, dt_ref, buf_ref,
         qb_ref, kb_ref, vb_ref, kf_ref, vf_ref, qd_ref, kd_ref, vd_ref, zg_ref, gate_ref, nbuf_ref) = refs
        carry_ref = None
    else:
        (x_ref, sc_ref, sh_ref, gn_ref, w_ref, gq_ref, gk_ref, wc_ref, al_ref, dt_ref,
         qb_ref, kb_ref, vb_ref, kf_ref, vf_ref, qd_ref, kd_ref, vd_ref, zg_ref, gate_ref, nbuf_ref,
         carry_ref) = refs
        buf_ref = None

        @pl.when(pl.program_id(1) == 0)
        def _():
            carry_ref[...] = jnp.zeros_like(carry_ref)

    x = x_ref[...]
    h = _rms_rows(x, gn_ref[...]) * (1.0 + sc_ref[...]) + sh_ref[...]
    hb = h.astype(BF16)
    tm = x.shape[0]
    lane = lax.broadcasted_iota(jnp.int32, (1, LANES), 1)
    lo = lane < SB_DIM

    for c in range(SB_WIDTH // LANES):
        sl = slice(c * LANES, (c + 1) * LANES)
        pq = _dot(hb, w_ref[:, C_SB + c * LANES:C_SB + (c + 1) * LANES])
        qn = _half_lane_rms(pq, lo) * gq_ref[:, sl]
        qb_ref[:, sl] = (qn * (SB_DIM ** -0.5)).astype(BF16)
        pk = _dot(hb, w_ref[:, C_SB + SB_WIDTH + c * LANES:C_SB + SB_WIDTH + (c + 1) * LANES])
        kn = _half_lane_rms(pk, lo) * gk_ref[:, sl]
        kf_ref[:, sl] = kn
        kb_ref[:, sl] = kn.astype(BF16)
    pv = _dot(hb, w_ref[:, C_SB + 2 * SB_WIDTH:C_SB + 3 * SB_WIDTH])
    vf_ref[...] = pv
    vb_ref[...] = pv.astype(BF16)

    cur = _dot(hb, w_ref[:, C_DN:C_Z])
    if grouped:
        y = _causal_conv(cur, wc_ref[...], None, True, buf_ref[...])
        nbuf_ref[...] = pltpu.roll(cur, tm - (SUBLANES - (wc_ref.shape[0] - 1)), 0)
    else:
        y = _causal_conv(cur, wc_ref[...], carry_ref[...], False, None)
        carry_ref[...] = cur[tm - SUBLANES:tm]
        nbuf_ref[...] = cur[tm - SUBLANES:tm]
    a = _silu(y)
    for hh in range(DN_HEADS):
        sl = slice(hh * DN_DIM, (hh + 1) * DN_DIM)
        qh = a[:, hh * DN_DIM:(hh + 1) * DN_DIM]
        kh = a[:, DN_WIDTH + hh * DN_DIM:DN_WIDTH + (hh + 1) * DN_DIM]
        qd_ref[:, sl] = qh * (lax.rsqrt(jnp.sum(qh * qh, axis=-1, keepdims=True) + NORM_EPS) * (DN_DIM ** -0.5))
        kd_ref[:, sl] = kh * lax.rsqrt(jnp.sum(kh * kh, axis=-1, keepdims=True) + NORM_EPS)
    vd_ref[...] = a[:, 2 * DN_WIDTH:3 * DN_WIDTH]

    zg_ref[...] = _silu(_dot(hb, w_ref[:, C_Z:C_G]))

    gb = _dot(hb, w_ref[:, C_G:C_END])
    sp, _ = _softplus_parts(gb + dt_ref[...])
    gate_ref[...] = jnp.where(lane < DN_HEADS, jax.nn.sigmoid(gb), -jnp.exp(al_ref[...]) * sp)


def _inproj(x, sc, sh, gn, w1, gq, gk, wc, al, dt, bufp, *, tm):
    grouped = bufp is not None
    d = x.shape[-1]
    if grouped:
        n = x.shape[0]
        grid = (n // tm,)
        row = lambda c: pl.BlockSpec((tm, c), lambda i: (i, 0))
        in_specs = [row(d), row(d), row(d)]
        lead = (n,)
        sem = ("arbitrary",)
        nb_shape, nb_spec = (n, 3 * DN_WIDTH), row(3 * DN_WIDTH)
        scratch = []
    else:
        b, t, _ = x.shape
        grid = (b, t // tm)
        row = lambda c: pl.BlockSpec((None, tm, c), lambda i, j: (i, j, 0))
        per_b = pl.BlockSpec((None, 1, d), lambda i, j: (i, 0, 0))
        in_specs = [row(d), per_b, per_b]
        lead = (b, t)
        sem = ("parallel", "arbitrary")
        nb_shape = (b, SUBLANES, 3 * DN_WIDTH)
        nb_spec = pl.BlockSpec((None, SUBLANES, 3 * DN_WIDTH), lambda i, j: (i, 0, 0))
        scratch = [pltpu.VMEM((SUBLANES, 3 * DN_WIDTH), F32)]
    in_specs += [_resident(gn.shape), _resident(w1.shape), _resident(gq.shape), _resident(gk.shape),
                 _resident(wc.shape), _resident(al.shape), _resident(dt.shape)]
    args = [x, sc, sh, gn, w1, gq, gk, wc, al, dt]
    if grouped:
        in_specs.append(row(3 * DN_WIDTH))
        args.append(bufp)
    out_cols = [(SB_WIDTH, BF16)] * 3 + [(SB_WIDTH, F32)] * 2 + [(DN_WIDTH, F32)] * 4 + [(LANES, F32)]
    out_shape = [jax.ShapeDtypeStruct(lead + (c,), dt_) for c, dt_ in out_cols]
    out_specs = [row(c) for c, _ in out_cols]
    out_shape.append(jax.ShapeDtypeStruct(nb_shape, F32))
    out_specs.append(nb_spec)
    return pl.pallas_call(
        functools.partial(_inproj_kernel, grouped=grouped),
        grid=grid, in_specs=in_specs, out_specs=out_specs, out_shape=out_shape,
        scratch_shapes=scratch,
        compiler_params=_cparams(sem),
        name="inproj_grouped" if grouped else "inproj_seq",
    )(*args)


def _sb_prompt_kernel(bias_ref, q_ref, k_ref, v_ref, g_ref, o_ref, acc_ref, run_ref, *, tb):
    p = pl.program_id(1)
    i = pl.program_id(2)
    lane = lax.broadcasted_iota(jnp.int32, (1, LANES), 1)
    lo = lane < SB_DIM
    q = q_ref[...]
    zero = jnp.zeros_like(q)
    q_heads = (jnp.where(lo, q, zero), jnp.where(lo, zero, q))
    row = lax.broadcasted_iota(jnp.int32, (tb, tb), 0)
    col = lax.broadcasted_iota(jnp.int32, (tb, tb), 1)
    later_keys = jnp.where(row > col, 1.0, 0.0).astype(BF16)
    valid = col < row

    acc_ref[...] = jnp.zeros_like(acc_ref)
    run_ref[...] = jnp.zeros_like(run_ref)

    def block(j, masked):
        start = pl.multiple_of(j * tb, tb)
        ks = k_ref[pl.ds(start, tb), :]
        vs = v_ref[pl.ds(start, tb), :]
        for a in range(2):
            z = _dot_nt(q_heads[a], ks) + bias_ref[2 * p + a]
            sp, lb = _softplus_parts(z)
            if masked:
                sp = jnp.where(valid, sp, 0.0)
            arg = lb - _dot(sp.astype(BF16), later_keys) - run_ref[a]
            att = jnp.exp(arg)
            if masked:
                att = jnp.where(valid, att, 0.0)
            acc_ref[a] += _dot(att.astype(BF16), vs)
            run_ref[a] += jnp.sum(sp, axis=-1, keepdims=True)

    block(i, True)

    def body(jj, carry):
        block(i - 1 - jj, False)
        return carry

    lax.fori_loop(0, i, body, 0)

    o = jnp.where(lo, acc_ref[0], acc_ref[1])
    o_ref[...] = (_half_lane_rms(o, lo) * g_ref[...]).astype(o_ref.dtype)


def _sb_prompt(qb, kb, vb, bias, g2, *, tb):
    b, t, w = qb.shape
    pairs = w // LANES
    return pl.pallas_call(
        functools.partial(_sb_prompt_kernel, tb=tb),
        grid_spec=pltpu.PrefetchScalarGridSpec(
            num_scalar_prefetch=0,
            grid=(b, pairs, t // tb),
            in_specs=[pl.BlockSpec(memory_space=pltpu.SMEM),
                      pl.BlockSpec((None, tb, LANES), lambda i, p, j: (i, j, p)),
                      pl.BlockSpec((None, t, LANES), lambda i, p, j: (i, 0, p)),
                      pl.BlockSpec((None, t, LANES), lambda i, p, j: (i, 0, p)),
                      pl.BlockSpec((1, LANES), lambda i, p, j: (0, 0))],
            out_specs=pl.BlockSpec((None, tb, LANES), lambda i, p, j: (i, j, p)),
            scratch_shapes=[pltpu.VMEM((2, tb, LANES), F32), pltpu.VMEM((2, tb, 1), F32)]),
        out_shape=jax.ShapeDtypeStruct((b, t, w), BF16),
        compiler_params=_cparams(("parallel", "parallel", "arbitrary")),
        name="sb_attn_prompt",
    )(bias, qb, kb, vb, g2)


def _sb_sample_kernel(*refs, pps, n_q):
    pt_ref, bias_ref, q_ref, kn_ref, vn_ref, g_ref = refs[:6]
    kp_refs = refs[6:6 + pps]
    vp_refs = refs[6 + pps:6 + 2 * pps]
    o_ref = refs[6 + 2 * pps]
    qx_ref, acc_ref, run_ref, newk_ref, newv_ref = refs[7 + 2 * pps:]
    del pt_ref
    j = pl.program_id(1)
    rows = SB_HEADS * n_q
    page = kp_refs[0].shape[0]
    width = SB_WIDTH
    r1 = lax.broadcasted_iota(jnp.int32, (rows, 1), 0)
    later_keys = jnp.where(lax.broadcasted_iota(jnp.int32, (page, page), 0)
                           > lax.broadcasted_iota(jnp.int32, (page, page), 1), 1.0, 0.0).astype(BF16)

    def block(kf, vf, masked):
        z = _dot_nt(qx_ref[...], kf.astype(BF16)) + bias_ref[...]
        sp, lb = _softplus_parts(z)
        if masked:
            valid = lax.broadcasted_iota(jnp.int32, (rows, page), 1) < (r1 & (n_q - 1))
            sp = jnp.where(valid, sp, 0.0)
        arg = lb - _dot(sp.astype(BF16), later_keys) - run_ref[...]
        att = jnp.exp(arg)
        if masked:
            att = jnp.where(valid, att, 0.0)
        acc_ref[...] += _dot(att.astype(BF16), vf.astype(BF16))
        run_ref[...] += jnp.sum(sp, axis=-1, keepdims=True)

    @pl.when(j == 0)
    def _():
        head_of_lane = lax.broadcasted_iota(jnp.int32, (rows, width), 1) // SB_DIM
        head_of_row = lax.broadcasted_iota(jnp.int32, (rows, width), 0) // n_q
        qrep = jnp.concatenate([q_ref[...].astype(F32)] * SB_HEADS, axis=0)
        qx_ref[...] = jnp.where(head_of_lane == head_of_row, qrep, 0.0).astype(BF16)
        acc_ref[...] = jnp.zeros_like(acc_ref)
        run_ref[...] = jnp.zeros_like(run_ref)
        newk_ref[...] = jnp.zeros_like(newk_ref)
        newv_ref[...] = jnp.zeros_like(newv_ref)
        newk_ref[0:n_q, :] = kn_ref[...]
        newv_ref[0:n_q, :] = vn_ref[...]
        block(newk_ref[...], newv_ref[...], True)

    for r in range(pps):
        block(kp_refs[r][...], vp_refs[r][...], False)

    @pl.when(j == pl.num_programs(1) - 1)
    def _():
        res = acc_ref[...]
        lane_head = lax.broadcasted_iota(jnp.int32, (n_q, width), 1) // SB_DIM
        o = jnp.zeros((n_q, width), F32)
        for hh in range(SB_HEADS):
            o = o + jnp.where(lane_head == hh, res[hh * n_q:(hh + 1) * n_q, :], 0.0)
        lo = lax.broadcasted_iota(jnp.int32, (1, LANES), 1) < SB_DIM
        for c in range(width // LANES):
            sl = slice(c * LANES, (c + 1) * LANES)
            o_ref[:, sl] = _half_lane_rms(o[:, sl], lo) * g_ref[:, sl]


def _sb_sample(page_table, bias_rows, q3, kn3, vn3, g_sb, cache_k, cache_v):
    bs, n_q, w = q3.shape
    n_pages = page_table.shape[1]
    page = cache_k.shape[1]
    pps = PAGES_PER_STEP
    rows = SB_HEADS * n_q

    def page_spec(r):
        return pl.BlockSpec((None, page, w),
                            lambda i, j, pt: (pt[i, n_pages - 1 - (j * pps + r)], 0, 0))

    tok = lambda: pl.BlockSpec((None, n_q, w), lambda i, j, pt: (i, 0, 0))
    const = lambda shp: pl.BlockSpec(shp, lambda i, j, pt: (0,) * len(shp))
    return pl.pallas_call(
        functools.partial(_sb_sample_kernel, pps=pps, n_q=n_q),
        grid_spec=pltpu.PrefetchScalarGridSpec(
            num_scalar_prefetch=1,
            grid=(bs, n_pages // pps),
            in_specs=[const((rows, 1)), tok(), tok(), tok(), const((1, w))]
                     + [page_spec(r) for r in range(pps)] + [page_spec(r) for r in range(pps)],
            out_specs=tok(),
            scratch_shapes=[pltpu.VMEM((rows, w), BF16), pltpu.VMEM((rows, w), F32),
                            pltpu.VMEM((rows, 1), F32), pltpu.VMEM((page, w), F32),
                            pltpu.VMEM((page, w), F32)]),
        out_shape=jax.ShapeDtypeStruct((bs, n_q, w), F32),
        compiler_params=_cparams(("parallel", "arbitrary")),
        name="sb_attn_paged",
    )(page_table, bias_rows, q3, kn3, vn3, g_sb, *([cache_k] * pps), *([cache_v] * pps))


def _split3(x):
    hi = x.astype(BF16)
    r = x - hi.astype(F32)
    mid = r.astype(BF16)
    lo = (r - mid.astype(F32)).astype(BF16)
    return hi, mid, lo


def _split2(x):
    hi = x.astype(BF16)
    return hi, (x - hi.astype(F32)).astype(BF16)


def _dn_local(q, k, v, beta, la, group):
    n = q.shape[0]
    row = lax.broadcasted_iota(jnp.int32, (n, n), 0)
    col = lax.broadcasted_iota(jnp.int32, (n, n), 1)
    shift = group.bit_length() - 1
    same = jnp.where((row >> shift) == (col >> shift), 1.0, 0.0)
    low = jnp.where(row >= col, same, 0.0)
    strict = jnp.where(row > col, same, 0.0)
    up = jnp.where(row <= col, same, 0.0)
    la_b = jnp.broadcast_to(la, (n, n))
    la_parts = _split3(la_b)
    low16 = low.astype(BF16)
    g_row = sum(_dot(low16, m) for m in la_parts)
    one = jnp.ones((n, n), BF16)
    g_col = sum(_dot(one, m) for m in _split3(la_b * up))
    same16 = same.astype(BF16)
    g_tot = sum(_dot(same16, m) for m in la_parts)
    decay = low * jnp.exp(jnp.minimum(g_row - g_col, 0.0))
    kb = k * beta
    k16 = k.astype(BF16)
    pw = -strict * (_dot_nt(kb.astype(BF16), k16) * decay)
    inv = jnp.where(row == col, 1.0, 0.0) + pw
    for _ in range(shift - 1):
        ph, pl_ = _split2(pw)
        pw = _dot(ph, ph) + (_dot(ph, pl_) + _dot(pl_, ph))
        ph, pl_ = _split2(pw)
        ih, il = _split2(inv)
        inv = inv + (_dot(ih, ph) + (_dot(ih, pl_) + _dot(il, ph)))
    eg = jnp.exp(g_row)
    rhs = jnp.concatenate([v * beta, kb * eg], axis=-1).astype(BF16)
    sol = _dot(inv.astype(BF16), rhs)
    u = sol[:, :DN_DIM]
    kcum = sol[:, DN_DIM:]
    intra = _dot_nt(q.astype(BF16), k16) * decay
    return u, kcum, intra, q * eg, k * jnp.exp(g_tot - g_row), g_tot


def _dn_seq_kernel(q_ref, k_ref, v_ref, gate_ref, o_ref, sout_ref, s_ref, *, group):
    @pl.when(pl.program_id(1) == 0)
    def _():
        s_ref[...] = jnp.zeros_like(s_ref)

    n = q_ref.shape[0]
    gates = gate_ref[...]
    for hh in range(DN_HEADS):
        sl = slice(hh * DN_DIM, (hh + 1) * DN_DIM)
        u, kcum, intra, qdec, kdec, g_tot = _dn_local(
            q_ref[:, sl], k_ref[:, sl], v_ref[:, sl],
            gates[:, hh:hh + 1], gates[:, DN_HEADS + hh:DN_HEADS + hh + 1], group)
        s = s_ref[hh]
        vnews, outs = [], []
        for c in range(n // group):
            rs = slice(c * group, (c + 1) * group)
            s16 = s.astype(BF16)
            vnew = u[rs] - _dot(kcum[rs].astype(BF16), s16)
            outs.append(_dot(qdec[rs].astype(BF16), s16))
            vnews.append(vnew)
            s = s * jnp.exp(g_tot[c * group:c * group + 1, :]) + _dot_tn(kdec[rs].astype(BF16), vnew.astype(BF16))
        s_ref[hh] = s
        vfull = jnp.concatenate(vnews, axis=0)
        o_ref[:, sl] = jnp.concatenate(outs, axis=0) + _dot(intra.astype(BF16), vfull.astype(BF16))

    @pl.when(pl.program_id(1) == pl.num_programs(1) - 1)
    def _():
        sout_ref[...] = s_ref[...]


def _dn_seq(qd, kd, vd, gates):
    b, t, w = qd.shape
    n = DN_BLOCK
    row = lambda c: pl.BlockSpec((None, n, c), lambda i, j: (i, j, 0))
    return pl.pallas_call(
        functools.partial(_dn_seq_kernel, group=min(64, n)),
        grid=(b, t // n),
        in_specs=[row(w), row(w), row(w), row(LANES)],
        out_specs=[row(w), pl.BlockSpec((None, DN_HEADS, DN_DIM, DN_DIM), lambda i, j: (i, 0, 0, 0))],
        out_shape=[jax.ShapeDtypeStruct((b, t, w), F32),
                   jax.ShapeDtypeStruct((b, DN_HEADS, DN_DIM, DN_DIM), F32)],
        scratch_shapes=[pltpu.VMEM((DN_HEADS, DN_DIM, DN_DIM), F32)],
        compiler_params=_cparams(("parallel", "arbitrary")),
        name="deltanet_seq",
    )(qd, kd, vd, gates)


def _dn_grouped_kernel(q_ref, k_ref, v_ref, gate_ref, s0_ref, o_ref, sout_ref,
                       u_ref, kc_ref, qd_ref, kd_ref, gt_ref, vn_ref, oq_ref, *, group):
    n = q_ref.shape[0]
    gates = gate_ref[...]
    row1 = lax.broadcasted_iota(jnp.int32, (n, 1), 0)
    shift = group.bit_length() - 1
    for hh in range(DN_HEADS):
        sl = slice(hh * DN_DIM, (hh + 1) * DN_DIM)
        u, kcum, intra, qdec, kdec, g_tot = _dn_local(
            q_ref[:, sl], k_ref[:, sl], v_ref[:, sl],
            gates[:, hh:hh + 1], gates[:, DN_HEADS + hh:DN_HEADS + hh + 1], group)
        u_ref[...] = u
        kc_ref[...] = kcum
        qd_ref[...] = qdec
        kd_ref[...] = kdec
        gt_ref[...] = g_tot
        vn_ref[...] = jnp.zeros_like(vn_ref)

        def per_seq(bi, carry):
            rs = pl.ds(pl.multiple_of(bi * group, group), group)
            s = s0_ref[bi, hh]
            vnew = u_ref[rs, :] - _dot(kc_ref[rs, :], s)
            oq_ref[rs, :] = _dot(qd_ref[rs, :], s)
            vn_ref[rs, :] = vnew
            mine = (row1 >> shift) == bi
            kmask = jnp.where(mine, kd_ref[...], 0.0).astype(BF16)
            sout_ref[bi, hh] = (s * jnp.exp(gt_ref[rs, :][0:1, :])
                                + _dot_tn(kmask, vn_ref[...].astype(BF16)))
            return carry

        lax.fori_loop(0, n // group, per_seq, 0)
        o_ref[:, sl] = oq_ref[...] + _dot(intra.astype(BF16), vn_ref[...].astype(BF16))


def _dn_grouped(qd, kd, vd, gates, s0, *, group):
    nrows, w = qd.shape
    n = DN_BLOCK
    per = n // group
    row = lambda c: pl.BlockSpec((n, c), lambda i: (i, 0))
    st = pl.BlockSpec((per, DN_HEADS, DN_DIM, DN_DIM), lambda i: (i, 0, 0, 0))
    return pl.pallas_call(
        functools.partial(_dn_grouped_kernel, group=group),
        grid=(nrows // n,),
        in_specs=[row(w), row(w), row(w), row(LANES), st],
        out_specs=[row(w), st],
        out_shape=[jax.ShapeDtypeStruct((nrows, w), F32), jax.ShapeDtypeStruct(s0.shape, F32)],
        scratch_shapes=[pltpu.VMEM((n, DN_DIM), F32)] * 7,
        compiler_params=_cparams(("parallel",)),
        name="deltanet_grouped",
    )(qd, kd, vd, gates, s0)


def _outproj_kernel(osb_ref, odn_ref, zg_ref, gdn_ref, x_ref, gt_ref, sc_ref, sh_ref, gn_ref, w_ref,
                    x1_ref, h2_ref):
    mix = _dot(osb_ref[...].astype(BF16), w_ref[0:SB_WIDTH, :])
    for hh in range(DN_HEADS):
        sl = slice(hh * DN_DIM, (hh + 1) * DN_DIM)
        on = _rms_rows(odn_ref[:, sl], gdn_ref[...]) * zg_ref[:, sl]
        mix = mix + _dot(on.astype(BF16), w_ref[SB_WIDTH + hh * DN_DIM:SB_WIDTH + (hh + 1) * DN_DIM, :])
    x1 = x_ref[...] + gt_ref[...] * mix
    x1_ref[...] = x1
    h2_ref[...] = (_rms_rows(x1, gn_ref[...]) * (1.0 + sc_ref[...]) + sh_ref[...]).astype(BF16)


def _mod_specs(x, tm, per_row):
    d = x.shape[-1]
    if per_row:
        n = x.shape[0]
        row = lambda c: pl.BlockSpec((tm, c), lambda i: (i, 0))
        return (n // tm,), row, row(d), (n,), ("arbitrary",)
    b, t, _ = x.shape
    row = lambda c: pl.BlockSpec((None, tm, c), lambda i, j: (i, j, 0))
    per_b = pl.BlockSpec((None, 1, d), lambda i, j: (i, 0, 0))
    return (b, t // tm), row, per_b, (b, t), ("parallel", "arbitrary")


def _outproj(osb, odn, zg, gdn, x, gt, sc, sh, gn, w, *, tm, per_row):
    d = x.shape[-1]
    grid, row, mod, lead, sem = _mod_specs(x, tm, per_row)
    return pl.pallas_call(
        _outproj_kernel,
        grid=grid,
        in_specs=[row(SB_WIDTH), row(DN_WIDTH), row(DN_WIDTH), _resident(gdn.shape), row(d), mod, mod, mod,
                  _resident(gn.shape), _resident(w.shape)],
        out_specs=[row(d), row(d)],
        out_shape=[jax.ShapeDtypeStruct(lead + (d,), F32), jax.ShapeDtypeStruct(lead + (d,), BF16)],
        compiler_params=_cparams(sem),
        name="outproj_rows" if per_row else "outproj_seq",
    )(osb, odn, zg, gdn, x, gt, sc, sh, gn, w)


def _ffn_kernel(*refs, grouped, cw):
    if grouped:
        h_ref, x1_ref, gt_ref, wu_ref, wc_ref, wd_ref, buf_ref, y_ref, nbuf_ref = refs
        carry_ref = None
    else:
        h_ref, x1_ref, gt_ref, wu_ref, wc_ref, wd_ref, y_ref, nbuf_ref, carry_ref = refs
        buf_ref = None

        @pl.when(pl.program_id(1) == 0)
        def _():
            carry_ref[...] = jnp.zeros_like(carry_ref)

    hb = h_ref[...]
    tm = hb.shape[0]
    dff = wd_ref.shape[0]
    taps = wc_ref.shape[0]
    acc = jnp.zeros(x1_ref.shape, F32)
    for c in range(dff // cw):
        halves = []
        for base in (0, dff):
            cs = slice(base + c * cw, base + (c + 1) * cw)
            cur = _dot(hb, wu_ref[:, cs])
            if grouped:
                y = _causal_conv(cur, wc_ref[:, cs], None, True, buf_ref[:, cs])
                nbuf_ref[:, cs] = pltpu.roll(cur, tm - (SUBLANES - (taps - 1)), 0)
            else:
                y = _causal_conv(cur, wc_ref[:, cs], carry_ref[:, cs], False, None)
                carry_ref[:, cs] = cur[tm - SUBLANES:tm]
                nbuf_ref[:, cs] = cur[tm - SUBLANES:tm]
            halves.append(y)
        act = (_silu(halves[1]) * halves[0]).astype(BF16)
        acc = acc + _dot(act, wd_ref[c * cw:(c + 1) * cw, :])
    y_ref[...] = x1_ref[...] + gt_ref[...] * acc


def _ffn(h2, x1, gt, wu, wc, wd, bufp, *, tm, per_row, cw=256):
    d = x1.shape[-1]
    up = wu.shape[1]
    grid, row, mod, lead, sem = _mod_specs(x1, tm, per_row)
    in_specs = [row(d), row(d), mod, _resident(wu.shape), _resident(wc.shape), _resident(wd.shape)]
    args = [h2, x1, gt, wu, wc, wd]
    if per_row:
        in_specs.append(row(up))
        args.append(bufp)
        nb_shape, nb_spec, scratch = lead + (up,), row(up), []
    else:
        nb_shape = (lead[0], SUBLANES, up)
        nb_spec = pl.BlockSpec((None, SUBLANES, up), lambda i, j: (i, 0, 0))
        scratch = [pltpu.VMEM((SUBLANES, up), F32)]
    return pl.pallas_call(
        functools.partial(_ffn_kernel, grouped=per_row, cw=cw),
        grid=grid, in_specs=in_specs,
        out_specs=[row(d), nb_spec],
        out_shape=[jax.ShapeDtypeStruct(lead + (d,), F32), jax.ShapeDtypeStruct(nb_shape, F32)],
        scratch_shapes=scratch,
        compiler_params=_cparams(sem),
        name="convffn_rows" if per_row else "convffn_seq",
    )(*args)


def _prep_weights(g_attn_norm, w_in, g_q, g_k, sb_bias, g_sb_out, w_dn_conv, a_log, dt_bias, g_dn_out,
                  w_out, g_ffn_norm, w_up, w_ffn_conv, w_down):
    d = w_in.shape[0]
    o1 = 3 * SB_WIDTH
    o2 = o1 + 3 * DN_WIDTH
    o3 = o2 + 2 * DN_HEADS
    pad = jnp.zeros((d, LANES - 2 * DN_HEADS), w_in.dtype)
    w1 = jnp.concatenate([w_in[:, :o2], w_in[:, o3:], w_in[:, o2:o3], pad], axis=1).astype(BF16)
    lane_pad = lambda v: jnp.zeros((1, LANES), F32).at[0, DN_HEADS:2 * DN_HEADS].set(v)
    return dict(
        gn1=g_attn_norm.reshape(1, d), w1=w1,
        gq=jnp.tile(g_q, SB_HEADS).reshape(1, SB_WIDTH), gk=jnp.tile(g_k, SB_HEADS).reshape(1, SB_WIDTH),
        bias=sb_bias.astype(F32), g_sb=jnp.tile(g_sb_out, SB_HEADS).reshape(1, SB_WIDTH),
        wc_dn=w_dn_conv, al=lane_pad(a_log), dt=lane_pad(dt_bias), gdn=g_dn_out.reshape(1, DN_DIM),
        w_out=w_out.astype(BF16), gn2=g_ffn_norm.reshape(1, d), w_up=w_up.astype(BF16),
        wc_ffn=w_ffn_conv, w_down=w_down.astype(BF16))


def _layer_prompt(x, mod, p):
    b, t, d = x.shape
    sh1, sc1, gt1, sh2, sc2, gt2 = [m.reshape(b, 1, d) for m in jnp.split(mod, 6, axis=-1)]
    qb, kb, vb, kf, vf, qd, kd, vd, zg, gates, dnbuf = _inproj(
        x, sc1, sh1, p["gn1"], p["w1"], p["gq"], p["gk"], p["wc_dn"], p["al"], p["dt"], None, tm=256)
    osb = _sb_prompt(qb, kb, vb, p["bias"], p["g_sb"][:, :LANES], tb=256)
    odn, s_new = _dn_seq(qd, kd, vd, gates)
    x1, h2 = _outproj(osb, odn, zg, p["gdn"], x, gt1, sc2, sh2, p["gn2"], p["w_out"], tm=256, per_row=False)
    y, ffbuf = _ffn(h2, x1, gt2, p["w_up"], p["wc_ffn"], p["w_down"], None, tm=256, per_row=False)
    k_dn = p["wc_dn"].shape[0] - 1
    k_ff = p["wc_ffn"].shape[0] - 1
    return (y, kf.reshape(b, t, SB_HEADS, SB_DIM), vf.reshape(b, t, SB_HEADS, SB_DIM), s_new,
            dnbuf[:, SUBLANES - k_dn:], ffbuf[:, SUBLANES - k_ff:])


def _layer_sample(x, mod, page_table, cache_k, cache_v, s0, dn_buf, ffn_buf, p):
    bs, n_q, d = x.shape
    assert n_q == SUBLANES
    n = bs * n_q
    rep = lambda m: jnp.broadcast_to(m[:, None, :], (bs, n_q, d)).reshape(n, d)
    sh1, sc1, gt1, sh2, sc2, gt2 = [rep(m) for m in jnp.split(mod, 6, axis=-1)]
    padrows = lambda buf: jnp.pad(buf, ((0, 0), (0, n_q - buf.shape[1]), (0, 0))).reshape(n, buf.shape[2])
    xf = x.reshape(n, d)
    qb, kb, vb, kf, vf, qd, kd, vd, zg, gates, dnbuf = _inproj(
        xf, sc1, sh1, p["gn1"], p["w1"], p["gq"], p["gk"], p["wc_dn"], p["al"], p["dt"], padrows(dn_buf), tm=128)
    del kb, vb
    w = SB_WIDTH
    bias_rows = jnp.repeat(p["bias"], n_q).reshape(SB_HEADS * n_q, 1)
    n_phys, page = cache_k.shape[0], cache_k.shape[1]
    osb = _sb_sample(page_table, bias_rows, qb.reshape(bs, n_q, w), kf.reshape(bs, n_q, w),
                     vf.reshape(bs, n_q, w), p["g_sb"], cache_k.reshape(n_phys, page, w),
                     cache_v.reshape(n_phys, page, w))
    odn, s_new = _dn_grouped(qd, kd, vd, gates, s0, group=n_q)
    x1, h2 = _outproj(osb.reshape(n, w), odn, zg, p["gdn"], xf, gt1, sc2, sh2, p["gn2"], p["w_out"],
                      tm=128, per_row=True)
    y, ffbuf = _ffn(h2, x1, gt2, p["w_up"], p["wc_ffn"], p["w_down"], padrows(ffn_buf), tm=128, per_row=True)
    k_dn = p["wc_dn"].shape[0] - 1
    k_ff = p["wc_ffn"].shape[0] - 1
    return (y.reshape(bs, n_q, d), kf.reshape(bs, n_q, SB_HEADS, SB_DIM), vf.reshape(bs, n_q, SB_HEADS, SB_DIM),
            s_new, dnbuf.reshape(bs, n_q, -1)[:, :k_dn], ffbuf.reshape(bs, n_q, -1)[:, :k_ff])


def kernel(x_prompt, x_sample, c_prompt, c_sample, cache_k, cache_v, page_table, state_delta, state_dn_conv, state_ffn_conv, w_ada, b_ada, g_attn_norm, w_in, g_q, g_k, sb_bias, g_sb_out, w_dn_conv, a_log, dt_bias, g_dn_out, w_out, g_ffn_norm, w_up, w_ffn_conv, w_down):
    depth = w_ada.shape[0]
    bp = x_prompt.shape[0]
    yp, ys = x_prompt, x_sample
    outs = [[] for _ in range(10)]
    c_all = jnp.concatenate([c_prompt, c_sample], axis=0)
    pad_rows = (-c_all.shape[0]) % SUBLANES
    c_all = jnp.pad(c_all, ((0, pad_rows), (0, 0)))
    for l in range(depth):
        p = _prep_weights(g_attn_norm[l], w_in[l], g_q[l], g_k[l], sb_bias[l], g_sb_out[l], w_dn_conv[l],
                          a_log[l], dt_bias[l], g_dn_out[l], w_out[l], g_ffn_norm[l], w_up[l],
                          w_ffn_conv[l], w_down[l])
        mod = _ada(c_all, w_ada[l], b_ada[l])
        yp, kp, vp, sp, dcp, fcp = _layer_prompt(yp, mod[:bp], p)
        ys, ks, vs, ss, dcs, fcs = _layer_sample(ys, mod[bp:bp + x_sample.shape[0]], page_table,
                                                 cache_k[l], cache_v[l], state_delta[l],
                                                 state_dn_conv[l], state_ffn_conv[l], p)
        for lst, val in zip(outs, (kp, vp, ks, vs, sp, ss, dcp, dcs, fcp, fcs)):
            lst.append(val)
    return (yp, ys) + tuple(jnp.stack(o) for o in outs)
```

```python
import functools

import jax
import jax.numpy as jnp
from jax import lax
from jax.experimental import pallas as pl
from jax.experimental.pallas import tpu as pltpu

F32 = jnp.float32
BF16 = jnp.bfloat16

NORM_EPS = 1e-6
LANES = 128
SUBLANES = 8
VMEM_LIMIT = 56 * 1024 * 1024

SB_HEADS = 8
SB_DIM = 64
SB_WIDTH = SB_HEADS * SB_DIM
DN_HEADS = 4
DN_DIM = 128
DN_WIDTH = DN_HEADS * DN_DIM
DN_BLOCK = 128
PAGES_PER_STEP = 8


def _dot(a, b):
    return jnp.dot(a, b, preferred_element_type=F32)


def _dot_nt(a, b):
    return lax.dot_general(a, b, (((1,), (1,)), ((), ())), preferred_element_type=F32)


def _dot_tn(a, b):
    return lax.dot_general(a, b, (((0,), (0,)), ((), ())), preferred_element_type=F32)


def _silu(x):
    return x * jax.nn.sigmoid(x)


def _softplus_parts(z):
    l = jnp.log(1.0 + jnp.exp(-jnp.abs(z)))
    return jnp.maximum(z, 0.0) + l, jnp.minimum(z, 0.0) - l


def _rms_rows(x, g):
    ms = jnp.mean(x * x, axis=-1, keepdims=True)
    return x * lax.rsqrt(ms + NORM_EPS) * g


def _half_lane_rms(blk, lo):
    sq = blk * blk
    s_lo = jnp.sum(jnp.where(lo, sq, 0.0), axis=-1, keepdims=True)
    s_hi = jnp.sum(jnp.where(lo, 0.0, sq), axis=-1, keepdims=True)
    ms = jnp.where(lo, s_lo, s_hi) * (1.0 / SB_DIM)
    return blk * lax.rsqrt(ms + NORM_EPS)


def _cparams(sem):
    return pltpu.CompilerParams(dimension_semantics=sem, vmem_limit_bytes=VMEM_LIMIT)


def _resident(shape):
    nd = len(shape)
    return pl.BlockSpec(shape, lambda *_: (0,) * nd, pipeline_mode=pl.Buffered(1))


def _ada_kernel(c_ref, w_ref, b_ref, o_ref):
    a = _silu(c_ref[...]).astype(BF16)
    o_ref[...] = _dot(a, w_ref[...].astype(BF16)) + b_ref[...]


def _ada(c, w, b):
    m, d = c.shape
    n = w.shape[1]
    tn = 512
    return pl.pallas_call(
        _ada_kernel,
        grid=(n // tn,),
        in_specs=[pl.BlockSpec((m, d), lambda j: (0, 0)),
                  pl.BlockSpec((d, tn), lambda j: (0, j)),
                  pl.BlockSpec((1, tn), lambda j: (0, j))],
        out_specs=pl.BlockSpec((m, tn), lambda j: (0, j)),
        out_shape=jax.ShapeDtypeStruct((m, n), F32),
        compiler_params=_cparams(("parallel",)),
        name="adaln_mod",
    )(c, w, b.reshape(1, n))


C_SB = 0
C_DN = 3 * SB_WIDTH
C_Z = C_DN + 3 * DN_WIDTH
C_G = C_Z + DN_WIDTH
C_END = C_G + LANES


def _causal_conv(cur, taps, prev_rows, grouped, bufp):
    w = taps.shape[0]
    tm = cur.shape[0]
    y = taps[w - 1:w, :] * cur
    if grouped:
        rowm = lax.broadcasted_iota(jnp.int32, (tm, 1), 0) & (SUBLANES - 1)
        for s in range(1, w):
            r = pltpu.roll(cur, s, 0)
            back = (w - 1) - s
            bs = bufp if back == 0 else pltpu.roll(bufp, tm - back, 0)
            y = y + taps[w - 1 - s:w - s, :] * jnp.where(rowm >= s, r, bs)
        return y
    row8 = lax.broadcasted_iota(jnp.int32, (SUBLANES, 1), 0)
    yh = taps[w - 1:w, :] * cur[0:SUBLANES]
    for s in range(1, w):
        r = pltpu.roll(cur, s, 0)
        y = y + taps[w - 1 - s:w - s, :] * r
        head = jnp.where(row8 < s, pltpu.roll(prev_rows, s, 0), r[0:SUBLANES])
        yh = yh + taps[w - 1 - s:w - s, :] * head
    return jnp.concatenate([yh, y[SUBLANES:]], axis=0)


def _inproj_kernel(*refs, grouped):
    if grouped:
        (x_ref, sc_ref, sh_ref, gn_ref, w_ref, gq_ref, gk_ref, wc_ref, al_ref, dt_ref, buf_ref,
         qb_ref, kb_ref, vb_ref, kf_ref, vf_ref, qd_ref, kd_ref, vd_ref, zg_ref, gate_ref, nbuf_ref) = refs
        carry_ref = None
    else:
        (x_ref, sc_ref, sh_ref, gn_ref, w_ref, gq_ref, gk_ref, wc_ref, al_ref, dt_ref,
         qb_ref, kb_ref, vb_ref, kf_ref, vf_ref, qd_ref, kd_ref, vd_ref, zg_ref, gate_ref, nbuf_ref,
         carry_ref) = refs
        buf_ref = None

        @pl.when(pl.program_id(1) == 0)
        def _():
            carry_ref[...] = jnp.zeros_like(carry_ref)

    x = x_ref[...]
    h = _rms_rows(x, gn_ref[...]) * (1.0 + sc_ref[...]) + sh_ref[...]
    hb = h.astype(BF16)
    tm = x.shape[0]
    lane = lax.broadcasted_iota(jnp.int32, (1, LANES), 1)
    lo = lane < SB_DIM

    for c in range(SB_WIDTH // LANES):
        sl = slice(c * LANES, (c + 1) * LANES)
        pq = _dot(hb, w_ref[:, C_SB + c * LANES:C_SB + (c + 1) * LANES])
        qn = _half_lane_rms(pq, lo) * gq_ref[:, sl]
        qb_ref[:, sl] = (qn * (SB_DIM ** -0.5)).astype(BF16)
        pk = _dot(hb, w_ref[:, C_SB + SB_WIDTH + c * LANES:C_SB + SB_WIDTH + (c + 1) * LANES])
        kn = _half_lane_rms(pk, lo) * gk_ref[:, sl]
        kf_ref[:, sl] = kn
        kb_ref[:, sl] = kn.astype(BF16)
    pv = _dot(hb, w_ref[:, C_SB + 2 * SB_WIDTH:C_SB + 3 * SB_WIDTH])
    vf_ref[...] = pv
    vb_ref[...] = pv.astype(BF16)

    cur = _dot(hb, w_ref[:, C_DN:C_Z])
    if grouped:
        y = _causal_conv(cur, wc_ref[...], None, True, buf_ref[...])
        nbuf_ref[...] = pltpu.roll(cur, tm - (SUBLANES - (wc_ref.shape[0] - 1)), 0)
    else:
        y = _causal_conv(cur, wc_ref[...], carry_ref[...], False, None)
        carry_ref[...] = cur[tm - SUBLANES:tm]
        nbuf_ref[...] = cur[tm - SUBLANES:tm]
    a = _silu(y)
    for hh in range(DN_HEADS):
        sl = slice(hh * DN_DIM, (hh + 1) * DN_DIM)
        qh = a[:, hh * DN_DIM:(hh + 1) * DN_DIM]
        kh = a[:, DN_WIDTH + hh * DN_DIM:DN_WIDTH + (hh + 1) * DN_DIM]
        qd_ref[:, sl] = qh * (lax.rsqrt(jnp.sum(qh * qh, axis=-1, keepdims=True) + NORM_EPS) * (DN_DIM ** -0.5))
        kd_ref[:, sl] = kh * lax.rsqrt(jnp.sum(kh * kh, axis=-1, keepdims=True) + NORM_EPS)
    vd_ref[...] = a[:, 2 * DN_WIDTH:3 * DN_WIDTH]

    zg_ref[...] = _silu(_dot(hb, w_ref[:, C_Z:C_G]))

    gb = _dot(hb, w_ref[:, C_G:C_END])
    sp, _ = _softplus_parts(gb + dt_ref[...])
    gate_ref[...] = jnp.where(lane < DN_HEADS, jax.nn.sigmoid(gb), -jnp.exp(al_ref[...]) * sp)


def _inproj(x, sc, sh, gn, w1, gq, gk, wc, al, dt, bufp, *, tm):
    grouped = bufp is not None
    d = x.shape[-1]
    if grouped:
        n = x.shape[0]
        grid = (n // tm,)
        row = lambda c: pl.BlockSpec((tm, c), lambda i: (i, 0))
        in_specs = [row(d), row(d), row(d)]
        lead = (n,)
        sem = ("arbitrary",)
        nb_shape, nb_spec = (n, 3 * DN_WIDTH), row(3 * DN_WIDTH)
        scratch = []
    else:
        b, t, _ = x.shape
        grid = (b, t // tm)
        row = lambda c: pl.BlockSpec((None, tm, c), lambda i, j: (i, j, 0))
        per_b = pl.BlockSpec((None, 1, d), lambda i, j: (i, 0, 0))
        in_specs = [row(d), per_b, per_b]
        lead = (b, t)
        sem = ("parallel", "arbitrary")
        nb_shape = (b, SUBLANES, 3 * DN_WIDTH)
        nb_spec = pl.BlockSpec((None, SUBLANES, 3 * DN_WIDTH), lambda i, j: (i, 0, 0))
        scratch = [pltpu.VMEM((SUBLANES, 3 * DN_WIDTH), F32)]
    in_specs += [_resident(gn.shape), _resident(w1.shape), _resident(gq.shape), _resident(gk.shape),
                 _resident(wc.shape), _resident(al.shape), _resident(dt.shape)]
    args = [x, sc, sh, gn, w1, gq, gk, wc, al, dt]
    if grouped:
        in_specs.append(row(3 * DN_WIDTH))
        args.append(bufp)
    out_cols = [(SB_WIDTH, BF16)] * 3 + [(SB_WIDTH, F32)] * 2 + [(DN_WIDTH, F32)] * 4 + [(LANES, F32)]
    out_shape = [jax.ShapeDtypeStruct(lead + (c,), dt_) for c, dt_ in out_cols]
    out_specs = [row(c) for c, _ in out_cols]
    out_shape.append(jax.ShapeDtypeStruct(nb_shape, F32))
    out_specs.append(nb_spec)
    return pl.pallas_call(
        functools.partial(_inproj_kernel, grouped=grouped),
        grid=grid, in_specs=in_specs, out_specs=out_specs, out_shape=out_shape,
        scratch_shapes=scratch,
        compiler_params=_cparams(sem),
        name="inproj_grouped" if grouped else "inproj_seq",
    )(*args)


def _sb_prompt_kernel(bias_ref, q_ref, k_ref, v_ref, g_ref, o_ref, acc_ref, run_ref, *, tb):
    p = pl.program_id(1)
    i = pl.program_id(2)
    lane = lax.broadcasted_iota(jnp.int32, (1, LANES), 1)
    lo = lane < SB_DIM
    q = q_ref[...]
    zero = jnp.zeros_like(q)
    q_heads = (jnp.where(lo, q, zero), jnp.where(lo, zero, q))
    row = lax.broadcasted_iota(jnp.int32, (tb, tb), 0)
    col = lax.broadcasted_iota(jnp.int32, (tb, tb), 1)
    later_keys = jnp.where(row > col, 1.0, 0.0).astype(BF16)
    valid = col < row

    acc_ref[...] = jnp.zeros_like(acc_ref)
    run_ref[...] = jnp.zeros_like(run_ref)

    def block(j, masked):
        start = pl.multiple_of(j * tb, tb)
        ks = k_ref[pl.ds(start, tb), :]
        vs = v_ref[pl.ds(start, tb), :]
        for a in range(2):
            z = _dot_nt(q_heads[a], ks) + bias_ref[2 * p + a]
            sp, lb = _softplus_parts(z)
            if masked:
                sp = jnp.where(valid, sp, 0.0)
            arg = lb - _dot(sp.astype(BF16), later_keys) - run_ref[a]
            att = jnp.exp(arg)
            if masked:
                att = jnp.where(valid, att, 0.0)
            acc_ref[a] += _dot(att.astype(BF16), vs)
            run_ref[a] += jnp.sum(sp, axis=-1, keepdims=True)

    block(i, True)

    def body(jj, carry):
        block(i - 1 - jj, False)
        return carry

    lax.fori_loop(0, i, body, 0)

    o = jnp.where(lo, acc_ref[0], acc_ref[1])
    o_ref[...] = (_half_lane_rms(o, lo) * g_ref[...]).astype(o_ref.dtype)


def _sb_prompt(qb, kb, vb, bias, g2, *, tb):
    b, t, w = qb.shape
    pairs = w // LANES
    return pl.pallas_call(
        functools.partial(_sb_prompt_kernel, tb=tb),
        grid_spec=pltpu.PrefetchScalarGridSpec(
            num_scalar_prefetch=0,
            grid=(b, pairs, t // tb),
            in_specs=[pl.BlockSpec(memory_space=pltpu.SMEM),
                      pl.BlockSpec((None, tb, LANES), lambda i, p, j: (i, j, p)),
                      pl.BlockSpec((None, t, LANES), lambda i, p, j: (i, 0, p)),
                      pl.BlockSpec((None, t, LANES), lambda i, p, j: (i, 0, p)),
                      pl.BlockSpec((1, LANES), lambda i, p, j: (0, 0))],
            out_specs=pl.BlockSpec((None, tb, LANES), lambda i, p, j: (i, j, p)),
            scratch_shapes=[pltpu.VMEM((2, tb, LANES), F32), pltpu.VMEM((2, tb, 1), F32)]),
        out_shape=jax.ShapeDtypeStruct((b, t, w), BF16),
        compiler_params=_cparams(("parallel", "parallel", "arbitrary")),
        name="sb_attn_prompt",
    )(bias, qb, kb, vb, g2)


def _sb_sample_kernel(*refs, pps, n_q):
    pt_ref, bias_ref, q_ref, kn_ref, vn_ref, g_ref = refs[:6]
    kp_refs = refs[6:6 + pps]
    vp_refs = refs[6 + pps:6 + 2 * pps]
    o_ref = refs[6 + 2 * pps]
    qx_ref, acc_ref, run_ref, newk_ref, newv_ref = refs[7 + 2 * pps:]
    del pt_ref
    j = pl.program_id(1)
    rows = SB_HEADS * n_q
    page = kp_refs[0].shape[1]
    width = SB_WIDTH
    r1 = lax.broadcasted_iota(jnp.int32, (rows, 1), 0)
    later_keys = jnp.where(lax.broadcasted_iota(jnp.int32, (page, page), 0)
                           > lax.broadcasted_iota(jnp.int32, (page, page), 1), 1.0, 0.0).astype(BF16)

    def block(kf, vf, new_tokens):
        k16 = kf.astype(BF16)
        v16 = vf.astype(BF16)
        z = (_dot_nt(qx_ref[...], k16) if new_tokens else _dot(qx_ref[...], k16)) + bias_ref[...]
        sp, lb = _softplus_parts(z)
        if new_tokens:
            valid = lax.broadcasted_iota(jnp.int32, (rows, page), 1) < (r1 & (n_q - 1))
            sp = jnp.where(valid, sp, 0.0)
        arg = lb - _dot(sp.astype(BF16), later_keys) - run_ref[...]
        att = jnp.exp(arg)
        if new_tokens:
            att = jnp.where(valid, att, 0.0)
        a16 = att.astype(BF16)
        acc_ref[...] += _dot(a16, v16) if new_tokens else _dot_nt(a16, v16)
        run_ref[...] += jnp.sum(sp, axis=-1, keepdims=True)

    @pl.when(j == 0)
    def _():
        head_of_lane = lax.broadcasted_iota(jnp.int32, (rows, width), 1) // SB_DIM
        head_of_row = lax.broadcasted_iota(jnp.int32, (rows, width), 0) // n_q
        qrep = jnp.concatenate([q_ref[...].astype(F32)] * SB_HEADS, axis=0)
        qx_ref[...] = jnp.where(head_of_lane == head_of_row, qrep, 0.0).astype(BF16)
        acc_ref[...] = jnp.zeros_like(acc_ref)
        run_ref[...] = jnp.zeros_like(run_ref)
        newk_ref[...] = jnp.zeros_like(newk_ref)
        newv_ref[...] = jnp.zeros_like(newv_ref)
        newk_ref[0:n_q, :] = kn_ref[...]
        newv_ref[0:n_q, :] = vn_ref[...]
        block(newk_ref[...], newv_ref[...], True)

    for r in range(pps):
        block(kp_refs[r][...], vp_refs[r][...], False)

    @pl.when(j == pl.num_programs(1) - 1)
    def _():
        res = acc_ref[...]
        lane_head = lax.broadcasted_iota(jnp.int32, (n_q, width), 1) // SB_DIM
        o = jnp.zeros((n_q, width), F32)
        for hh in range(SB_HEADS):
            o = o + jnp.where(lane_head == hh, res[hh * n_q:(hh + 1) * n_q, :], 0.0)
        lo = lax.broadcasted_iota(jnp.int32, (1, LANES), 1) < SB_DIM
        for c in range(width // LANES):
            sl = slice(c * LANES, (c + 1) * LANES)
            o_ref[:, sl] = _half_lane_rms(o[:, sl], lo) * g_ref[:, sl]


def _sb_sample(page_table, bias_rows, q3, kn3, vn3, g_sb, cache_k, cache_v):
    bs, n_q, w = q3.shape
    n_pages = page_table.shape[1]
    page = cache_k.shape[2]
    pps = PAGES_PER_STEP
    rows = SB_HEADS * n_q

    def page_spec(r):
        return pl.BlockSpec((None, w, page),
                            lambda i, j, pt: (pt[i, n_pages - 1 - (j * pps + r)], 0, 0))

    tok = lambda: pl.BlockSpec((None, n_q, w), lambda i, j, pt: (i, 0, 0))
    const = lambda shp: pl.BlockSpec(shp, lambda i, j, pt: (0,) * len(shp))
    return pl.pallas_call(
        functools.partial(_sb_sample_kernel, pps=pps, n_q=n_q),
        grid_spec=pltpu.PrefetchScalarGridSpec(
            num_scalar_prefetch=1,
            grid=(bs, n_pages // pps),
            in_specs=[const((rows, 1)), tok(), tok(), tok(), const((1, w))]
                     + [page_spec(r) for r in range(pps)] + [page_spec(r) for r in range(pps)],
            out_specs=tok(),
            scratch_shapes=[pltpu.VMEM((rows, w), BF16), pltpu.VMEM((rows, w), F32),
                            pltpu.VMEM((rows, 1), F32), pltpu.VMEM((page, w), F32),
                            pltpu.VMEM((page, w), F32)]),
        out_shape=jax.ShapeDtypeStruct((bs, n_q, w), F32),
        compiler_params=_cparams(("parallel", "arbitrary")),
        name="sb_attn_paged",
    )(page_table, bias_rows, q3, kn3, vn3, g_sb, *([cache_k] * pps), *([cache_v] * pps))


def _split3(x):
    hi = x.astype(BF16)
    r = x - hi.astype(F32)
    mid = r.astype(BF16)
    lo = (r - mid.astype(F32)).astype(BF16)
    return hi, mid, lo


def _split2(x):
    hi = x.astype(BF16)
    return hi, (x - hi.astype(F32)).astype(BF16)


def _dn_local(q, k, v, beta, la, group):
    n = q.shape[0]
    row = lax.broadcasted_iota(jnp.int32, (n, n), 0)
    col = lax.broadcasted_iota(jnp.int32, (n, n), 1)
    shift = group.bit_length() - 1
    same = jnp.where((row >> shift) == (col >> shift), 1.0, 0.0)
    low = jnp.where(row >= col, same, 0.0)
    strict = jnp.where(row > col, same, 0.0)
    up = jnp.where(row <= col, same, 0.0)
    la_b = jnp.broadcast_to(la, (n, n))
    la_parts = _split3(la_b)
    low16 = low.astype(BF16)
    g_row = sum(_dot(low16, m) for m in la_parts)
    one = jnp.ones((n, n), BF16)
    g_col = sum(_dot(one, m) for m in _split3(la_b * up))
    same16 = same.astype(BF16)
    g_tot = sum(_dot(same16, m) for m in la_parts)
    decay = low * jnp.exp(jnp.minimum(g_row - g_col, 0.0))
    kb = k * beta
    k16 = k.astype(BF16)
    pw = -strict * (_dot_nt(kb.astype(BF16), k16) * decay)
    inv = jnp.where(row == col, 1.0, 0.0) + pw
    for _ in range(shift - 1):
        ph, pl_ = _split2(pw)
        pw = _dot(ph, ph) + (_dot(ph, pl_) + _dot(pl_, ph))
        ph, pl_ = _split2(pw)
        ih, il = _split2(inv)
        inv = inv + (_dot(ih, ph) + (_dot(ih, pl_) + _dot(il, ph)))
    eg = jnp.exp(g_row)
    rhs = jnp.concatenate([v * beta, kb * eg], axis=-1).astype(BF16)
    sol = _dot(inv.astype(BF16), rhs)
    u = sol[:, :DN_DIM]
    kcum = sol[:, DN_DIM:]
    intra = _dot_nt(q.astype(BF16), k16) * decay
    return u, kcum, intra, q * eg, k * jnp.exp(g_tot - g_row), g_tot


def _dn_seq_kernel(q_ref, k_ref, v_ref, gate_ref, o_ref, sout_ref, s_ref, *, group):
    @pl.when(pl.program_id(1) == 0)
    def _():
        s_ref[...] = jnp.zeros_like(s_ref)

    n = q_ref.shape[0]
    gates = gate_ref[...]
    for hh in range(DN_HEADS):
        sl = slice(hh * DN_DIM, (hh + 1) * DN_DIM)
        u, kcum, intra, qdec, kdec, g_tot = _dn_local(
            q_ref[:, sl], k_ref[:, sl], v_ref[:, sl],
            gates[:, hh:hh + 1], gates[:, DN_HEADS + hh:DN_HEADS + hh + 1], group)
        s = s_ref[hh]
        vnews, outs = [], []
        for c in range(n // group):
            rs = slice(c * group, (c + 1) * group)
            s16 = s.astype(BF16)
            vnew = u[rs] - _dot(kcum[rs].astype(BF16), s16)
            outs.append(_dot(qdec[rs].astype(BF16), s16))
            vnews.append(vnew)
            s = s * jnp.exp(g_tot[c * group:c * group + 1, :]) + _dot_tn(kdec[rs].astype(BF16), vnew.astype(BF16))
        s_ref[hh] = s
        vfull = jnp.concatenate(vnews, axis=0)
        o_ref[:, sl] = jnp.concatenate(outs, axis=0) + _dot(intra.astype(BF16), vfull.astype(BF16))

    @pl.when(pl.program_id(1) == pl.num_programs(1) - 1)
    def _():
        sout_ref[...] = s_ref[...]


def _dn_seq(qd, kd, vd, gates):
    b, t, w = qd.shape
    n = DN_BLOCK
    row = lambda c: pl.BlockSpec((None, n, c), lambda i, j: (i, j, 0))
    return pl.pallas_call(
        functools.partial(_dn_seq_kernel, group=min(64, n)),
        grid=(b, t // n),
        in_specs=[row(w), row(w), row(w), row(LANES)],
        out_specs=[row(w), pl.BlockSpec((None, DN_HEADS, DN_DIM, DN_DIM), lambda i, j: (i, 0, 0, 0))],
        out_shape=[jax.ShapeDtypeStruct((b, t, w), F32),
                   jax.ShapeDtypeStruct((b, DN_HEADS, DN_DIM, DN_DIM), F32)],
        scratch_shapes=[pltpu.VMEM((DN_HEADS, DN_DIM, DN_DIM), F32)],
        compiler_params=_cparams(("parallel", "arbitrary")),
        name="deltanet_seq",
    )(qd, kd, vd, gates)


def _dn_grouped_kernel(q_ref, k_ref, v_ref, gate_ref, s0_ref, o_ref, sout_ref,
                       u_ref, kc_ref, qd_ref, kd_ref, gt_ref, vn_ref, oq_ref, *, group):
    n = q_ref.shape[0]
    gates = gate_ref[...]
    row1 = lax.broadcasted_iota(jnp.int32, (n, 1), 0)
    shift = group.bit_length() - 1
    for hh in range(DN_HEADS):
        sl = slice(hh * DN_DIM, (hh + 1) * DN_DIM)
        u, kcum, intra, qdec, kdec, g_tot = _dn_local(
            q_ref[:, sl], k_ref[:, sl], v_ref[:, sl],
            gates[:, hh:hh + 1], gates[:, DN_HEADS + hh:DN_HEADS + hh + 1], group)
        u_ref[...] = u
        kc_ref[...] = kcum
        qd_ref[...] = qdec
        kd_ref[...] = kdec
        gt_ref[...] = g_tot
        vn_ref[...] = jnp.zeros_like(vn_ref)

        def per_seq(bi, carry):
            rs = pl.ds(pl.multiple_of(bi * group, group), group)
            s = s0_ref[bi, hh]
            vnew = u_ref[rs, :] - _dot(kc_ref[rs, :], s)
            oq_ref[rs, :] = _dot(qd_ref[rs, :], s)
            vn_ref[rs, :] = vnew
            mine = (row1 >> shift) == bi
            kmask = jnp.where(mine, kd_ref[...], 0.0).astype(BF16)
            sout_ref[bi, hh] = (s * jnp.exp(gt_ref[rs, :][0:1, :])
                                + _dot_tn(kmask, vn_ref[...].astype(BF16)))
            return carry

        lax.fori_loop(0, n // group, per_seq, 0)
        o_ref[:, sl] = oq_ref[...] + _dot(intra.astype(BF16), vn_ref[...].astype(BF16))


def _dn_grouped(qd, kd, vd, gates, s0, *, group):
    nrows, w = qd.shape
    n = DN_BLOCK
    per = n // group
    row = lambda c: pl.BlockSpec((n, c), lambda i: (i, 0))
    st = pl.BlockSpec((per, DN_HEADS, DN_DIM, DN_DIM), lambda i: (i, 0, 0, 0))
    return pl.pallas_call(
        functools.partial(_dn_grouped_kernel, group=group),
        grid=(nrows // n,),
        in_specs=[row(w), row(w), row(w), row(LANES), st],
        out_specs=[row(w), st],
        out_shape=[jax.ShapeDtypeStruct((nrows, w), F32), jax.ShapeDtypeStruct(s0.shape, F32)],
        scratch_shapes=[pltpu.VMEM((n, DN_DIM), F32)] * 7,
        compiler_params=_cparams(("parallel",)),
        name="deltanet_grouped",
    )(qd, kd, vd, gates, s0)


def _outproj_kernel(osb_ref, odn_ref, zg_ref, gdn_ref, x_ref, gt_ref, sc_ref, sh_ref, gn_ref, w_ref,
                    x1_ref, h2_ref):
    mix = _dot(osb_ref[...].astype(BF16), w_ref[0:SB_WIDTH, :])
    for hh in range(DN_HEADS):
        sl = slice(hh * DN_DIM, (hh + 1) * DN_DIM)
        on = _rms_rows(odn_ref[:, sl], gdn_ref[...]) * zg_ref[:, sl]
        mix = mix + _dot(on.astype(BF16), w_ref[SB_WIDTH + hh * DN_DIM:SB_WIDTH + (hh + 1) * DN_DIM, :])
    x1 = x_ref[...] + gt_ref[...] * mix
    x1_ref[...] = x1
    h2_ref[...] = (_rms_rows(x1, gn_ref[...]) * (1.0 + sc_ref[...]) + sh_ref[...]).astype(BF16)


def _mod_specs(x, tm, per_row):
    d = x.shape[-1]
    if per_row:
        n = x.shape[0]
        row = lambda c: pl.BlockSpec((tm, c), lambda i: (i, 0))
        return (n // tm,), row, row(d), (n,), ("arbitrary",)
    b, t, _ = x.shape
    row = lambda c: pl.BlockSpec((None, tm, c), lambda i, j: (i, j, 0))
    per_b = pl.BlockSpec((None, 1, d), lambda i, j: (i, 0, 0))
    return (b, t // tm), row, per_b, (b, t), ("parallel", "arbitrary")


def _outproj(osb, odn, zg, gdn, x, gt, sc, sh, gn, w, *, tm, per_row):
    d = x.shape[-1]
    grid, row, mod, lead, sem = _mod_specs(x, tm, per_row)
    return pl.pallas_call(
        _outproj_kernel,
        grid=grid,
        in_specs=[row(SB_WIDTH), row(DN_WIDTH), row(DN_WIDTH), _resident(gdn.shape), row(d), mod, mod, mod,
                  _resident(gn.shape), _resident(w.shape)],
        out_specs=[row(d), row(d)],
        out_shape=[jax.ShapeDtypeStruct(lead + (d,), F32), jax.ShapeDtypeStruct(lead + (d,), BF16)],
        compiler_params=_cparams(sem),
        name="outproj_rows" if per_row else "outproj_seq",
    )(osb, odn, zg, gdn, x, gt, sc, sh, gn, w)


def _ffn_kernel(*refs, grouped, cw):
    if grouped:
        h_ref, x1_ref, gt_ref, wu_ref, wc_ref, wd_ref, buf_ref, y_ref, nbuf_ref = refs
        carry_ref = None
    else:
        h_ref, x1_ref, gt_ref, wu_ref, wc_ref, wd_ref, y_ref, nbuf_ref, carry_ref = refs
        buf_ref = None

        @pl.when(pl.program_id(1) == 0)
        def _():
            carry_ref[...] = jnp.zeros_like(carry_ref)

    hb = h_ref[...]
    tm = hb.shape[0]
    dff = wd_ref.shape[0]
    taps = wc_ref.shape[0]
    acc = jnp.zeros(x1_ref.shape, F32)
    for c in range(dff // cw):
        halves = []
        for base in (0, dff):
            cs = slice(base + c * cw, base + (c + 1) * cw)
            cur = _dot(hb, wu_ref[:, cs])
            if grouped:
                y = _causal_conv(cur, wc_ref[:, cs], None, True, buf_ref[:, cs])
                nbuf_ref[:, cs] = pltpu.roll(cur, tm - (SUBLANES - (taps - 1)), 0)
            else:
                y = _causal_conv(cur, wc_ref[:, cs], carry_ref[:, cs], False, None)
                carry_ref[:, cs] = cur[tm - SUBLANES:tm]
                nbuf_ref[:, cs] = cur[tm - SUBLANES:tm]
            halves.append(y)
        act = (_silu(halves[1]) * halves[0]).astype(BF16)
        acc = acc + _dot(act, wd_ref[c * cw:(c + 1) * cw, :])
    y_ref[...] = x1_ref[...] + gt_ref[...] * acc


def _ffn(h2, x1, gt, wu, wc, wd, bufp, *, tm, per_row, cw=256):
    d = x1.shape[-1]
    up = wu.shape[1]
    grid, row, mod, lead, sem = _mod_specs(x1, tm, per_row)
    in_specs = [row(d), row(d), mod, _resident(wu.shape), _resident(wc.shape), _resident(wd.shape)]
    args = [h2, x1, gt, wu, wc, wd]
    if per_row:
        in_specs.append(row(up))
        args.append(bufp)
        nb_shape, nb_spec, scratch = lead + (up,), row(up), []
    else:
        nb_shape = (lead[0], SUBLANES, up)
        nb_spec = pl.BlockSpec((None, SUBLANES, up), lambda i, j: (i, 0, 0))
        scratch = [pltpu.VMEM((SUBLANES, up), F32)]
    return pl.pallas_call(
        functools.partial(_ffn_kernel, grouped=per_row, cw=cw),
        grid=grid, in_specs=in_specs,
        out_specs=[row(d), nb_spec],
        out_shape=[jax.ShapeDtypeStruct(lead + (d,), F32), jax.ShapeDtypeStruct(nb_shape, F32)],
        scratch_shapes=scratch,
        compiler_params=_cparams(sem),
        name="convffn_rows" if per_row else "convffn_seq",
    )(*args)


def _prep_weights(g_attn_norm, w_in, g_q, g_k, sb_bias, g_sb_out, w_dn_conv, a_log, dt_bias, g_dn_out,
                  w_out, g_ffn_norm, w_up, w_ffn_conv, w_down):
    d = w_in.shape[0]
    o1 = 3 * SB_WIDTH
    o2 = o1 + 3 * DN_WIDTH
    o3 = o2 + 2 * DN_HEADS
    pad = jnp.zeros((d, LANES - 2 * DN_HEADS), w_in.dtype)
    w1 = jnp.concatenate([w_in[:, :o2], w_in[:, o3:], w_in[:, o2:o3], pad], axis=1).astype(BF16)
    lane_pad = lambda v: jnp.zeros((1, LANES), F32).at[0, DN_HEADS:2 * DN_HEADS].set(v)
    return dict(
        gn1=g_attn_norm.reshape(1, d), w1=w1,
        gq=jnp.tile(g_q, SB_HEADS).reshape(1, SB_WIDTH), gk=jnp.tile(g_k, SB_HEADS).reshape(1, SB_WIDTH),
        bias=sb_bias.astype(F32), g_sb=jnp.tile(g_sb_out, SB_HEADS).reshape(1, SB_WIDTH),
        wc_dn=w_dn_conv, al=lane_pad(a_log), dt=lane_pad(dt_bias), gdn=g_dn_out.reshape(1, DN_DIM),
        w_out=w_out.astype(BF16), gn2=g_ffn_norm.reshape(1, d), w_up=w_up.astype(BF16),
        wc_ffn=w_ffn_conv, w_down=w_down.astype(BF16))


def _layer_prompt(x, mod, p):
    b, t, d = x.shape
    sh1, sc1, gt1, sh2, sc2, gt2 = [m.reshape(b, 1, d) for m in jnp.split(mod, 6, axis=-1)]
    qb, kb, vb, kf, vf, qd, kd, vd, zg, gates, dnbuf = _inproj(
        x, sc1, sh1, p["gn1"], p["w1"], p["gq"], p["gk"], p["wc_dn"], p["al"], p["dt"], None, tm=256)
    osb = _sb_prompt(qb, kb, vb, p["bias"], p["g_sb"][:, :LANES], tb=256)
    odn, s_new = _dn_seq(qd, kd, vd, gates)
    x1, h2 = _outproj(osb, odn, zg, p["gdn"], x, gt1, sc2, sh2, p["gn2"], p["w_out"], tm=256, per_row=False)
    y, ffbuf = _ffn(h2, x1, gt2, p["w_up"], p["wc_ffn"], p["w_down"], None, tm=256, per_row=False)
    k_dn = p["wc_dn"].shape[0] - 1
    k_ff = p["wc_ffn"].shape[0] - 1
    return (y, kf.reshape(b, t, SB_HEADS, SB_DIM), vf.reshape(b, t, SB_HEADS, SB_DIM), s_new,
            dnbuf[:, SUBLANES - k_dn:], ffbuf[:, SUBLANES - k_ff:])


def _layer_sample(x, mod, page_table, cache_k, cache_v, s0, dn_buf, ffn_buf, p):
    bs, n_q, d = x.shape
    assert n_q == SUBLANES
    n = bs * n_q
    rep = lambda m: jnp.broadcast_to(m[:, None, :], (bs, n_q, d)).reshape(n, d)
    sh1, sc1, gt1, sh2, sc2, gt2 = [rep(m) for m in jnp.split(mod, 6, axis=-1)]
    padrows = lambda buf: jnp.pad(buf, ((0, 0), (0, n_q - buf.shape[1]), (0, 0))).reshape(n, buf.shape[2])
    xf = x.reshape(n, d)
    qb, kb, vb, kf, vf, qd, kd, vd, zg, gates, dnbuf = _inproj(
        xf, sc1, sh1, p["gn1"], p["w1"], p["gq"], p["gk"], p["wc_dn"], p["al"], p["dt"], padrows(dn_buf), tm=128)
    del kb, vb
    w = SB_WIDTH
    bias_rows = jnp.repeat(p["bias"], n_q).reshape(SB_HEADS * n_q, 1)
    n_phys, page = cache_k.shape[0], cache_k.shape[1]
    pool_t = lambda c: jnp.transpose(c, (0, 2, 3, 1)).reshape(n_phys, w, page)
    osb = _sb_sample(page_table, bias_rows, qb.reshape(bs, n_q, w), kf.reshape(bs, n_q, w),
                     vf.reshape(bs, n_q, w), p["g_sb"], pool_t(cache_k), pool_t(cache_v))
    odn, s_new = _dn_grouped(qd, kd, vd, gates, s0, group=n_q)
    x1, h2 = _outproj(osb.reshape(n, w), odn, zg, p["gdn"], xf, gt1, sc2, sh2, p["gn2"], p["w_out"],
                      tm=128, per_row=True)
    y, ffbuf = _ffn(h2, x1, gt2, p["w_up"], p["wc_ffn"], p["w_down"], padrows(ffn_buf), tm=128, per_row=True)
    k_dn = p["wc_dn"].shape[0] - 1
    k_ff = p["wc_ffn"].shape[0] - 1
    return (y.reshape(bs, n_q, d), kf.reshape(bs, n_q, SB_HEADS, SB_DIM), vf.reshape(bs, n_q, SB_HEADS, SB_DIM),
            s_new, dnbuf.reshape(bs, n_q, -1)[:, :k_dn], ffbuf.reshape(bs, n_q, -1)[:, :k_ff])


def kernel(x_prompt, x_sample, c_prompt, c_sample, cache_k, cache_v, page_table, state_delta, state_dn_conv, state_ffn_conv, w_ada, b_ada, g_attn_norm, w_in, g_q, g_k, sb_bias, g_sb_out, w_dn_conv, a_log, dt_bias, g_dn_out, w_out, g_ffn_norm, w_up, w_ffn_conv, w_down):
    depth = w_ada.shape[0]
    bp = x_prompt.shape[0]
    yp, ys = x_prompt, x_sample
    outs = [[] for _ in range(10)]
    c_all = jnp.concatenate([c_prompt, c_sample], axis=0)
    pad_rows = (-c_all.shape[0]) % SUBLANES
    c_all = jnp.pad(c_all, ((0, pad_rows), (0, 0)))
    for l in range(depth):
        p = _prep_weights(g_attn_norm[l], w_in[l], g_q[l], g_k[l], sb_bias[l], g_sb_out[l], w_dn_conv[l],
                          a_log[l], dt_bias[l], g_dn_out[l], w_out[l], g_ffn_norm[l], w_up[l],
                          w_ffn_conv[l], w_down[l])
        mod = _ada(c_all, w_ada[l], b_ada[l])
        yp, kp, vp, sp, dcp, fcp = _layer_prompt(yp, mod[:bp], p)
        ys, ks, vs, ss, dcs, fcs = _layer_sample(ys, mod[bp:bp + x_sample.shape[0]], page_table,
                                                 cache_k[l], cache_v[l], state_delta[l],
                                                 state_dn_conv[l], state_ffn_conv[l], p)
        for lst, val in zip(outs, (kp, vp, ks, vs, sp, ss, dcp, dcs, fcp, fcs)):
            lst.append(val)
    return (yp, ys) + tuple(jnp.stack(o) for o in outs)
```

```python
import functools

import jax
import jax.numpy as jnp
from jax import lax
from jax.experimental import pallas as pl
from jax.experimental.pallas import tpu as pltpu

F32 = jnp.float32
BF16 = jnp.bfloat16

NORM_EPS = 1e-6
LOG2E = 1.4426950408889634
LANES = 128
SUBLANES = 8
VMEM_LIMIT = 56 * 1024 * 1024

SB_HEADS = 8
SB_DIM = 64
SB_WIDTH = SB_HEADS * SB_DIM
DN_HEADS = 4
DN_DIM = 128
DN_WIDTH = DN_HEADS * DN_DIM
DN_BLOCK = 128
PAGES_PER_STEP = 8


def _dot(a, b):
    return jnp.dot(a, b, preferred_element_type=F32)


def _dot_nt(a, b):
    return lax.dot_general(a, b, (((1,), (1,)), ((), ())), preferred_element_type=F32)


def _dot_tn(a, b):
    return lax.dot_general(a, b, (((0,), (0,)), ((), ())), preferred_element_type=F32)


def _silu(x):
    return x * jax.nn.sigmoid(x)


def _exp_neg_abs(x):
    return jnp.exp2(jnp.abs(x) * (-LOG2E))


def _softplus_parts(z):
    l = jnp.log(1.0 + jnp.exp(-jnp.abs(z)))
    return jnp.maximum(z, 0.0) + l, jnp.minimum(z, 0.0) - l


def _rms_rows(x, g):
    ms = jnp.mean(x * x, axis=-1, keepdims=True)
    return x * lax.rsqrt(ms + NORM_EPS) * g


def _half_lane_rms(blk, lo):
    sq = blk * blk
    s_lo = jnp.sum(jnp.where(lo, sq, 0.0), axis=-1, keepdims=True)
    s_hi = jnp.sum(jnp.where(lo, 0.0, sq), axis=-1, keepdims=True)
    ms = jnp.where(lo, s_lo, s_hi) * (1.0 / SB_DIM)
    return blk * lax.rsqrt(ms + NORM_EPS)


def _cparams(sem):
    return pltpu.CompilerParams(dimension_semantics=sem, vmem_limit_bytes=VMEM_LIMIT)


def _resident(shape):
    nd = len(shape)
    return pl.BlockSpec(shape, lambda *_: (0,) * nd, pipeline_mode=pl.Buffered(1))


def _ada_kernel(c_ref, w_ref, b_ref, o_ref):
    a = _silu(c_ref[...]).astype(BF16)
    o_ref[...] = _dot(a, w_ref[...].astype(BF16)) + b_ref[...]


def _ada(c, w, b):
    m, d = c.shape
    n = w.shape[1]
    tn = 512
    return pl.pallas_call(
        _ada_kernel,
        grid=(n // tn,),
        in_specs=[pl.BlockSpec((m, d), lambda j: (0, 0)),
                  pl.BlockSpec((d, tn), lambda j: (0, j)),
                  pl.BlockSpec((1, tn), lambda j: (0, j))],
        out_specs=pl.BlockSpec((m, tn), lambda j: (0, j)),
        out_shape=jax.ShapeDtypeStruct((m, n), F32),
        compiler_params=_cparams(("parallel",)),
        name="adaln_mod",
    )(c, w, b.reshape(1, n))


C_SB = 0
C_DN = 3 * SB_WIDTH
C_Z = C_DN + 3 * DN_WIDTH
C_G = C_Z + DN_WIDTH
C_END = C_G + LANES


def _causal_conv(cur, taps, prev_rows, grouped, bufp):
    w = taps.shape[0]
    tm = cur.shape[0]
    y = taps[w - 1:w, :] * cur
    if grouped:
        rowm = lax.broadcasted_iota(jnp.int32, (tm, 1), 0) & (SUBLANES - 1)
        for s in range(1, w):
            r = pltpu.roll(cur, s, 0)
            back = (w - 1) - s
            bs = bufp if back == 0 else pltpu.roll(bufp, tm - back, 0)
            y = y + taps[w - 1 - s:w - s, :] * jnp.where(rowm >= s, r, bs)
        return y
    row8 = lax.broadcasted_iota(jnp.int32, (SUBLANES, 1), 0)
    yh = taps[w - 1:w, :] * cur[0:SUBLANES]
    for s in range(1, w):
        r = pltpu.roll(cur, s, 0)
        y = y + taps[w - 1 - s:w - s, :] * r
        head = jnp.where(row8 < s, pltpu.roll(prev_rows, s, 0), r[0:SUBLANES])
        yh = yh + taps[w - 1 - s:w - s, :] * head
    return jnp.concatenate([yh, y[SUBLANES:]], axis=0)


def _inproj_kernel(*refs, grouped):
    if grouped:
        (x_ref, sc_ref, sh_ref, gn_ref, w_ref, gq_ref, gk_ref, wc_ref, al_ref, dt_ref, buf_ref,
         qb_ref, kb_ref, vb_ref, kf_ref, vf_ref, qd_ref, kd_ref, vd_ref, zg_ref, gate_ref, nbuf_ref) = refs
        carry_ref = None
    else:
        (x_ref, sc_ref, sh_ref, gn_ref, w_ref, gq_ref, gk_ref, wc_ref, al_ref, dt_ref,
         qb_ref, kb_ref, vb_ref, kf_ref, vf_ref, qd_ref, kd_ref, vd_ref, zg_ref, gate_ref, nbuf_ref,
         carry_ref) = refs
        buf_ref = None

        @pl.when(pl.program_id(1) == 0)
        def _():
            carry_ref[...] = jnp.zeros_like(carry_ref)

    x = x_ref[...]
    h = _rms_rows(x, gn_ref[...]) * (1.0 + sc_ref[...]) + sh_ref[...]
    hb = h.astype(BF16)
    tm = x.shape[0]
    lane = lax.broadcasted_iota(jnp.int32, (1, LANES), 1)
    lo = lane < SB_DIM

    for c in range(SB_WIDTH // LANES):
        sl = slice(c * LANES, (c + 1) * LANES)
        pq = _dot(hb, w_ref[:, C_SB + c * LANES:C_SB + (c + 1) * LANES])
        qn = _half_lane_rms(pq, lo) * gq_ref[:, sl]
        qb_ref[:, sl] = (qn * (SB_DIM ** -0.5)).astype(BF16)
        pk = _dot(hb, w_ref[:, C_SB + SB_WIDTH + c * LANES:C_SB + SB_WIDTH + (c + 1) * LANES])
        kn = _half_lane_rms(pk, lo) * gk_ref[:, sl]
        kf_ref[:, sl] = kn
        kb_ref[:, sl] = kn.astype(BF16)
    pv = _dot(hb, w_ref[:, C_SB + 2 * SB_WIDTH:C_SB + 3 * SB_WIDTH])
    vf_ref[...] = pv
    vb_ref[...] = pv.astype(BF16)

    cur = _dot(hb, w_ref[:, C_DN:C_Z])
    if grouped:
        y = _causal_conv(cur, wc_ref[...], None, True, buf_ref[...])
        nbuf_ref[...] = pltpu.roll(cur, tm - (SUBLANES - (wc_ref.shape[0] - 1)), 0)
    else:
        y = _causal_conv(cur, wc_ref[...], carry_ref[...], False, None)
        carry_ref[...] = cur[tm - SUBLANES:tm]
        nbuf_ref[...] = cur[tm - SUBLANES:tm]
    a = _silu(y)
    for hh in range(DN_HEADS):
        sl = slice(hh * DN_DIM, (hh + 1) * DN_DIM)
        qh = a[:, hh * DN_DIM:(hh + 1) * DN_DIM]
        kh = a[:, DN_WIDTH + hh * DN_DIM:DN_WIDTH + (hh + 1) * DN_DIM]
        qd_ref[:, sl] = qh * (lax.rsqrt(jnp.sum(qh * qh, axis=-1, keepdims=True) + NORM_EPS) * (DN_DIM ** -0.5))
        kd_ref[:, sl] = kh * lax.rsqrt(jnp.sum(kh * kh, axis=-1, keepdims=True) + NORM_EPS)
    vd_ref[...] = a[:, 2 * DN_WIDTH:3 * DN_WIDTH]

    zg_ref[...] = _silu(_dot(hb, w_ref[:, C_Z:C_G]))

    gb = _dot(hb, w_ref[:, C_G:C_END])
    sp, _ = _softplus_parts(gb + dt_ref[...])
    gate_ref[...] = jnp.where(lane < DN_HEADS, jax.nn.sigmoid(gb), -jnp.exp(al_ref[...]) * sp)


def _inproj(x, sc, sh, gn, w1, gq, gk, wc, al, dt, bufp, *, tm):
    grouped = bufp is not None
    d = x.shape[-1]
    if grouped:
        n = x.shape[0]
        grid = (n // tm,)
        row = lambda c: pl.BlockSpec((tm, c), lambda i: (i, 0))
        in_specs = [row(d), row(d), row(d)]
        lead = (n,)
        sem = ("arbitrary",)
        nb_shape, nb_spec = (n, 3 * DN_WIDTH), row(3 * DN_WIDTH)
        scratch = []
    else:
        b, t, _ = x.shape
        grid = (b, t // tm)
        row = lambda c: pl.BlockSpec((None, tm, c), lambda i, j: (i, j, 0))
        per_b = pl.BlockSpec((None, 1, d), lambda i, j: (i, 0, 0))
        in_specs = [row(d), per_b, per_b]
        lead = (b, t)
        sem = ("parallel", "arbitrary")
        nb_shape = (b, SUBLANES, 3 * DN_WIDTH)
        nb_spec = pl.BlockSpec((None, SUBLANES, 3 * DN_WIDTH), lambda i, j: (i, 0, 0))
        scratch = [pltpu.VMEM((SUBLANES, 3 * DN_WIDTH), F32)]
    in_specs += [_resident(gn.shape), _resident(w1.shape), _resident(gq.shape), _resident(gk.shape),
                 _resident(wc.shape), _resident(al.shape), _resident(dt.shape)]
    args = [x, sc, sh, gn, w1, gq, gk, wc, al, dt]
    if grouped:
        in_specs.append(row(3 * DN_WIDTH))
        args.append(bufp)
    out_cols = [(SB_WIDTH, BF16)] * 3 + [(SB_WIDTH, F32)] * 2 + [(DN_WIDTH, F32)] * 4 + [(LANES, F32)]
    out_shape = [jax.ShapeDtypeStruct(lead + (c,), dt_) for c, dt_ in out_cols]
    out_specs = [row(c) for c, _ in out_cols]
    out_shape.append(jax.ShapeDtypeStruct(nb_shape, F32))
    out_specs.append(nb_spec)
    return pl.pallas_call(
        functools.partial(_inproj_kernel, grouped=grouped),
        grid=grid, in_specs=in_specs, out_specs=out_specs, out_shape=out_shape,
        scratch_shapes=scratch,
        compiler_params=_cparams(sem),
        name="inproj_grouped" if grouped else "inproj_seq",
    )(*args)


def _sb_prompt_kernel(bias_ref, q_ref, k_ref, v_ref, g_ref, o_ref, acc_ref, run_ref, *, tb):
    p = pl.program_id(1)
    i = pl.program_id(2)
    lane = lax.broadcasted_iota(jnp.int32, (1, LANES), 1)
    lo = lane < SB_DIM
    q = q_ref[...]
    zero = jnp.zeros_like(q)
    q2 = jnp.concatenate([jnp.where(lo, q, zero), jnp.where(lo, zero, q)], axis=0)
    head_row = lax.broadcasted_iota(jnp.int32, (2 * tb, 1), 0) < tb
    bias2 = jnp.where(head_row, bias_ref[2 * p], bias_ref[2 * p + 1])
    row = lax.broadcasted_iota(jnp.int32, (tb, tb), 0)
    col = lax.broadcasted_iota(jnp.int32, (tb, tb), 1)
    later_keys = jnp.where(row > col, 1.0, 0.0).astype(BF16)
    qpos = lax.broadcasted_iota(jnp.int32, (2 * tb, tb), 0) & (tb - 1)
    valid = lax.broadcasted_iota(jnp.int32, (2 * tb, tb), 1) < qpos

    acc_ref[...] = jnp.zeros_like(acc_ref)
    run_ref[...] = jnp.zeros_like(run_ref)

    def block(j, masked):
        start = pl.multiple_of(j * tb, tb)
        ks = k_ref[pl.ds(start, tb), :]
        vs = v_ref[pl.ds(start, tb), :]
        z = _dot_nt(q2, ks) + bias2
        sp = jnp.maximum(z, 0.0) + jnp.log(1.0 + _exp_neg_abs(z))
        lb = z - sp
        if masked:
            sp = jnp.where(valid, sp, 0.0)
        later = _dot(sp.astype(BF16), later_keys)
        run = run_ref[...]
        att = jnp.concatenate(
            [jnp.exp(lb[:, c * LANES:(c + 1) * LANES] - later[:, c * LANES:(c + 1) * LANES] - run)
             for c in range(tb // LANES)], axis=1)
        if masked:
            att = jnp.where(valid, att, 0.0)
        acc_ref[...] += _dot(att.astype(BF16), vs)
        run_ref[...] = run + (later[:, 0:1] + sp[:, 0:1])

    def pair(j):
        start = pl.multiple_of((j - 1) * tb, tb)
        ks = k_ref[pl.ds(start, 2 * tb), :]
        vs = v_ref[pl.ds(start, 2 * tb), :]
        z = _dot_nt(q2, ks) + bias2
        sp = jnp.maximum(z, 0.0) + jnp.log(1.0 + _exp_neg_abs(z))
        lb = z - sp
        sp16 = sp.astype(BF16)
        later = _dot(jnp.concatenate([sp16[:, tb:], sp16[:, :tb]], axis=0), later_keys)
        later_new, later_old = later[:2 * tb], later[2 * tb:]
        run_new = run_ref[...]
        run_old = run_new + (later_new[:, 0:1] + sp[:, tb:tb + 1])
        nc = tb // LANES
        att = jnp.concatenate(
            [jnp.exp(lb[:, c * LANES:(c + 1) * LANES] - later_old[:, c * LANES:(c + 1) * LANES] - run_old)
             for c in range(nc)]
            + [jnp.exp(lb[:, tb + c * LANES:tb + (c + 1) * LANES] - later_new[:, c * LANES:(c + 1) * LANES] - run_new)
               for c in range(nc)], axis=1)
        acc_ref[...] += _dot(att.astype(BF16), vs)
        run_ref[...] = run_old + (later_old[:, 0:1] + sp[:, 0:1])

    block(i, True)
    odd = i & 1

    @pl.when(odd == 1)
    def _():
        block(i - 1, False)

    def body(jj, carry):
        pair(i - 1 - odd - 2 * jj)
        return carry

    lax.fori_loop(0, i >> 1, body, 0)

    o = jnp.where(lo, acc_ref[0:tb], acc_ref[tb:2 * tb])
    o_ref[...] = (_half_lane_rms(o, lo) * g_ref[...]).astype(o_ref.dtype)


def _sb_prompt(qb, kb, vb, bias, g2, *, tb):
    b, t, w = qb.shape
    pairs = w // LANES
    return pl.pallas_call(
        functools.partial(_sb_prompt_kernel, tb=tb),
        grid_spec=pltpu.PrefetchScalarGridSpec(
            num_scalar_prefetch=0,
            grid=(b, pairs, t // tb),
            in_specs=[pl.BlockSpec(memory_space=pltpu.SMEM),
                      pl.BlockSpec((None, tb, LANES), lambda i, p, j: (i, j, p)),
                      pl.BlockSpec((None, t, LANES), lambda i, p, j: (i, 0, p)),
                      pl.BlockSpec((None, t, LANES), lambda i, p, j: (i, 0, p)),
                      pl.BlockSpec((1, LANES), lambda i, p, j: (0, 0))],
            out_specs=pl.BlockSpec((None, tb, LANES), lambda i, p, j: (i, j, p)),
            scratch_shapes=[pltpu.VMEM((2 * tb, LANES), F32), pltpu.VMEM((2 * tb, LANES), F32)]),
        out_shape=jax.ShapeDtypeStruct((b, t, w), BF16),
        compiler_params=_cparams(("parallel", "parallel", "arbitrary")),
        name="sb_attn_prompt",
    )(bias, qb, kb, vb, g2)


def _sb_sample_kernel(*refs, pps, n_q):
    pt_ref, bias_ref, q_ref, kn_ref, vn_ref, g_ref = refs[:6]
    kp_refs = refs[6:6 + pps]
    vp_refs = refs[6 + pps:6 + 2 * pps]
    o_ref = refs[6 + 2 * pps]
    qx_ref, acc_ref, run_ref, newk_ref, newv_ref = refs[7 + 2 * pps:]
    del pt_ref
    j = pl.program_id(1)
    rows = SB_HEADS * n_q
    page = kp_refs[0].shape[1]
    width = SB_WIDTH
    r1 = lax.broadcasted_iota(jnp.int32, (rows, 1), 0)
    later_keys = jnp.where(lax.broadcasted_iota(jnp.int32, (page, page), 0)
                           > lax.broadcasted_iota(jnp.int32, (page, page), 1), 1.0, 0.0).astype(BF16)

    def new_token_block(kf, vf):
        z = _dot_nt(qx_ref[...], kf.astype(BF16)) + bias_ref[...]
        sp, lb = _softplus_parts(z)
        valid = lax.broadcasted_iota(jnp.int32, (rows, page), 1) < (r1 & (n_q - 1))
        sp = jnp.where(valid, sp, 0.0)
        arg = lb - _dot(sp.astype(BF16), later_keys) - run_ref[...]
        att = jnp.where(valid, jnp.exp(arg), 0.0)
        acc_ref[...] += _dot(att.astype(BF16), vf.astype(BF16))
        run_ref[...] += jnp.sum(sp, axis=-1, keepdims=True)

    def cached_pages():
        k_all = jnp.concatenate([kp_refs[r][...].astype(BF16) for r in range(pps)], axis=1)
        z = _dot(qx_ref[...], k_all) + bias_ref[...]
        sp, lb = _softplus_parts(z)
        sp16 = sp.astype(BF16)
        later = _dot(jnp.concatenate([sp16[:, r * page:(r + 1) * page] for r in range(pps)], axis=0),
                     later_keys)
        run = run_ref[...]
        args = []
        for r in range(pps):
            cs = slice(r * page, (r + 1) * page)
            args.append(lb[:, cs] - later[r * rows:(r + 1) * rows] - run)
            run = run + jnp.sum(sp[:, cs], axis=-1, keepdims=True)
        run_ref[...] = run
        att = jnp.exp(jnp.concatenate(args, axis=1)).astype(BF16)
        v_all = jnp.concatenate([vp_refs[r][...].astype(BF16) for r in range(pps)], axis=1)
        acc_ref[...] += _dot_nt(att, v_all)

    @pl.when(j == 0)
    def _():
        head_of_lane = lax.broadcasted_iota(jnp.int32, (rows, width), 1) // SB_DIM
        head_of_row = lax.broadcasted_iota(jnp.int32, (rows, width), 0) // n_q
        qrep = jnp.concatenate([q_ref[...].astype(F32)] * SB_HEADS, axis=0)
        qx_ref[...] = jnp.where(head_of_lane == head_of_row, qrep, 0.0).astype(BF16)
        acc_ref[...] = jnp.zeros_like(acc_ref)
        run_ref[...] = jnp.zeros_like(run_ref)
        newk_ref[...] = jnp.zeros_like(newk_ref)
        newv_ref[...] = jnp.zeros_like(newv_ref)
        newk_ref[0:n_q, :] = kn_ref[...]
        newv_ref[0:n_q, :] = vn_ref[...]
        new_token_block(newk_ref[...], newv_ref[...])

    cached_pages()

    @pl.when(j == pl.num_programs(1) - 1)
    def _():
        res = acc_ref[...]
        lane_head = lax.broadcasted_iota(jnp.int32, (n_q, width), 1) // SB_DIM
        o = jnp.zeros((n_q, width), F32)
        for hh in range(SB_HEADS):
            o = o + jnp.where(lane_head == hh, res[hh * n_q:(hh + 1) * n_q, :], 0.0)
        lo = lax.broadcasted_iota(jnp.int32, (1, LANES), 1) < SB_DIM
        for c in range(width // LANES):
            sl = slice(c * LANES, (c + 1) * LANES)
            o_ref[:, sl] = _half_lane_rms(o[:, sl], lo) * g_ref[:, sl]


def _sb_sample(page_table, bias_rows, q3, kn3, vn3, g_sb, cache_k, cache_v):
    bs, n_q, w = q3.shape
    n_pages = page_table.shape[1]
    page = cache_k.shape[2]
    pps = PAGES_PER_STEP
    rows = SB_HEADS * n_q

    def page_spec(r):
        return pl.BlockSpec((None, w, page),
                            lambda i, j, pt: (pt[i, n_pages - 1 - (j * pps + r)], 0, 0))

    tok = lambda: pl.BlockSpec((None, n_q, w), lambda i, j, pt: (i, 0, 0))
    const = lambda shp: pl.BlockSpec(shp, lambda i, j, pt: (0,) * len(shp))
    return pl.pallas_call(
        functools.partial(_sb_sample_kernel, pps=pps, n_q=n_q),
        grid_spec=pltpu.PrefetchScalarGridSpec(
            num_scalar_prefetch=1,
            grid=(bs, n_pages // pps),
            in_specs=[const((rows, 1)), tok(), tok(), tok(), const((1, w))]
                     + [page_spec(r) for r in range(pps)] + [page_spec(r) for r in range(pps)],
            out_specs=tok(),
            scratch_shapes=[pltpu.VMEM((rows, w), BF16), pltpu.VMEM((rows, w), F32),
                            pltpu.VMEM((rows, 1), F32), pltpu.VMEM((page, w), F32),
                            pltpu.VMEM((page, w), F32)]),
        out_shape=jax.ShapeDtypeStruct((bs, n_q, w), F32),
        compiler_params=_cparams(("parallel", "arbitrary")),
        name="sb_attn_paged",
    )(page_table, bias_rows, q3, kn3, vn3, g_sb, *([cache_k] * pps), *([cache_v] * pps))


def _split3(x):
    hi = x.astype(BF16)
    r = x - hi.astype(F32)
    mid = r.astype(BF16)
    lo = (r - mid.astype(F32)).astype(BF16)
    return hi, mid, lo


def _dn_local(q_ref, k_ref, v_ref, gates, group):
    n = gates.shape[0]
    heads = range(DN_HEADS)
    hs = [slice(h * DN_DIM, (h + 1) * DN_DIM) for h in heads]
    row = lax.broadcasted_iota(jnp.int32, (n, n), 0)
    col = lax.broadcasted_iota(jnp.int32, (n, n), 1)
    shift = group.bit_length() - 1
    same = jnp.where((row >> shift) == (col >> shift), 1.0, 0.0)
    low = jnp.where(row >= col, same, 0.0)
    strict = jnp.where(row > col, same, 0.0)
    gparts = _split3(gates)
    low16 = low.astype(BF16)
    same16 = same.astype(BF16)
    g_all = sum(_dot(low16, m) for m in gparts)
    t_all = sum(_dot(same16, m) for m in gparts)
    g_row = [jnp.broadcast_to(g_all[:, DN_HEADS + h:DN_HEADS + h + 1], (n, n)) for h in heads]
    gtot = [jnp.broadcast_to(t_all[:, DN_HEADS + h:DN_HEADS + h + 1], (n, n)) for h in heads]
    decay = [low * jnp.exp(jnp.minimum(g_row[h] - g_row[h].T, 0.0)) for h in heads]
    beta = [gates[:, h:h + 1] for h in heads]
    k = [k_ref[:, hs[h]] for h in heads]
    k16 = [k[h].astype(BF16) for h in heads]
    kb = [k[h] * beta[h] for h in heads]
    a_mat = [strict * (_dot_nt(kb[h].astype(BF16), k16[h]) * decay[h]) for h in heads]
    eye = jnp.where(row == col, 1.0, 0.0)
    inv = [eye - a_mat[h] * jnp.where((row >> 1) == (col >> 1), 1.0, 0.0) for h in heads]
    for lb in range(1, shift):
        sel = jnp.where((row >> (lb + 1)) == (col >> (lb + 1)),
                        jnp.where(((row >> lb) & 1) > ((col >> lb) & 1), 1.0, 0.0), 0.0)
        inv16 = [inv[h].astype(BF16) for h in heads]
        x = [_dot((a_mat[h] * sel).astype(BF16), inv16[h]) for h in heads]
        inv = [inv[h] - _dot(inv16[h], x[h].astype(BF16)) for h in heads]
    eg = [jnp.exp(g_row[h]) for h in heads]
    rhs = [jnp.concatenate([v_ref[:, hs[h]] * beta[h], kb[h] * eg[h]], axis=-1).astype(BF16) for h in heads]
    sol = [_dot(inv[h].astype(BF16), rhs[h]) for h in heads]
    q = [q_ref[:, hs[h]] for h in heads]
    intra = [_dot_nt(q[h].astype(BF16), k16[h]) * decay[h] for h in heads]
    u = [sol[h][:, :DN_DIM] for h in heads]
    kcum = [sol[h][:, DN_DIM:] for h in heads]
    qdec = [q[h] * eg[h] for h in heads]
    kdec = [k[h] * jnp.exp(gtot[h] - g_row[h]) for h in heads]
    return u, kcum, intra, qdec, kdec, gtot


def _dn_seq_kernel(q_ref, k_ref, v_ref, gate_ref, o_ref, sout_ref, s_ref, *, group):
    @pl.when(pl.program_id(1) == 0)
    def _():
        s_ref[...] = jnp.zeros_like(s_ref)

    n = q_ref.shape[0]
    heads = range(DN_HEADS)
    u, kcum, intra, qdec, kdec, gtot = _dn_local(q_ref, k_ref, v_ref, gate_ref[...], group)
    s = [s_ref[h] for h in heads]
    vnews = [[] for _ in heads]
    outs = [[] for _ in heads]
    for c in range(n // group):
        rs = slice(c * group, (c + 1) * group)
        s16 = [s[h].astype(BF16) for h in heads]
        lhs = [jnp.concatenate([kcum[h][rs], qdec[h][rs]], axis=0).astype(BF16) for h in heads]
        both = [_dot(lhs[h], s16[h]) for h in heads]
        vnew = [u[h][rs] - both[h][:group] for h in heads]
        upd = [_dot_tn(kdec[h][rs].astype(BF16), vnew[h].astype(BF16)) for h in heads]
        s = [s[h] * jnp.exp(gtot[h][c * group:c * group + 1, :]) + upd[h] for h in heads]
        for h in heads:
            vnews[h].append(vnew[h])
            outs[h].append(both[h][group:])
    for h in heads:
        s_ref[h] = s[h]
        vfull = jnp.concatenate(vnews[h], axis=0).astype(BF16)
        o_ref[:, h * DN_DIM:(h + 1) * DN_DIM] = (jnp.concatenate(outs[h], axis=0)
                                                 + _dot(intra[h].astype(BF16), vfull))

    @pl.when(pl.program_id(1) == pl.num_programs(1) - 1)
    def _():
        sout_ref[...] = s_ref[...]


def _dn_seq(qd, kd, vd, gates):
    b, t, w = qd.shape
    n = DN_BLOCK
    row = lambda c: pl.BlockSpec((None, n, c), lambda i, j: (i, j, 0))
    return pl.pallas_call(
        functools.partial(_dn_seq_kernel, group=min(64, n)),
        grid=(b, t // n),
        in_specs=[row(w), row(w), row(w), row(LANES)],
        out_specs=[row(w), pl.BlockSpec((None, DN_HEADS, DN_DIM, DN_DIM), lambda i, j: (i, 0, 0, 0))],
        out_shape=[jax.ShapeDtypeStruct((b, t, w), F32),
                   jax.ShapeDtypeStruct((b, DN_HEADS, DN_DIM, DN_DIM), F32)],
        scratch_shapes=[pltpu.VMEM((DN_HEADS, DN_DIM, DN_DIM), F32)],
        compiler_params=_cparams(("parallel", "arbitrary")),
        name="deltanet_seq",
    )(qd, kd, vd, gates)


def _dn_grouped_kernel(q_ref, k_ref, v_ref, gate_ref, s0_ref, o_ref, sout_ref,
                       u_ref, kc_ref, qd_ref, kd_ref, gt_ref, vn_ref, oq_ref, *, group):
    n = q_ref.shape[0]
    heads = range(DN_HEADS)
    row1 = lax.broadcasted_iota(jnp.int32, (n, 1), 0)
    shift = group.bit_length() - 1
    u, kcum, intra, qdec, kdec, gtot = _dn_local(q_ref, k_ref, v_ref, gate_ref[...], group)
    for h in heads:
        u_ref[h] = u[h]
        kc_ref[h] = kcum[h]
        qd_ref[h] = qdec[h]
        kd_ref[h] = kdec[h]
        gt_ref[h] = gtot[h]
    vn_ref[...] = jnp.zeros_like(vn_ref)

    def per_seq(bi, carry):
        rs = pl.ds(pl.multiple_of(bi * group, group), group)
        mine = (row1 >> shift) == bi
        for h in heads:
            s = s0_ref[bi, h]
            vnew = u_ref[h, rs, :] - _dot(kc_ref[h, rs, :], s)
            oq_ref[h, rs, :] = _dot(qd_ref[h, rs, :], s)
            vn_ref[h, rs, :] = vnew
            kmask = jnp.where(mine, kd_ref[h], 0.0).astype(BF16)
            sout_ref[bi, h] = (s * jnp.exp(gt_ref[h, rs, :][0:1, :])
                               + _dot_tn(kmask, vn_ref[h].astype(BF16)))
        return carry

    lax.fori_loop(0, n // group, per_seq, 0)
    for h in heads:
        o_ref[:, h * DN_DIM:(h + 1) * DN_DIM] = (oq_ref[h]
                                                 + _dot(intra[h].astype(BF16), vn_ref[h].astype(BF16)))


def _dn_grouped(qd, kd, vd, gates, s0, *, group):
    nrows, w = qd.shape
    n = DN_BLOCK
    per = n // group
    row = lambda c: pl.BlockSpec((n, c), lambda i: (i, 0))
    st = pl.BlockSpec((per, DN_HEADS, DN_DIM, DN_DIM), lambda i: (i, 0, 0, 0))
    return pl.pallas_call(
        functools.partial(_dn_grouped_kernel, group=group),
        grid=(nrows // n,),
        in_specs=[row(w), row(w), row(w), row(LANES), st],
        out_specs=[row(w), st],
        out_shape=[jax.ShapeDtypeStruct((nrows, w), F32), jax.ShapeDtypeStruct(s0.shape, F32)],
        scratch_shapes=[pltpu.VMEM((DN_HEADS, n, DN_DIM), F32)] * 7,
        compiler_params=_cparams(("parallel",)),
        name="deltanet_grouped",
    )(qd, kd, vd, gates, s0)


def _outproj_kernel(osb_ref, odn_ref, zg_ref, gdn_ref, x_ref, gt_ref, sc_ref, sh_ref, gn_ref, w_ref,
                    x1_ref, h2_ref):
    mix = _dot(osb_ref[...].astype(BF16), w_ref[0:SB_WIDTH, :])
    for hh in range(DN_HEADS):
        sl = slice(hh * DN_DIM, (hh + 1) * DN_DIM)
        on = _rms_rows(odn_ref[:, sl], gdn_ref[...]) * zg_ref[:, sl]
        mix = mix + _dot(on.astype(BF16), w_ref[SB_WIDTH + hh * DN_DIM:SB_WIDTH + (hh + 1) * DN_DIM, :])
    x1 = x_ref[...] + gt_ref[...] * mix
    x1_ref[...] = x1
    h2_ref[...] = (_rms_rows(x1, gn_ref[...]) * (1.0 + sc_ref[...]) + sh_ref[...]).astype(BF16)


def _mod_specs(x, tm, per_row):
    d = x.shape[-1]
    if per_row:
        n = x.shape[0]
        row = lambda c: pl.BlockSpec((tm, c), lambda i: (i, 0))
        return (n // tm,), row, row(d), (n,), ("arbitrary",)
    b, t, _ = x.shape
    row = lambda c: pl.BlockSpec((None, tm, c), lambda i, j: (i, j, 0))
    per_b = pl.BlockSpec((None, 1, d), lambda i, j: (i, 0, 0))
    return (b, t // tm), row, per_b, (b, t), ("parallel", "arbitrary")


def _outproj(osb, odn, zg, gdn, x, gt, sc, sh, gn, w, *, tm, per_row):
    d = x.shape[-1]
    grid, row, mod, lead, sem = _mod_specs(x, tm, per_row)
    return pl.pallas_call(
        _outproj_kernel,
        grid=grid,
        in_specs=[row(SB_WIDTH), row(DN_WIDTH), row(DN_WIDTH), _resident(gdn.shape), row(d), mod, mod, mod,
                  _resident(gn.shape), _resident(w.shape)],
        out_specs=[row(d), row(d)],
        out_shape=[jax.ShapeDtypeStruct(lead + (d,), F32), jax.ShapeDtypeStruct(lead + (d,), BF16)],
        compiler_params=_cparams(sem),
        name="outproj_rows" if per_row else "outproj_seq",
    )(osb, odn, zg, gdn, x, gt, sc, sh, gn, w)


def _ffn_kernel(*refs, grouped, cw):
    if grouped:
        h_ref, x1_ref, gt_ref, wu_ref, wc_ref, wd_ref, buf_ref, y_ref, nbuf_ref = refs
        carry_ref = None
    else:
        h_ref, x1_ref, gt_ref, wu_ref, wc_ref, wd_ref, y_ref, nbuf_ref, carry_ref = refs
        buf_ref = None

        @pl.when(pl.program_id(1) == 0)
        def _():
            carry_ref[...] = jnp.zeros_like(carry_ref)

    hb = h_ref[...]
    tm = hb.shape[0]
    dff = wd_ref.shape[0]
    taps = wc_ref.shape[0]
    acc = jnp.zeros(x1_ref.shape, F32)
    for c in range(dff // cw):
        halves = []
        for base in (0, dff):
            cs = slice(base + c * cw, base + (c + 1) * cw)
            cur = _dot(hb, wu_ref[:, cs])
            if grouped:
                y = _causal_conv(cur, wc_ref[:, cs], None, True, buf_ref[:, cs])
                nbuf_ref[:, cs] = pltpu.roll(cur, tm - (SUBLANES - (taps - 1)), 0)
            else:
                y = _causal_conv(cur, wc_ref[:, cs], carry_ref[:, cs], False, None)
                carry_ref[:, cs] = cur[tm - SUBLANES:tm]
                nbuf_ref[:, cs] = cur[tm - SUBLANES:tm]
            halves.append(y)
        act = (_silu(halves[1]) * halves[0]).astype(BF16)
        acc = acc + _dot(act, wd_ref[c * cw:(c + 1) * cw, :])
    y_ref[...] = x1_ref[...] + gt_ref[...] * acc


def _ffn(h2, x1, gt, wu, wc, wd, bufp, *, tm, per_row, cw=256):
    d = x1.shape[-1]
    up = wu.shape[1]
    grid, row, mod, lead, sem = _mod_specs(x1, tm, per_row)
    in_specs = [row(d), row(d), mod, _resident(wu.shape), _resident(wc.shape), _resident(wd.shape)]
    args = [h2, x1, gt, wu, wc, wd]
    if per_row:
        in_specs.append(row(up))
        args.append(bufp)
        nb_shape, nb_spec, scratch = lead + (up,), row(up), []
    else:
        nb_shape = (lead[0], SUBLANES, up)
        nb_spec = pl.BlockSpec((None, SUBLANES, up), lambda i, j: (i, 0, 0))
        scratch = [pltpu.VMEM((SUBLANES, up), F32)]
    return pl.pallas_call(
        functools.partial(_ffn_kernel, grouped=per_row, cw=cw),
        grid=grid, in_specs=in_specs,
        out_specs=[row(d), nb_spec],
        out_shape=[jax.ShapeDtypeStruct(lead + (d,), F32), jax.ShapeDtypeStruct(nb_shape, F32)],
        scratch_shapes=scratch,
        compiler_params=_cparams(sem),
        name="convffn_rows" if per_row else "convffn_seq",
    )(*args)


def _prep_weights(g_attn_norm, w_in, g_q, g_k, sb_bias, g_sb_out, w_dn_conv, a_log, dt_bias, g_dn_out,
                  w_out, g_ffn_norm, w_up, w_ffn_conv, w_down):
    d = w_in.shape[0]
    o1 = 3 * SB_WIDTH
    o2 = o1 + 3 * DN_WIDTH
    o3 = o2 + 2 * DN_HEADS
    pad = jnp.zeros((d, LANES - 2 * DN_HEADS), w_in.dtype)
    w1 = jnp.concatenate([w_in[:, :o2], w_in[:, o3:], w_in[:, o2:o3], pad], axis=1).astype(BF16)
    lane_pad = lambda v: jnp.zeros((1, LANES), F32).at[0, DN_HEADS:2 * DN_HEADS].set(v)
    return dict(
        gn1=g_attn_norm.reshape(1, d), w1=w1,
        gq=jnp.tile(g_q, SB_HEADS).reshape(1, SB_WIDTH), gk=jnp.tile(g_k, SB_HEADS).reshape(1, SB_WIDTH),
        bias=sb_bias.astype(F32), g_sb=jnp.tile(g_sb_out, SB_HEADS).reshape(1, SB_WIDTH),
        wc_dn=w_dn_conv, al=lane_pad(a_log), dt=lane_pad(dt_bias), gdn=g_dn_out.reshape(1, DN_DIM),
        w_out=w_out.astype(BF16), gn2=g_ffn_norm.reshape(1, d), w_up=w_up.astype(BF16),
        wc_ffn=w_ffn_conv, w_down=w_down.astype(BF16))


def _layer_prompt(x, mod, p):
    b, t, d = x.shape
    sh1, sc1, gt1, sh2, sc2, gt2 = [m.reshape(b, 1, d) for m in jnp.split(mod, 6, axis=-1)]
    qb, kb, vb, kf, vf, qd, kd, vd, zg, gates, dnbuf = _inproj(
        x, sc1, sh1, p["gn1"], p["w1"], p["gq"], p["gk"], p["wc_dn"], p["al"], p["dt"], None, tm=256)
    osb = _sb_prompt(qb, kb, vb, p["bias"], p["g_sb"][:, :LANES], tb=256)
    odn, s_new = _dn_seq(qd, kd, vd, gates)
    x1, h2 = _outproj(osb, odn, zg, p["gdn"], x, gt1, sc2, sh2, p["gn2"], p["w_out"], tm=256, per_row=False)
    y, ffbuf = _ffn(h2, x1, gt2, p["w_up"], p["wc_ffn"], p["w_down"], None, tm=256, per_row=False)
    k_dn = p["wc_dn"].shape[0] - 1
    k_ff = p["wc_ffn"].shape[0] - 1
    return (y, kf.reshape(b, t, SB_HEADS, SB_DIM), vf.reshape(b, t, SB_HEADS, SB_DIM), s_new,
            dnbuf[:, SUBLANES - k_dn:], ffbuf[:, SUBLANES - k_ff:])


def _layer_sample(x, mod, page_table, cache_k, cache_v, s0, dn_buf, ffn_buf, p):
    bs, n_q, d = x.shape
    assert n_q == SUBLANES
    n = bs * n_q
    rep = lambda m: jnp.broadcast_to(m[:, None, :], (bs, n_q, d)).reshape(n, d)
    sh1, sc1, gt1, sh2, sc2, gt2 = [rep(m) for m in jnp.split(mod, 6, axis=-1)]
    padrows = lambda buf: jnp.pad(buf, ((0, 0), (0, n_q - buf.shape[1]), (0, 0))).reshape(n, buf.shape[2])
    xf = x.reshape(n, d)
    qb, kb, vb, kf, vf, qd, kd, vd, zg, gates, dnbuf = _inproj(
        xf, sc1, sh1, p["gn1"], p["w1"], p["gq"], p["gk"], p["wc_dn"], p["al"], p["dt"], padrows(dn_buf), tm=128)
    del kb, vb
    w = SB_WIDTH
    bias_rows = jnp.repeat(p["bias"], n_q).reshape(SB_HEADS * n_q, 1)
    n_phys, page = cache_k.shape[0], cache_k.shape[1]
    pool_t = lambda c: jnp.transpose(c, (0, 2, 3, 1)).reshape(n_phys, w, page)
    osb = _sb_sample(page_table, bias_rows, qb.reshape(bs, n_q, w), kf.reshape(bs, n_q, w),
                     vf.reshape(bs, n_q, w), p["g_sb"], pool_t(cache_k), pool_t(cache_v))
    odn, s_new = _dn_grouped(qd, kd, vd, gates, s0, group=n_q)
    x1, h2 = _outproj(osb.reshape(n, w), odn, zg, p["gdn"], xf, gt1, sc2, sh2, p["gn2"], p["w_out"],
                      tm=128, per_row=True)
    y, ffbuf = _ffn(h2, x1, gt2, p["w_up"], p["wc_ffn"], p["w_down"], padrows(ffn_buf), tm=128, per_row=True)
    k_dn = p["wc_dn"].shape[0] - 1
    k_ff = p["wc_ffn"].shape[0] - 1
    return (y.reshape(bs, n_q, d), kf.reshape(bs, n_q, SB_HEADS, SB_DIM), vf.reshape(bs, n_q, SB_HEADS, SB_DIM),
            s_new, dnbuf.reshape(bs, n_q, -1)[:, :k_dn], ffbuf.reshape(bs, n_q, -1)[:, :k_ff])


def kernel(x_prompt, x_sample, c_prompt, c_sample, cache_k, cache_v, page_table, state_delta, state_dn_conv, state_ffn_conv, w_ada, b_ada, g_attn_norm, w_in, g_q, g_k, sb_bias, g_sb_out, w_dn_conv, a_log, dt_bias, g_dn_out, w_out, g_ffn_norm, w_up, w_ffn_conv, w_down):
    depth = w_ada.shape[0]
    bp = x_prompt.shape[0]
    yp, ys = x_prompt, x_sample
    outs = [[] for _ in range(10)]
    c_all = jnp.concatenate([c_prompt, c_sample], axis=0)
    pad_rows = (-c_all.shape[0]) % SUBLANES
    c_all = jnp.pad(c_all, ((0, pad_rows), (0, 0)))
    for l in range(depth):
        p = _prep_weights(g_attn_norm[l], w_in[l], g_q[l], g_k[l], sb_bias[l], g_sb_out[l], w_dn_conv[l],
                          a_log[l], dt_bias[l], g_dn_out[l], w_out[l], g_ffn_norm[l], w_up[l],
                          w_ffn_conv[l], w_down[l])
        mod = _ada(c_all, w_ada[l], b_ada[l])
        yp, kp, vp, sp, dcp, fcp = _layer_prompt(yp, mod[:bp], p)
        ys, ks, vs, ss, dcs, fcs = _layer_sample(ys, mod[bp:bp + x_sample.shape[0]], page_table,
                                                 cache_k[l], cache_v[l], state_delta[l],
                                                 state_dn_conv[l], state_ffn_conv[l], p)
        for lst, val in zip(outs, (kp, vp, ks, vs, sp, ss, dcp, dcs, fcp, fcs)):
            lst.append(val)
    return (yp, ys) + tuple(jnp.stack(o) for o in outs)
```

```python
import functools

import jax
import jax.numpy as jnp
from jax import lax
from jax.experimental import pallas as pl
from jax.experimental.pallas import tpu as pltpu

F32 = jnp.float32
BF16 = jnp.bfloat16

NORM_EPS = 1e-6
LOG2E = 1.4426950408889634
LANES = 128
SUBLANES = 8
VMEM_LIMIT = 56 * 1024 * 1024

SB_HEADS = 8
SB_DIM = 64
SB_WIDTH = SB_HEADS * SB_DIM
DN_HEADS = 4
DN_DIM = 128
DN_WIDTH = DN_HEADS * DN_DIM
DN_BLOCK = 128
PAGES_PER_STEP = 8


def _dot(a, b):
    return jnp.dot(a, b, preferred_element_type=F32)


def _dot_nt(a, b):
    return lax.dot_general(a, b, (((1,), (1,)), ((), ())), preferred_element_type=F32)


def _dot_tn(a, b):
    return lax.dot_general(a, b, (((0,), (0,)), ((), ())), preferred_element_type=F32)


def _silu(x):
    return x * jax.nn.sigmoid(x)


def _exp_neg_abs(x):
    return jnp.exp2(jnp.abs(x) * (-LOG2E))


def _softplus_parts(z):
    l = jnp.log(1.0 + jnp.exp(-jnp.abs(z)))
    return jnp.maximum(z, 0.0) + l, jnp.minimum(z, 0.0) - l


def _rms_rows(x, g):
    ms = jnp.mean(x * x, axis=-1, keepdims=True)
    return x * lax.rsqrt(ms + NORM_EPS) * g


def _half_lane_rms(blk, lo):
    sq = blk * blk
    s_lo = jnp.sum(jnp.where(lo, sq, 0.0), axis=-1, keepdims=True)
    s_hi = jnp.sum(jnp.where(lo, 0.0, sq), axis=-1, keepdims=True)
    ms = jnp.where(lo, s_lo, s_hi) * (1.0 / SB_DIM)
    return blk * lax.rsqrt(ms + NORM_EPS)


def _cparams(sem):
    return pltpu.CompilerParams(dimension_semantics=sem, vmem_limit_bytes=VMEM_LIMIT)


def _resident(shape):
    nd = len(shape)
    return pl.BlockSpec(shape, lambda *_: (0,) * nd, pipeline_mode=pl.Buffered(1))


def _ada_kernel(c_ref, w_ref, b_ref, o_ref):
    a = _silu(c_ref[...]).astype(BF16)
    o_ref[...] = _dot(a, w_ref[...].astype(BF16)) + b_ref[...]


def _ada(c, w, b):
    m, d = c.shape
    n = w.shape[1]
    tn = 512
    return pl.pallas_call(
        _ada_kernel,
        grid=(n // tn,),
        in_specs=[pl.BlockSpec((m, d), lambda j: (0, 0)),
                  pl.BlockSpec((d, tn), lambda j: (0, j)),
                  pl.BlockSpec((1, tn), lambda j: (0, j))],
        out_specs=pl.BlockSpec((m, tn), lambda j: (0, j)),
        out_shape=jax.ShapeDtypeStruct((m, n), F32),
        compiler_params=_cparams(("parallel",)),
        name="adaln_mod",
    )(c, w, b.reshape(1, n))


C_SB = 0
C_DN = 3 * SB_WIDTH
C_Z = C_DN + 3 * DN_WIDTH
C_G = C_Z + DN_WIDTH
C_END = C_G + LANES


def _causal_conv(cur, taps, prev_rows, grouped, bufp):
    w = taps.shape[0]
    tm = cur.shape[0]
    y = taps[w - 1:w, :] * cur
    if grouped:
        rowm = lax.broadcasted_iota(jnp.int32, (tm, 1), 0) & (SUBLANES - 1)
        for s in range(1, w):
            r = pltpu.roll(cur, s, 0)
            back = (w - 1) - s
            bs = bufp if back == 0 else pltpu.roll(bufp, tm - back, 0)
            y = y + taps[w - 1 - s:w - s, :] * jnp.where(rowm >= s, r, bs)
        return y
    row8 = lax.broadcasted_iota(jnp.int32, (SUBLANES, 1), 0)
    yh = taps[w - 1:w, :] * cur[0:SUBLANES]
    for s in range(1, w):
        r = pltpu.roll(cur, s, 0)
        y = y + taps[w - 1 - s:w - s, :] * r
        head = jnp.where(row8 < s, pltpu.roll(prev_rows, s, 0), r[0:SUBLANES])
        yh = yh + taps[w - 1 - s:w - s, :] * head
    return jnp.concatenate([yh, y[SUBLANES:]], axis=0)


def _inproj_kernel(*refs, grouped):
    if grouped:
        (x_ref, sc_ref, sh_ref, gn_ref, w_ref, gq_ref, gk_ref, wc_ref, al_ref, dt_ref, buf_ref,
         qb_ref, kb_ref, vb_ref, kf_ref, vf_ref, qd_ref, kd_ref, vd_ref, zg_ref, gate_ref, nbuf_ref) = refs
        carry_ref = None
    else:
        (x_ref, sc_ref, sh_ref, gn_ref, w_ref, gq_ref, gk_ref, wc_ref, al_ref, dt_ref,
         qb_ref, kb_ref, vb_ref, kf_ref, vf_ref, qd_ref, kd_ref, vd_ref, zg_ref, gate_ref, nbuf_ref,
         carry_ref) = refs
        buf_ref = None

        @pl.when(pl.program_id(1) == 0)
        def _():
            carry_ref[...] = jnp.zeros_like(carry_ref)

    x = x_ref[...]
    h = _rms_rows(x, gn_ref[...]) * (1.0 + sc_ref[...]) + sh_ref[...]
    hb = h.astype(BF16)
    tm = x.shape[0]
    lane = lax.broadcasted_iota(jnp.int32, (1, LANES), 1)
    lo = lane < SB_DIM

    for c in range(SB_WIDTH // LANES):
        sl = slice(c * LANES, (c + 1) * LANES)
        pq = _dot(hb, w_ref[:, C_SB + c * LANES:C_SB + (c + 1) * LANES])
        qn = _half_lane_rms(pq, lo) * gq_ref[:, sl]
        qb_ref[:, sl] = (qn * (SB_DIM ** -0.5)).astype(BF16)
        pk = _dot(hb, w_ref[:, C_SB + SB_WIDTH + c * LANES:C_SB + SB_WIDTH + (c + 1) * LANES])
        kn = _half_lane_rms(pk, lo) * gk_ref[:, sl]
        kf_ref[:, sl] = kn
        kb_ref[:, sl] = kn.astype(BF16)
    pv = _dot(hb, w_ref[:, C_SB + 2 * SB_WIDTH:C_SB + 3 * SB_WIDTH])
    vf_ref[...] = pv
    vb_ref[...] = pv.astype(BF16)

    cur = _dot(hb, w_ref[:, C_DN:C_Z])
    if grouped:
        y = _causal_conv(cur, wc_ref[...], None, True, buf_ref[...])
        nbuf_ref[...] = pltpu.roll(cur, tm - (SUBLANES - (wc_ref.shape[0] - 1)), 0)
    else:
        y = _causal_conv(cur, wc_ref[...], carry_ref[...], False, None)
        carry_ref[...] = cur[tm - SUBLANES:tm]
        nbuf_ref[...] = cur[tm - SUBLANES:tm]
    a = _silu(y)
    for hh in range(DN_HEADS):
        sl = slice(hh * DN_DIM, (hh + 1) * DN_DIM)
        qh = a[:, hh * DN_DIM:(hh + 1) * DN_DIM]
        kh = a[:, DN_WIDTH + hh * DN_DIM:DN_WIDTH + (hh + 1) * DN_DIM]
        qd_ref[:, sl] = qh * (lax.rsqrt(jnp.sum(qh * qh, axis=-1, keepdims=True) + NORM_EPS) * (DN_DIM ** -0.5))
        kd_ref[:, sl] = kh * lax.rsqrt(jnp.sum(kh * kh, axis=-1, keepdims=True) + NORM_EPS)
    vd_ref[...] = a[:, 2 * DN_WIDTH:3 * DN_WIDTH]

    zg_ref[...] = _silu(_dot(hb, w_ref[:, C_Z:C_G]))

    gb = _dot(hb, w_ref[:, C_G:C_END])
    sp, _ = _softplus_parts(gb + dt_ref[...])
    gate_ref[...] = jnp.where(lane < DN_HEADS, jax.nn.sigmoid(gb), -jnp.exp(al_ref[...]) * sp)


def _inproj(x, sc, sh, gn, w1, gq, gk, wc, al, dt, bufp, *, tm):
    grouped = bufp is not None
    d = x.shape[-1]
    if grouped:
        n = x.shape[0]
        grid = (n // tm,)
        row = lambda c: pl.BlockSpec((tm, c), lambda i: (i, 0))
        in_specs = [row(d), row(d), row(d)]
        lead = (n,)
        sem = ("arbitrary",)
        nb_shape, nb_spec = (n, 3 * DN_WIDTH), row(3 * DN_WIDTH)
        scratch = []
    else:
        b, t, _ = x.shape
        grid = (b, t // tm)
        row = lambda c: pl.BlockSpec((None, tm, c), lambda i, j: (i, j, 0))
        per_b = pl.BlockSpec((None, 1, d), lambda i, j: (i, 0, 0))
        in_specs = [row(d), per_b, per_b]
        lead = (b, t)
        sem = ("parallel", "arbitrary")
        nb_shape = (b, SUBLANES, 3 * DN_WIDTH)
        nb_spec = pl.BlockSpec((None, SUBLANES, 3 * DN_WIDTH), lambda i, j: (i, 0, 0))
        scratch = [pltpu.VMEM((SUBLANES, 3 * DN_WIDTH), F32)]
    in_specs += [_resident(gn.shape), _resident(w1.shape), _resident(gq.shape), _resident(gk.shape),
                 _resident(wc.shape), _resident(al.shape), _resident(dt.shape)]
    args = [x, sc, sh, gn, w1, gq, gk, wc, al, dt]
    if grouped:
        in_specs.append(row(3 * DN_WIDTH))
        args.append(bufp)
    out_cols = [(SB_WIDTH, BF16)] * 3 + [(SB_WIDTH, F32)] * 2 + [(DN_WIDTH, F32)] * 4 + [(LANES, F32)]
    out_shape = [jax.ShapeDtypeStruct(lead + (c,), dt_) for c, dt_ in out_cols]
    out_specs = [row(c) for c, _ in out_cols]
    out_shape.append(jax.ShapeDtypeStruct(nb_shape, F32))
    out_specs.append(nb_spec)
    return pl.pallas_call(
        functools.partial(_inproj_kernel, grouped=grouped),
        grid=grid, in_specs=in_specs, out_specs=out_specs, out_shape=out_shape,
        scratch_shapes=scratch,
        compiler_params=_cparams(sem),
        name="inproj_grouped" if grouped else "inproj_seq",
    )(*args)


def _sb_prompt_kernel(bias_ref, q_ref, k_ref, v_ref, g_ref, o_ref, acc_ref, run_ref, *, tb):
    p = pl.program_id(1)
    i = pl.program_id(2)
    lane = lax.broadcasted_iota(jnp.int32, (1, LANES), 1)
    lo = lane < SB_DIM
    q = q_ref[...]
    zero = jnp.zeros_like(q)
    q2 = jnp.concatenate([jnp.where(lo, q, zero), jnp.where(lo, zero, q)], axis=0)
    head_row = lax.broadcasted_iota(jnp.int32, (2 * tb, 1), 0) < tb
    bias2 = jnp.where(head_row, bias_ref[2 * p], bias_ref[2 * p + 1])
    row = lax.broadcasted_iota(jnp.int32, (tb, tb), 0)
    col = lax.broadcasted_iota(jnp.int32, (tb, tb), 1)
    later_keys = jnp.where(row > col, 1.0, 0.0).astype(BF16)
    qpos = lax.broadcasted_iota(jnp.int32, (2 * tb, tb), 0) & (tb - 1)
    valid = lax.broadcasted_iota(jnp.int32, (2 * tb, tb), 1) < qpos

    acc_ref[...] = jnp.zeros_like(acc_ref)
    run_ref[...] = jnp.zeros_like(run_ref)

    def block(j, masked):
        start = pl.multiple_of(j * tb, tb)
        ks = k_ref[pl.ds(start, tb), :]
        vs = v_ref[pl.ds(start, tb), :]
        z = _dot_nt(q2, ks) + bias2
        sp = jnp.maximum(z, 0.0) + jnp.log(1.0 + _exp_neg_abs(z))
        lb = z - sp
        if masked:
            sp = jnp.where(valid, sp, 0.0)
        later = _dot(sp.astype(BF16), later_keys)
        run = run_ref[...]
        att = jnp.concatenate(
            [jnp.exp(lb[:, c * LANES:(c + 1) * LANES] - later[:, c * LANES:(c + 1) * LANES] - run)
             for c in range(tb // LANES)], axis=1)
        if masked:
            att = jnp.where(valid, att, 0.0)
        acc_ref[...] += _dot(att.astype(BF16), vs)
        run_ref[...] = run + (later[:, 0:1] + sp[:, 0:1])

    def pair(j):
        start = pl.multiple_of((j - 1) * tb, tb)
        ks = k_ref[pl.ds(start, 2 * tb), :]
        vs = v_ref[pl.ds(start, 2 * tb), :]
        z = _dot_nt(q2, ks) + bias2
        sp = jnp.maximum(z, 0.0) + jnp.log(1.0 + _exp_neg_abs(z))
        lb = z - sp
        sp16 = sp.astype(BF16)
        later = _dot(jnp.concatenate([sp16[:, tb:], sp16[:, :tb]], axis=0), later_keys)
        later_new, later_old = later[:2 * tb], later[2 * tb:]
        run_new = run_ref[...]
        run_old = run_new + (later_new[:, 0:1] + sp[:, tb:tb + 1])
        nc = tb // LANES
        att = jnp.concatenate(
            [jnp.exp(lb[:, c * LANES:(c + 1) * LANES] - later_old[:, c * LANES:(c + 1) * LANES] - run_old)
             for c in range(nc)]
            + [jnp.exp(lb[:, tb + c * LANES:tb + (c + 1) * LANES] - later_new[:, c * LANES:(c + 1) * LANES] - run_new)
               for c in range(nc)], axis=1)
        acc_ref[...] += _dot(att.astype(BF16), vs)
        run_ref[...] = run_old + (later_old[:, 0:1] + sp[:, 0:1])

    block(i, True)
    odd = i & 1

    @pl.when(odd == 1)
    def _():
        block(i - 1, False)

    def body(jj, carry):
        pair(i - 1 - odd - 2 * jj)
        return carry

    lax.fori_loop(0, i >> 1, body, 0)

    o = jnp.where(lo, acc_ref[0:tb], acc_ref[tb:2 * tb])
    o_ref[...] = (_half_lane_rms(o, lo) * g_ref[...]).astype(o_ref.dtype)


def _sb_prompt(qb, kb, vb, bias, g2, *, tb):
    b, t, w = qb.shape
    pairs = w // LANES
    return pl.pallas_call(
        functools.partial(_sb_prompt_kernel, tb=tb),
        grid_spec=pltpu.PrefetchScalarGridSpec(
            num_scalar_prefetch=0,
            grid=(b, pairs, t // tb),
            in_specs=[pl.BlockSpec(memory_space=pltpu.SMEM),
                      pl.BlockSpec((None, tb, LANES), lambda i, p, j: (i, j, p)),
                      pl.BlockSpec((None, t, LANES), lambda i, p, j: (i, 0, p)),
                      pl.BlockSpec((None, t, LANES), lambda i, p, j: (i, 0, p)),
                      pl.BlockSpec((1, LANES), lambda i, p, j: (0, 0))],
            out_specs=pl.BlockSpec((None, tb, LANES), lambda i, p, j: (i, j, p)),
            scratch_shapes=[pltpu.VMEM((2 * tb, LANES), F32), pltpu.VMEM((2 * tb, LANES), F32)]),
        out_shape=jax.ShapeDtypeStruct((b, t, w), BF16),
        compiler_params=_cparams(("parallel", "parallel", "arbitrary")),
        name="sb_attn_prompt",
    )(bias, qb, kb, vb, g2)


def _sb_sample_kernel(*refs, pps, n_q):
    pt_ref, bias_ref, q_ref, kn_ref, vn_ref, g_ref = refs[:6]
    kp_refs = refs[6:6 + pps]
    vp_refs = refs[6 + pps:6 + 2 * pps]
    o_ref = refs[6 + 2 * pps]
    qx_ref, acc_ref, run_ref, newk_ref, newv_ref = refs[7 + 2 * pps:]
    del pt_ref
    j = pl.program_id(1)
    rows = SB_HEADS * n_q
    page = kp_refs[0].shape[1]
    width = SB_WIDTH
    r1 = lax.broadcasted_iota(jnp.int32, (rows, 1), 0)
    later_keys = jnp.where(lax.broadcasted_iota(jnp.int32, (page, page), 0)
                           > lax.broadcasted_iota(jnp.int32, (page, page), 1), 1.0, 0.0).astype(BF16)

    def new_token_block(kf, vf):
        z = _dot_nt(qx_ref[...], kf.astype(BF16)) + bias_ref[...]
        sp, lb = _softplus_parts(z)
        valid = lax.broadcasted_iota(jnp.int32, (rows, page), 1) < (r1 & (n_q - 1))
        sp = jnp.where(valid, sp, 0.0)
        arg = lb - _dot(sp.astype(BF16), later_keys) - run_ref[...]
        att = jnp.where(valid, jnp.exp(arg), 0.0)
        acc_ref[...] += _dot(att.astype(BF16), vf.astype(BF16))
        run_ref[...] += jnp.sum(sp, axis=-1, keepdims=True)

    def cached_pages():
        k_all = jnp.concatenate([kp_refs[r][...].astype(BF16) for r in range(pps)], axis=1)
        z = _dot(qx_ref[...], k_all) + bias_ref[...]
        sp, lb = _softplus_parts(z)
        sp16 = sp.astype(BF16)
        later = _dot(jnp.concatenate([sp16[:, r * page:(r + 1) * page] for r in range(pps)], axis=0),
                     later_keys)
        run = run_ref[...]
        args = []
        for r in range(pps):
            cs = slice(r * page, (r + 1) * page)
            args.append(lb[:, cs] - later[r * rows:(r + 1) * rows] - run)
            run = run + jnp.sum(sp[:, cs], axis=-1, keepdims=True)
        run_ref[...] = run
        att = jnp.exp(jnp.concatenate(args, axis=1)).astype(BF16)
        v_all = jnp.concatenate([vp_refs[r][...].astype(BF16) for r in range(pps)], axis=1)
        acc_ref[...] += _dot_nt(att, v_all)

    @pl.when(j == 0)
    def _():
        head_of_lane = lax.broadcasted_iota(jnp.int32, (rows, width), 1) // SB_DIM
        head_of_row = lax.broadcasted_iota(jnp.int32, (rows, width), 0) // n_q
        qrep = jnp.concatenate([q_ref[...].astype(F32)] * SB_HEADS, axis=0)
        qx_ref[...] = jnp.where(head_of_lane == head_of_row, qrep, 0.0).astype(BF16)
        acc_ref[...] = jnp.zeros_like(acc_ref)
        run_ref[...] = jnp.zeros_like(run_ref)
        newk_ref[...] = jnp.zeros_like(newk_ref)
        newv_ref[...] = jnp.zeros_like(newv_ref)
        newk_ref[0:n_q, :] = kn_ref[...]
        newv_ref[0:n_q, :] = vn_ref[...]
        new_token_block(newk_ref[...], newv_ref[...])

    cached_pages()

    @pl.when(j == pl.num_programs(1) - 1)
    def _():
        res = acc_ref[...]
        lane_head = lax.broadcasted_iota(jnp.int32, (n_q, width), 1) // SB_DIM
        o = jnp.zeros((n_q, width), F32)
        for hh in range(SB_HEADS):
            o = o + jnp.where(lane_head == hh, res[hh * n_q:(hh + 1) * n_q, :], 0.0)
        lo = lax.broadcasted_iota(jnp.int32, (1, LANES), 1) < SB_DIM
        for c in range(width // LANES):
            sl = slice(c * LANES, (c + 1) * LANES)
            o_ref[:, sl] = _half_lane_rms(o[:, sl], lo) * g_ref[:, sl]


def _sb_sample(page_table, bias_rows, q3, kn3, vn3, g_sb, cache_k, cache_v):
    bs, n_q, w = q3.shape
    n_pages = page_table.shape[1]
    page = cache_k.shape[2]
    pps = PAGES_PER_STEP
    rows = SB_HEADS * n_q

    def page_spec(r):
        return pl.BlockSpec((None, w, page),
                            lambda i, j, pt: (pt[i, n_pages - 1 - (j * pps + r)], 0, 0))

    tok = lambda: pl.BlockSpec((None, n_q, w), lambda i, j, pt: (i, 0, 0))
    const = lambda shp: pl.BlockSpec(shp, lambda i, j, pt: (0,) * len(shp))
    return pl.pallas_call(
        functools.partial(_sb_sample_kernel, pps=pps, n_q=n_q),
        grid_spec=pltpu.PrefetchScalarGridSpec(
            num_scalar_prefetch=1,
            grid=(bs, n_pages // pps),
            in_specs=[const((rows, 1)), tok(), tok(), tok(), const((1, w))]
                     + [page_spec(r) for r in range(pps)] + [page_spec(r) for r in range(pps)],
            out_specs=tok(),
            scratch_shapes=[pltpu.VMEM((rows, w), BF16), pltpu.VMEM((rows, w), F32),
                            pltpu.VMEM((rows, 1), F32), pltpu.VMEM((page, w), F32),
                            pltpu.VMEM((page, w), F32)]),
        out_shape=jax.ShapeDtypeStruct((bs, n_q, w), F32),
        compiler_params=_cparams(("parallel", "arbitrary")),
        name="sb_attn_paged",
    )(page_table, bias_rows, q3, kn3, vn3, g_sb, *([cache_k] * pps), *([cache_v] * pps))


def _split3(x):
    hi = x.astype(BF16)
    r = x - hi.astype(F32)
    mid = r.astype(BF16)
    lo = (r - mid.astype(F32)).astype(BF16)
    return hi, mid, lo


def _dn_local(q_ref, k_ref, v_ref, gates, group):
    n = gates.shape[0]
    heads = range(DN_HEADS)
    hs = [slice(h * DN_DIM, (h + 1) * DN_DIM) for h in heads]
    row = lax.broadcasted_iota(jnp.int32, (n, n), 0)
    col = lax.broadcasted_iota(jnp.int32, (n, n), 1)
    shift = group.bit_length() - 1
    same = jnp.where((row >> shift) == (col >> shift), 1.0, 0.0)
    low = jnp.where(row >= col, same, 0.0)
    strict = jnp.where(row > col, same, 0.0)
    gparts = _split3(gates)
    low16 = low.astype(BF16)
    same16 = same.astype(BF16)
    g_all = sum(_dot(low16, m) for m in gparts)
    t_all = sum(_dot(same16, m) for m in gparts)
    g_row = [jnp.broadcast_to(g_all[:, DN_HEADS + h:DN_HEADS + h + 1], (n, n)) for h in heads]
    gtot = [jnp.broadcast_to(t_all[:, DN_HEADS + h:DN_HEADS + h + 1], (n, n)) for h in heads]
    decay = [low * jnp.exp(jnp.minimum(g_row[h] - g_row[h].T, 0.0)) for h in heads]
    beta = [gates[:, h:h + 1] for h in heads]
    k = [k_ref[:, hs[h]] for h in heads]
    k16 = [k[h].astype(BF16) for h in heads]
    kb = [k[h] * beta[h] for h in heads]
    a_mat = [strict * (_dot_nt(kb[h].astype(BF16), k16[h]) * decay[h]) for h in heads]
    eye = jnp.where(row == col, 1.0, 0.0)
    inv = [eye - a_mat[h] * jnp.where((row >> 1) == (col >> 1), 1.0, 0.0) for h in heads]
    for lb in range(1, shift):
        sel = jnp.where((row >> (lb + 1)) == (col >> (lb + 1)),
                        jnp.where(((row >> lb) & 1) > ((col >> lb) & 1), 1.0, 0.0), 0.0)
        inv16 = [inv[h].astype(BF16) for h in heads]
        x = [_dot((a_mat[h] * sel).astype(BF16), inv16[h]) for h in heads]
        inv = [inv[h] - _dot(inv16[h], x[h].astype(BF16)) for h in heads]
    eg = [jnp.exp(g_row[h]) for h in heads]
    rhs = [jnp.concatenate([v_ref[:, hs[h]] * beta[h], kb[h] * eg[h]], axis=-1).astype(BF16) for h in heads]
    sol = [_dot(inv[h].astype(BF16), rhs[h]) for h in heads]
    q = [q_ref[:, hs[h]] for h in heads]
    intra = [_dot_nt(q[h].astype(BF16), k16[h]) * decay[h] for h in heads]
    u = [sol[h][:, :DN_DIM] for h in heads]
    kcum = [sol[h][:, DN_DIM:] for h in heads]
    qdec = [q[h] * eg[h] for h in heads]
    kdec = [k[h] * jnp.exp(gtot[h] - g_row[h]) for h in heads]
    return u, kcum, intra, qdec, kdec, gtot


def _dn_seq_kernel(q_ref, k_ref, v_ref, gate_ref, o_ref, sout_ref, s_ref, *, group):
    @pl.when(pl.program_id(1) == 0)
    def _():
        s_ref[...] = jnp.zeros_like(s_ref)

    n = q_ref.shape[0]
    heads = range(DN_HEADS)
    u, kcum, intra, qdec, kdec, gtot = _dn_local(q_ref, k_ref, v_ref, gate_ref[...], group)
    s = [s_ref[h] for h in heads]
    vnews = [[] for _ in heads]
    outs = [[] for _ in heads]
    for c in range(n // group):
        rs = slice(c * group, (c + 1) * group)
        s16 = [s[h].astype(BF16) for h in heads]
        lhs = [jnp.concatenate([kcum[h][rs], qdec[h][rs]], axis=0).astype(BF16) for h in heads]
        both = [_dot(lhs[h], s16[h]) for h in heads]
        vnew = [u[h][rs] - both[h][:group] for h in heads]
        upd = [_dot_tn(kdec[h][rs].astype(BF16), vnew[h].astype(BF16)) for h in heads]
        s = [s[h] * jnp.exp(gtot[h][c * group:c * group + 1, :]) + upd[h] for h in heads]
        for h in heads:
            vnews[h].append(vnew[h])
            outs[h].append(both[h][group:])
    for h in heads:
        s_ref[h] = s[h]
        vfull = jnp.concatenate(vnews[h], axis=0).astype(BF16)
        o_ref[:, h * DN_DIM:(h + 1) * DN_DIM] = (jnp.concatenate(outs[h], axis=0)
                                                 + _dot(intra[h].astype(BF16), vfull))

    @pl.when(pl.program_id(1) == pl.num_programs(1) - 1)
    def _():
        sout_ref[...] = s_ref[...]


def _dn_seq(qd, kd, vd, gates):
    b, t, w = qd.shape
    n = DN_BLOCK
    row = lambda c: pl.BlockSpec((None, n, c), lambda i, j: (i, j, 0))
    return pl.pallas_call(
        functools.partial(_dn_seq_kernel, group=min(64, n)),
        grid=(b, t // n),
        in_specs=[row(w), row(w), row(w), row(LANES)],
        out_specs=[row(w), pl.BlockSpec((None, DN_HEADS, DN_DIM, DN_DIM), lambda i, j: (i, 0, 0, 0))],
        out_shape=[jax.ShapeDtypeStruct((b, t, w), F32),
                   jax.ShapeDtypeStruct((b, DN_HEADS, DN_DIM, DN_DIM), F32)],
        scratch_shapes=[pltpu.VMEM((DN_HEADS, DN_DIM, DN_DIM), F32)],
        compiler_params=_cparams(("parallel", "arbitrary")),
        name="deltanet_seq",
    )(qd, kd, vd, gates)


def _dn_grouped_kernel(q_ref, k_ref, v_ref, gate_ref, s0_ref, o_ref, sout_ref,
                       u_ref, kc_ref, qd_ref, kd_ref, gt_ref, vn_ref, oq_ref, *, group):
    n = q_ref.shape[0]
    heads = range(DN_HEADS)
    row1 = lax.broadcasted_iota(jnp.int32, (n, 1), 0)
    shift = group.bit_length() - 1
    u, kcum, intra, qdec, kdec, gtot = _dn_local(q_ref, k_ref, v_ref, gate_ref[...], group)
    for h in heads:
        u_ref[h] = u[h]
        kc_ref[h] = kcum[h]
        qd_ref[h] = qdec[h]
        kd_ref[h] = kdec[h]
        gt_ref[h] = gtot[h]
    vn_ref[...] = jnp.zeros_like(vn_ref)

    def per_seq(bi, carry):
        rs = pl.ds(pl.multiple_of(bi * group, group), group)
        mine = (row1 >> shift) == bi
        for h in heads:
            s = s0_ref[bi, h]
            vnew = u_ref[h, rs, :] - _dot(kc_ref[h, rs, :], s)
            oq_ref[h, rs, :] = _dot(qd_ref[h, rs, :], s)
            vn_ref[h, rs, :] = vnew
            kmask = jnp.where(mine, kd_ref[h], 0.0).astype(BF16)
            sout_ref[bi, h] = (s * jnp.exp(gt_ref[h, rs, :][0:1, :])
                               + _dot_tn(kmask, vn_ref[h].astype(BF16)))
        return carry

    lax.fori_loop(0, n // group, per_seq, 0)
    for h in heads:
        o_ref[:, h * DN_DIM:(h + 1) * DN_DIM] = (oq_ref[h]
                                                 + _dot(intra[h].astype(BF16), vn_ref[h].astype(BF16)))


def _dn_grouped(qd, kd, vd, gates, s0, *, group):
    nrows, w = qd.shape
    n = DN_BLOCK
    per = n // group
    row = lambda c: pl.BlockSpec((n, c), lambda i: (i, 0))
    st = pl.BlockSpec((per, DN_HEADS, DN_DIM, DN_DIM), lambda i: (i, 0, 0, 0))
    return pl.pallas_call(
        functools.partial(_dn_grouped_kernel, group=group),
        grid=(nrows // n,),
        in_specs=[row(w), row(w), row(w), row(LANES), st],
        out_specs=[row(w), st],
        out_shape=[jax.ShapeDtypeStruct((nrows, w), F32), jax.ShapeDtypeStruct(s0.shape, F32)],
        scratch_shapes=[pltpu.VMEM((DN_HEADS, n, DN_DIM), F32)] * 7,
        compiler_params=_cparams(("parallel",)),
        name="deltanet_grouped",
    )(qd, kd, vd, gates, s0)


def _outproj_kernel(osb_ref, odn_ref, zg_ref, gdn_ref, x_ref, gt_ref, sc_ref, sh_ref, gn_ref, w_ref,
                    x1_ref, h2_ref):
    parts = [osb_ref[...].astype(BF16)]
    for hh in range(DN_HEADS):
        sl = slice(hh * DN_DIM, (hh + 1) * DN_DIM)
        parts.append((_rms_rows(odn_ref[:, sl], gdn_ref[...]) * zg_ref[:, sl]).astype(BF16))
    mix = _dot(jnp.concatenate(parts, axis=1), w_ref[...])
    x1 = x_ref[...] + gt_ref[...] * mix
    x1_ref[...] = x1
    h2_ref[...] = (_rms_rows(x1, gn_ref[...]) * (1.0 + sc_ref[...]) + sh_ref[...]).astype(BF16)


def _mod_specs(x, tm, per_row):
    d = x.shape[-1]
    if per_row:
        n = x.shape[0]
        row = lambda c: pl.BlockSpec((tm, c), lambda i: (i, 0))
        return (n // tm,), row, row(d), (n,), ("arbitrary",)
    b, t, _ = x.shape
    row = lambda c: pl.BlockSpec((None, tm, c), lambda i, j: (i, j, 0))
    per_b = pl.BlockSpec((None, 1, d), lambda i, j: (i, 0, 0))
    return (b, t // tm), row, per_b, (b, t), ("parallel", "arbitrary")


def _outproj(osb, odn, zg, gdn, x, gt, sc, sh, gn, w, *, tm, per_row):
    d = x.shape[-1]
    grid, row, mod, lead, sem = _mod_specs(x, tm, per_row)
    return pl.pallas_call(
        _outproj_kernel,
        grid=grid,
        in_specs=[row(SB_WIDTH), row(DN_WIDTH), row(DN_WIDTH), _resident(gdn.shape), row(d), mod, mod, mod,
                  _resident(gn.shape), _resident(w.shape)],
        out_specs=[row(d), row(d)],
        out_shape=[jax.ShapeDtypeStruct(lead + (d,), F32), jax.ShapeDtypeStruct(lead + (d,), BF16)],
        compiler_params=_cparams(sem),
        name="outproj_rows" if per_row else "outproj_seq",
    )(osb, odn, zg, gdn, x, gt, sc, sh, gn, w)


def _ffn_kernel(*refs, grouped, cw):
    if grouped:
        h_ref, x1_ref, gt_ref, wu_ref, wc_ref, wd_ref, buf_ref, y_ref, nbuf_ref, act_ref = refs
        carry_ref = None
    else:
        h_ref, x1_ref, gt_ref, wu_ref, wc_ref, wd_ref, y_ref, nbuf_ref, act_ref, carry_ref = refs
        buf_ref = None

        @pl.when(pl.program_id(1) == 0)
        def _():
            carry_ref[...] = jnp.zeros_like(carry_ref)

    hb = h_ref[...]
    tm = hb.shape[0]
    dff = wd_ref.shape[0]
    taps = wc_ref.shape[0]
    for c in range(dff // cw):
        halves = []
        for base in (0, dff):
            cs = slice(base + c * cw, base + (c + 1) * cw)
            cur = _dot(hb, wu_ref[:, cs])
            if grouped:
                y = _causal_conv(cur, wc_ref[:, cs], None, True, buf_ref[:, cs])
                nbuf_ref[:, cs] = pltpu.roll(cur, tm - (SUBLANES - (taps - 1)), 0)
            else:
                y = _causal_conv(cur, wc_ref[:, cs], carry_ref[:, cs], False, None)
                carry_ref[:, cs] = cur[tm - SUBLANES:tm]
                nbuf_ref[:, cs] = cur[tm - SUBLANES:tm]
            halves.append(y)
        act_ref[:, c * cw:(c + 1) * cw] = (_silu(halves[1]) * halves[0]).astype(BF16)
    y_ref[...] = x1_ref[...] + gt_ref[...] * _dot(act_ref[...], wd_ref[...])


def _ffn(h2, x1, gt, wu, wc, wd, bufp, *, tm, per_row, cw=256):
    d = x1.shape[-1]
    up = wu.shape[1]
    grid, row, mod, lead, sem = _mod_specs(x1, tm, per_row)
    in_specs = [row(d), row(d), mod, _resident(wu.shape), _resident(wc.shape), _resident(wd.shape)]
    args = [h2, x1, gt, wu, wc, wd]
    if per_row:
        in_specs.append(row(up))
        args.append(bufp)
        nb_shape, nb_spec, scratch = lead + (up,), row(up), [pltpu.VMEM((tm, up // 2), BF16)]
    else:
        nb_shape = (lead[0], SUBLANES, up)
        nb_spec = pl.BlockSpec((None, SUBLANES, up), lambda i, j: (i, 0, 0))
        scratch = [pltpu.VMEM((tm, up // 2), BF16), pltpu.VMEM((SUBLANES, up), F32)]
    return pl.pallas_call(
        functools.partial(_ffn_kernel, grouped=per_row, cw=cw),
        grid=grid, in_specs=in_specs,
        out_specs=[row(d), nb_spec],
        out_shape=[jax.ShapeDtypeStruct(lead + (d,), F32), jax.ShapeDtypeStruct(nb_shape, F32)],
        scratch_shapes=scratch,
        compiler_params=_cparams(sem),
        name="convffn_rows" if per_row else "convffn_seq",
    )(*args)


def _prep_weights(g_attn_norm, w_in, g_q, g_k, sb_bias, g_sb_out, w_dn_conv, a_log, dt_bias, g_dn_out,
                  w_out, g_ffn_norm, w_up, w_ffn_conv, w_down):
    d = w_in.shape[0]
    o1 = 3 * SB_WIDTH
    o2 = o1 + 3 * DN_WIDTH
    o3 = o2 + 2 * DN_HEADS
    pad = jnp.zeros((d, LANES - 2 * DN_HEADS), w_in.dtype)
    w1 = jnp.concatenate([w_in[:, :o2], w_in[:, o3:], w_in[:, o2:o3], pad], axis=1).astype(BF16)
    lane_pad = lambda v: jnp.zeros((1, LANES), F32).at[0, DN_HEADS:2 * DN_HEADS].set(v)
    return dict(
        gn1=g_attn_norm.reshape(1, d), w1=w1,
        gq=jnp.tile(g_q, SB_HEADS).reshape(1, SB_WIDTH), gk=jnp.tile(g_k, SB_HEADS).reshape(1, SB_WIDTH),
        bias=sb_bias.astype(F32), g_sb=jnp.tile(g_sb_out, SB_HEADS).reshape(1, SB_WIDTH),
        wc_dn=w_dn_conv, al=lane_pad(a_log), dt=lane_pad(dt_bias), gdn=g_dn_out.reshape(1, DN_DIM),
        w_out=w_out.astype(BF16), gn2=g_ffn_norm.reshape(1, d), w_up=w_up.astype(BF16),
        wc_ffn=w_ffn_conv, w_down=w_down.astype(BF16))


def _layer_prompt(x, mod, p):
    b, t, d = x.shape
    sh1, sc1, gt1, sh2, sc2, gt2 = [m.reshape(b, 1, d) for m in jnp.split(mod, 6, axis=-1)]
    qb, kb, vb, kf, vf, qd, kd, vd, zg, gates, dnbuf = _inproj(
        x, sc1, sh1, p["gn1"], p["w1"], p["gq"], p["gk"], p["wc_dn"], p["al"], p["dt"], None, tm=512)
    osb = _sb_prompt(qb, kb, vb, p["bias"], p["g_sb"][:, :LANES], tb=256)
    odn, s_new = _dn_seq(qd, kd, vd, gates)
    x1, h2 = _outproj(osb, odn, zg, p["gdn"], x, gt1, sc2, sh2, p["gn2"], p["w_out"], tm=512, per_row=False)
    y, ffbuf = _ffn(h2, x1, gt2, p["w_up"], p["wc_ffn"], p["w_down"], None, tm=512, per_row=False)
    k_dn = p["wc_dn"].shape[0] - 1
    k_ff = p["wc_ffn"].shape[0] - 1
    return (y, kf.reshape(b, t, SB_HEADS, SB_DIM), vf.reshape(b, t, SB_HEADS, SB_DIM), s_new,
            dnbuf[:, SUBLANES - k_dn:], ffbuf[:, SUBLANES - k_ff:])


def _layer_sample(x, mod, page_table, cache_k, cache_v, s0, dn_buf, ffn_buf, p):
    bs, n_q, d = x.shape
    assert n_q == SUBLANES
    n = bs * n_q
    rep = lambda m: jnp.broadcast_to(m[:, None, :], (bs, n_q, d)).reshape(n, d)
    sh1, sc1, gt1, sh2, sc2, gt2 = [rep(m) for m in jnp.split(mod, 6, axis=-1)]
    padrows = lambda buf: jnp.pad(buf, ((0, 0), (0, n_q - buf.shape[1]), (0, 0))).reshape(n, buf.shape[2])
    xf = x.reshape(n, d)
    qb, kb, vb, kf, vf, qd, kd, vd, zg, gates, dnbuf = _inproj(
        xf, sc1, sh1, p["gn1"], p["w1"], p["gq"], p["gk"], p["wc_dn"], p["al"], p["dt"], padrows(dn_buf), tm=128)
    del kb, vb
    w = SB_WIDTH
    bias_rows = jnp.repeat(p["bias"], n_q).reshape(SB_HEADS * n_q, 1)
    n_phys, page = cache_k.shape[0], cache_k.shape[1]
    pool_t = lambda c: jnp.transpose(c, (0, 2, 3, 1)).reshape(n_phys, w, page)
    osb = _sb_sample(page_table, bias_rows, qb.reshape(bs, n_q, w), kf.reshape(bs, n_q, w),
                     vf.reshape(bs, n_q, w), p["g_sb"], pool_t(cache_k), pool_t(cache_v))
    odn, s_new = _dn_grouped(qd, kd, vd, gates, s0, group=n_q)
    x1, h2 = _outproj(osb.reshape(n, w), odn, zg, p["gdn"], xf, gt1, sc2, sh2, p["gn2"], p["w_out"],
                      tm=128, per_row=True)
    y, ffbuf = _ffn(h2, x1, gt2, p["w_up"], p["wc_ffn"], p["w_down"], padrows(ffn_buf), tm=128, per_row=True)
    k_dn = p["wc_dn"].shape[0] - 1
    k_ff = p["wc_ffn"].shape[0] - 1
    return (y.reshape(bs, n_q, d), kf.reshape(bs, n_q, SB_HEADS, SB_DIM), vf.reshape(bs, n_q, SB_HEADS, SB_DIM),
            s_new, dnbuf.reshape(bs, n_q, -1)[:, :k_dn], ffbuf.reshape(bs, n_q, -1)[:, :k_ff])


def kernel(x_prompt, x_sample, c_prompt, c_sample, cache_k, cache_v, page_table, state_delta, state_dn_conv, state_ffn_conv, w_ada, b_ada, g_attn_norm, w_in, g_q, g_k, sb_bias, g_sb_out, w_dn_conv, a_log, dt_bias, g_dn_out, w_out, g_ffn_norm, w_up, w_ffn_conv, w_down):
    depth = w_ada.shape[0]
    bp = x_prompt.shape[0]
    yp, ys = x_prompt, x_sample
    outs = [[] for _ in range(10)]
    c_all = jnp.concatenate([c_prompt, c_sample], axis=0)
    pad_rows = (-c_all.shape[0]) % SUBLANES
    c_all = jnp.pad(c_all, ((0, pad_rows), (0, 0)))
    for l in range(depth):
        p = _prep_weights(g_attn_norm[l], w_in[l], g_q[l], g_k[l], sb_bias[l], g_sb_out[l], w_dn_conv[l],
                          a_log[l], dt_bias[l], g_dn_out[l], w_out[l], g_ffn_norm[l], w_up[l],
                          w_ffn_conv[l], w_down[l])
        mod = _ada(c_all, w_ada[l], b_ada[l])
        yp, kp, vp, sp, dcp, fcp = _layer_prompt(yp, mod[:bp], p)
        ys, ks, vs, ss, dcs, fcs = _layer_sample(ys, mod[bp:bp + x_sample.shape[0]], page_table,
                                                 cache_k[l], cache_v[l], state_delta[l],
                                                 state_dn_conv[l], state_ffn_conv[l], p)
        for lst, val in zip(outs, (kp, vp, ks, vs, sp, ss, dcp, dcs, fcp, fcs)):
            lst.append(val)
    return (yp, ys) + tuple(jnp.stack(o) for o in outs)
```

```python
import functools

import jax
import jax.numpy as jnp
from jax import lax
from jax.experimental import pallas as pl
from jax.experimental.pallas import tpu as pltpu

F32 = jnp.float32
BF16 = jnp.bfloat16

NORM_EPS = 1e-6
LOG2E = 1.4426950408889634
LANES = 128
SUBLANES = 8
VMEM_LIMIT = 56 * 1024 * 1024

SB_HEADS = 8
SB_DIM = 64
SB_WIDTH = SB_HEADS * SB_DIM
DN_HEADS = 4
DN_DIM = 128
DN_WIDTH = DN_HEADS * DN_DIM
DN_BLOCK = 128
PAGES_PER_STEP = 8


def _dot(a, b):
    return jnp.dot(a, b, preferred_element_type=F32)


def _dot_nt(a, b):
    return lax.dot_general(a, b, (((1,), (1,)), ((), ())), preferred_element_type=F32)


def _dot_tn(a, b):
    return lax.dot_general(a, b, (((0,), (0,)), ((), ())), preferred_element_type=F32)


def _silu(x):
    return x * jax.nn.sigmoid(x)


def _exp_neg_abs(x):
    return jnp.exp2(jnp.abs(x) * (-LOG2E))


def _softplus_parts(z):
    l = jnp.log(1.0 + jnp.exp(-jnp.abs(z)))
    return jnp.maximum(z, 0.0) + l, jnp.minimum(z, 0.0) - l


def _rms_rows(x, g):
    ms = jnp.mean(x * x, axis=-1, keepdims=True)
    return x * lax.rsqrt(ms + NORM_EPS) * g


def _half_lane_rms(blk, lo):
    sq = blk * blk
    s_lo = jnp.sum(jnp.where(lo, sq, 0.0), axis=-1, keepdims=True)
    s_hi = jnp.sum(jnp.where(lo, 0.0, sq), axis=-1, keepdims=True)
    ms = jnp.where(lo, s_lo, s_hi) * (1.0 / SB_DIM)
    return blk * lax.rsqrt(ms + NORM_EPS)


def _cparams(sem):
    return pltpu.CompilerParams(dimension_semantics=sem, vmem_limit_bytes=VMEM_LIMIT)


def _resident(shape):
    nd = len(shape)
    return pl.BlockSpec(shape, lambda *_: (0,) * nd, pipeline_mode=pl.Buffered(1))


def _ada_kernel(c_ref, w_ref, b_ref, o_ref):
    a = _silu(c_ref[...]).astype(BF16)
    o_ref[...] = _dot(a, w_ref[...].astype(BF16)) + b_ref[...]


def _ada(c, w, b):
    m, d = c.shape
    n = w.shape[1]
    tn = 512
    return pl.pallas_call(
        _ada_kernel,
        grid=(n // tn,),
        in_specs=[pl.BlockSpec((m, d), lambda j: (0, 0)),
                  pl.BlockSpec((d, tn), lambda j: (0, j)),
                  pl.BlockSpec((1, tn), lambda j: (0, j))],
        out_specs=pl.BlockSpec((m, tn), lambda j: (0, j)),
        out_shape=jax.ShapeDtypeStruct((m, n), F32),
        compiler_params=_cparams(("parallel",)),
        name="adaln_mod",
    )(c, w, b.reshape(1, n))


C_SB = 0
C_DN = 3 * SB_WIDTH
C_Z = C_DN + 3 * DN_WIDTH
C_G = C_Z + DN_WIDTH
C_END = C_G + LANES


def _causal_conv(cur, taps, prev_rows, grouped, bufp):
    w = taps.shape[0]
    tm = cur.shape[0]
    y = taps[w - 1:w, :] * cur
    if grouped:
        rowm = lax.broadcasted_iota(jnp.int32, (tm, 1), 0) & (SUBLANES - 1)
        for s in range(1, w):
            r = pltpu.roll(cur, s, 0)
            back = (w - 1) - s
            bs = bufp if back == 0 else pltpu.roll(bufp, tm - back, 0)
            y = y + taps[w - 1 - s:w - s, :] * jnp.where(rowm >= s, r, bs)
        return y
    row8 = lax.broadcasted_iota(jnp.int32, (SUBLANES, 1), 0)
    yh = taps[w - 1:w, :] * cur[0:SUBLANES]
    for s in range(1, w):
        r = pltpu.roll(cur, s, 0)
        y = y + taps[w - 1 - s:w - s, :] * r
        head = jnp.where(row8 < s, pltpu.roll(prev_rows, s, 0), r[0:SUBLANES])
        yh = yh + taps[w - 1 - s:w - s, :] * head
    return jnp.concatenate([yh, y[SUBLANES:]], axis=0)


def _inproj_kernel(*refs, grouped):
    if grouped:
        (x_ref, sc_ref, sh_ref, gn_ref, w_ref, gq_ref, gk_ref, wc_ref, al_ref, dt_ref, buf_ref,
         qb_ref, kb_ref, vb_ref, kf_ref, vf_ref, qd_ref, kd_ref, vd_ref, zg_ref, gate_ref, nbuf_ref) = refs
        carry_ref = None
    else:
        (x_ref, sc_ref, sh_ref, gn_ref, w_ref, gq_ref, gk_ref, wc_ref, al_ref, dt_ref,
         qb_ref, kb_ref, vb_ref, kf_ref, vf_ref, qd_ref, kd_ref, vd_ref, zg_ref, gate_ref, nbuf_ref,
         carry_ref) = refs
        buf_ref = None

        @pl.when(pl.program_id(1) == 0)
        def _():
            carry_ref[...] = jnp.zeros_like(carry_ref)

    x = x_ref[...]
    h = _rms_rows(x, gn_ref[...]) * (1.0 + sc_ref[...]) + sh_ref[...]
    hb = h.astype(BF16)
    tm = x.shape[0]
    lane = lax.broadcasted_iota(jnp.int32, (1, LANES), 1)
    lo = lane < SB_DIM

    for c in range(SB_WIDTH // LANES):
        sl = slice(c * LANES, (c + 1) * LANES)
        pq = _dot(hb, w_ref[:, C_SB + c * LANES:C_SB + (c + 1) * LANES])
        qn = _half_lane_rms(pq, lo) * gq_ref[:, sl]
        qb_ref[:, sl] = (qn * (SB_DIM ** -0.5)).astype(BF16)
        pk = _dot(hb, w_ref[:, C_SB + SB_WIDTH + c * LANES:C_SB + SB_WIDTH + (c + 1) * LANES])
        kn = _half_lane_rms(pk, lo) * gk_ref[:, sl]
        kf_ref[:, sl] = kn
        kb_ref[:, sl] = kn.astype(BF16)
    pv = _dot(hb, w_ref[:, C_SB + 2 * SB_WIDTH:C_SB + 3 * SB_WIDTH])
    vf_ref[...] = pv
    vb_ref[...] = pv.astype(BF16)

    cur = _dot(hb, w_ref[:, C_DN:C_Z])
    if grouped:
        y = _causal_conv(cur, wc_ref[...], None, True, buf_ref[...])
        nbuf_ref[...] = pltpu.roll(cur, tm - (SUBLANES - (wc_ref.shape[0] - 1)), 0)
    else:
        y = _causal_conv(cur, wc_ref[...], carry_ref[...], False, None)
        carry_ref[...] = cur[tm - SUBLANES:tm]
        nbuf_ref[...] = cur[tm - SUBLANES:tm]
    a = _silu(y)
    for hh in range(DN_HEADS):
        sl = slice(hh * DN_DIM, (hh + 1) * DN_DIM)
        qh = a[:, hh * DN_DIM:(hh + 1) * DN_DIM]
        kh = a[:, DN_WIDTH + hh * DN_DIM:DN_WIDTH + (hh + 1) * DN_DIM]
        qd_ref[:, sl] = qh * (lax.rsqrt(jnp.sum(qh * qh, axis=-1, keepdims=True) + NORM_EPS) * (DN_DIM ** -0.5))
        kd_ref[:, sl] = kh * lax.rsqrt(jnp.sum(kh * kh, axis=-1, keepdims=True) + NORM_EPS)
    vd_ref[...] = a[:, 2 * DN_WIDTH:3 * DN_WIDTH]

    zg_ref[...] = _silu(_dot(hb, w_ref[:, C_Z:C_G]))

    gb = _dot(hb, w_ref[:, C_G:C_END])
    sp, _ = _softplus_parts(gb + dt_ref[...])
    gate_ref[...] = jnp.where(lane < DN_HEADS, jax.nn.sigmoid(gb), -jnp.exp(al_ref[...]) * sp)


def _inproj(x, sc, sh, gn, w1, gq, gk, wc, al, dt, bufp, *, tm):
    grouped = bufp is not None
    d = x.shape[-1]
    if grouped:
        n = x.shape[0]
        grid = (n // tm,)
        row = lambda c: pl.BlockSpec((tm, c), lambda i: (i, 0))
        in_specs = [row(d), row(d), row(d)]
        lead = (n,)
        sem = ("arbitrary",)
        nb_shape, nb_spec = (n, 3 * DN_WIDTH), row(3 * DN_WIDTH)
        scratch = []
    else:
        b, t, _ = x.shape
        grid = (b, t // tm)
        row = lambda c: pl.BlockSpec((None, tm, c), lambda i, j: (i, j, 0))
        per_b = pl.BlockSpec((None, 1, d), lambda i, j: (i, 0, 0))
        in_specs = [row(d), per_b, per_b]
        lead = (b, t)
        sem = ("parallel", "arbitrary")
        nb_shape = (b, SUBLANES, 3 * DN_WIDTH)
        nb_spec = pl.BlockSpec((None, SUBLANES, 3 * DN_WIDTH), lambda i, j: (i, 0, 0))
        scratch = [pltpu.VMEM((SUBLANES, 3 * DN_WIDTH), F32)]
    in_specs += [_resident(gn.shape), _resident(w1.shape), _resident(gq.shape), _resident(gk.shape),
                 _resident(wc.shape), _resident(al.shape), _resident(dt.shape)]
    args = [x, sc, sh, gn, w1, gq, gk, wc, al, dt]
    if grouped:
        in_specs.append(row(3 * DN_WIDTH))
        args.append(bufp)
    out_cols = [(SB_WIDTH, BF16)] * 3 + [(SB_WIDTH, F32)] * 2 + [(DN_WIDTH, F32)] * 4 + [(LANES, F32)]
    out_shape = [jax.ShapeDtypeStruct(lead + (c,), dt_) for c, dt_ in out_cols]
    out_specs = [row(c) for c, _ in out_cols]
    out_shape.append(jax.ShapeDtypeStruct(nb_shape, F32))
    out_specs.append(nb_spec)
    return pl.pallas_call(
        functools.partial(_inproj_kernel, grouped=grouped),
        grid=grid, in_specs=in_specs, out_specs=out_specs, out_shape=out_shape,
        scratch_shapes=scratch,
        compiler_params=_cparams(sem),
        name="inproj_grouped" if grouped else "inproj_seq",
    )(*args)


def _sb_prompt_kernel(bias_ref, q_ref, k_ref, v_ref, g_ref, o_ref, acc_ref, run_ref, sp_ref, lb_ref,
                      *, tb, unit):
    p = pl.program_id(1)
    i = pl.program_id(2)
    lane = lax.broadcasted_iota(jnp.int32, (1, LANES), 1)
    lo = lane < SB_DIM
    q = q_ref[...]
    zero = jnp.zeros_like(q)
    q2 = jnp.concatenate([jnp.where(lo, q, zero), jnp.where(lo, zero, q)], axis=0)
    head_row = lax.broadcasted_iota(jnp.int32, (2 * tb, LANES), 0) < tb
    lane2 = lax.broadcasted_iota(jnp.int32, (2 * tb, LANES), 1)
    b_parts = [m.astype(F32) for m in _split3(jnp.where(head_row, bias_ref[2 * p], bias_ref[2 * p + 1]))]
    bias_lanes = jnp.where(lane2 == 0, b_parts[0],
                           jnp.where(lane2 == 1, b_parts[1], jnp.where(lane2 == 2, b_parts[2], 0.0)))
    q2x = jnp.concatenate([q2, bias_lanes.astype(BF16)], axis=1)
    ones_lanes = jnp.where(lax.broadcasted_iota(jnp.int32, (tb, LANES), 1) < 3, 1.0, 0.0).astype(BF16)

    def with_ones(ks):
        n = ks.shape[0] // tb
        ones = ones_lanes if n == 1 else jnp.concatenate([ones_lanes] * n, axis=0)
        return jnp.concatenate([ks, ones], axis=1)
    row = lax.broadcasted_iota(jnp.int32, (tb, tb), 0)
    col = lax.broadcasted_iota(jnp.int32, (tb, tb), 1)
    later_keys = jnp.where(row > col, 1.0, 0.0).astype(BF16)
    qpos = lax.broadcasted_iota(jnp.int32, (2 * tb, tb), 0) & (tb - 1)
    valid = lax.broadcasted_iota(jnp.int32, (2 * tb, tb), 1) < qpos

    acc_ref[...] = jnp.zeros_like(acc_ref)
    run_ref[...] = jnp.zeros_like(run_ref)

    def block(j, masked):
        start = pl.multiple_of(j * tb, tb)
        ks = k_ref[pl.ds(start, tb), :]
        vs = v_ref[pl.ds(start, tb), :]
        z = _dot_nt(q2x, with_ones(ks))
        sp = jnp.maximum(z, 0.0) + jnp.log(1.0 + _exp_neg_abs(z))
        lb = z - sp
        if masked:
            sp = jnp.where(valid, sp, 0.0)
        later = _dot(sp.astype(BF16), later_keys)
        run = run_ref[...]
        att = jnp.concatenate(
            [jnp.exp(lb[:, c * LANES:(c + 1) * LANES] - later[:, c * LANES:(c + 1) * LANES] - run)
             for c in range(tb // LANES)], axis=1)
        if masked:
            att = jnp.where(valid, att, 0.0)
        acc_ref[...] += _dot(att.astype(BF16), vs)
        run_ref[...] = run + (later[:, 0:1] + sp[:, 0:1])

    def stage1(jhi, slot):
        start = pl.multiple_of((jhi - (unit - 1)) * tb, tb)
        z = _dot_nt(q2x, with_ones(k_ref[pl.ds(start, unit * tb), :]))
        sp = jnp.maximum(z, 0.0) + jnp.log(1.0 + _exp_neg_abs(z))
        lb_ref[slot] = z - sp
        sp_ref[slot] = sp.astype(BF16)

    def stage2(jhi, slot):
        start = pl.multiple_of((jhi - (unit - 1)) * tb, tb)
        sp16 = sp_ref[slot]
        cols = [slice((unit - 1 - s) * tb, (unit - s) * tb) for s in range(unit)]
        later = _dot(jnp.concatenate([sp16[:, cs] for cs in cols], axis=0), later_keys)
        run = run_ref[...]
        att = [None] * unit
        for s, cs in enumerate(cols):
            lat = later[s * 2 * tb:(s + 1) * 2 * tb]
            att[unit - 1 - s] = jnp.concatenate(
                [jnp.exp(lb_ref[slot, :, cs.start + c * LANES:cs.start + (c + 1) * LANES]
                         - lat[:, c * LANES:(c + 1) * LANES] - run) for c in range(tb // LANES)], axis=1)
            run = run + (lat[:, 0:1] + sp16[:, cs.start:cs.start + 1].astype(F32))
        run_ref[...] = run
        acc_ref[...] += _dot(jnp.concatenate(att, axis=1).astype(BF16), v_ref[pl.ds(start, unit * tb), :])

    left = i % unit
    n_units = i // unit
    top = i - 1 - left
    lone = n_units & 1
    first = top - lone * unit
    n_pairs = n_units >> 1

    @pl.when(n_pairs >= 1)
    def _():
        stage1(first, 0)
        block(i, True)

    @pl.when(n_pairs < 1)
    def _():
        block(i, True)

    for r in range(unit - 1):
        @pl.when(left > r)
        def _(r=r):
            block(i - 1 - r, False)

    @pl.when(lone == 1)
    def _():
        stage1(top, 1)
        stage2(top, 1)

    @pl.when(n_pairs >= 1)
    def _():
        def body(u, carry):
            jhi = first - 2 * u * unit
            stage2(jhi, 0)
            stage1(jhi - unit, 1)
            stage2(jhi - unit, 1)
            stage1(jhi - 2 * unit, 0)
            return carry

        lax.fori_loop(0, n_pairs - 1, body, 0)
        jhi = first - 2 * (n_pairs - 1) * unit
        stage2(jhi, 0)
        stage1(jhi - unit, 1)
        stage2(jhi - unit, 1)

    o = jnp.where(lo, acc_ref[0:tb], acc_ref[tb:2 * tb])
    o_ref[...] = (_half_lane_rms(o, lo) * g_ref[...]).astype(o_ref.dtype)


def _sb_prompt(qb, kb, vb, bias, g2, *, tb, unit):
    b, t, w = qb.shape
    pairs = w // LANES
    return pl.pallas_call(
        functools.partial(_sb_prompt_kernel, tb=tb, unit=unit),
        grid_spec=pltpu.PrefetchScalarGridSpec(
            num_scalar_prefetch=0,
            grid=(b, pairs, t // tb),
            in_specs=[pl.BlockSpec(memory_space=pltpu.SMEM),
                      pl.BlockSpec((None, tb, LANES), lambda i, p, j: (i, j, p)),
                      pl.BlockSpec((None, t, LANES), lambda i, p, j: (i, 0, p)),
                      pl.BlockSpec((None, t, LANES), lambda i, p, j: (i, 0, p)),
                      pl.BlockSpec((1, LANES), lambda i, p, j: (0, 0))],
            out_specs=pl.BlockSpec((None, tb, LANES), lambda i, p, j: (i, j, p)),
            scratch_shapes=[pltpu.VMEM((2 * tb, LANES), F32), pltpu.VMEM((2 * tb, LANES), F32),
                            pltpu.VMEM((2, 2 * tb, unit * tb), BF16), pltpu.VMEM((2, 2 * tb, unit * tb), F32)]),
        out_shape=jax.ShapeDtypeStruct((b, t, w), BF16),
        compiler_params=_cparams(("parallel", "parallel", "arbitrary")),
        name="sb_attn_prompt",
    )(bias, qb, kb, vb, g2)


def _sb_sample_kernel(*refs, pps, n_q):
    pt_ref, bias_ref, q_ref, kn_ref, vn_ref, g_ref = refs[:6]
    kp_refs = refs[6:6 + pps]
    vp_refs = refs[6 + pps:6 + 2 * pps]
    o_ref = refs[6 + 2 * pps]
    qx_ref, acc_ref, run_ref, newk_ref, newv_ref = refs[7 + 2 * pps:]
    del pt_ref
    j = pl.program_id(1)
    rows = SB_HEADS * n_q
    page = kp_refs[0].shape[1]
    width = SB_WIDTH
    r1 = lax.broadcasted_iota(jnp.int32, (rows, 1), 0)
    later_keys = jnp.where(lax.broadcasted_iota(jnp.int32, (page, page), 0)
                           > lax.broadcasted_iota(jnp.int32, (page, page), 1), 1.0, 0.0).astype(BF16)

    def new_token_block(kf, vf):
        z = _dot_nt(qx_ref[...], kf.astype(BF16)) + bias_ref[...]
        sp, lb = _softplus_parts(z)
        valid = lax.broadcasted_iota(jnp.int32, (rows, page), 1) < (r1 & (n_q - 1))
        sp = jnp.where(valid, sp, 0.0)
        arg = lb - _dot(sp.astype(BF16), later_keys) - run_ref[...]
        att = jnp.where(valid, jnp.exp(arg), 0.0)
        acc_ref[...] += _dot(att.astype(BF16), vf.astype(BF16))
        run_ref[...] += jnp.sum(sp, axis=-1, keepdims=True)

    def cached_pages():
        k_all = jnp.concatenate([kp_refs[r][...].astype(BF16) for r in range(pps)], axis=1)
        z = _dot(qx_ref[...], k_all) + bias_ref[...]
        sp, lb = _softplus_parts(z)
        sp16 = sp.astype(BF16)
        later = _dot(jnp.concatenate([sp16[:, r * page:(r + 1) * page] for r in range(pps)], axis=0),
                     later_keys)
        run = run_ref[...]
        args = []
        for r in range(pps):
            cs = slice(r * page, (r + 1) * page)
            args.append(lb[:, cs] - later[r * rows:(r + 1) * rows] - run)
            run = run + jnp.sum(sp[:, cs], axis=-1, keepdims=True)
        run_ref[...] = run
        att = jnp.exp(jnp.concatenate(args, axis=1)).astype(BF16)
        v_all = jnp.concatenate([vp_refs[r][...].astype(BF16) for r in range(pps)], axis=1)
        acc_ref[...] += _dot_nt(att, v_all)

    @pl.when(j == 0)
    def _():
        head_of_lane = lax.broadcasted_iota(jnp.int32, (rows, width), 1) // SB_DIM
        head_of_row = lax.broadcasted_iota(jnp.int32, (rows, width), 0) // n_q
        qrep = jnp.concatenate([q_ref[...].astype(F32)] * SB_HEADS, axis=0)
        qx_ref[...] = jnp.where(head_of_lane == head_of_row, qrep, 0.0).astype(BF16)
        acc_ref[...] = jnp.zeros_like(acc_ref)
        run_ref[...] = jnp.zeros_like(run_ref)
        newk_ref[...] = jnp.zeros_like(newk_ref)
        newv_ref[...] = jnp.zeros_like(newv_ref)
        newk_ref[0:n_q, :] = kn_ref[...]
        newv_ref[0:n_q, :] = vn_ref[...]
        new_token_block(newk_ref[...], newv_ref[...])

    cached_pages()

    @pl.when(j == pl.num_programs(1) - 1)
    def _():
        res = acc_ref[...]
        lane_head = lax.broadcasted_iota(jnp.int32, (n_q, width), 1) // SB_DIM
        o = jnp.zeros((n_q, width), F32)
        for hh in range(SB_HEADS):
            o = o + jnp.where(lane_head == hh, res[hh * n_q:(hh + 1) * n_q, :], 0.0)
        lo = lax.broadcasted_iota(jnp.int32, (1, LANES), 1) < SB_DIM
        for c in range(width // LANES):
            sl = slice(c * LANES, (c + 1) * LANES)
            o_ref[:, sl] = _half_lane_rms(o[:, sl], lo) * g_ref[:, sl]


def _sb_sample(page_table, bias_rows, q3, kn3, vn3, g_sb, cache_k, cache_v):
    bs, n_q, w = q3.shape
    n_pages = page_table.shape[1]
    page = cache_k.shape[2]
    pps = PAGES_PER_STEP
    rows = SB_HEADS * n_q

    def page_spec(r):
        return pl.BlockSpec((None, w, page),
                            lambda i, j, pt: (pt[i, n_pages - 1 - (j * pps + r)], 0, 0))

    tok = lambda: pl.BlockSpec((None, n_q, w), lambda i, j, pt: (i, 0, 0))
    const = lambda shp: pl.BlockSpec(shp, lambda i, j, pt: (0,) * len(shp))
    return pl.pallas_call(
        functools.partial(_sb_sample_kernel, pps=pps, n_q=n_q),
        grid_spec=pltpu.PrefetchScalarGridSpec(
            num_scalar_prefetch=1,
            grid=(bs, n_pages // pps),
            in_specs=[const((rows, 1)), tok(), tok(), tok(), const((1, w))]
                     + [page_spec(r) for r in range(pps)] + [page_spec(r) for r in range(pps)],
            out_specs=tok(),
            scratch_shapes=[pltpu.VMEM((rows, w), BF16), pltpu.VMEM((rows, w), F32),
                            pltpu.VMEM((rows, 1), F32), pltpu.VMEM((page, w), F32),
                            pltpu.VMEM((page, w), F32)]),
        out_shape=jax.ShapeDtypeStruct((bs, n_q, w), F32),
        compiler_params=_cparams(("parallel", "arbitrary")),
        name="sb_attn_paged",
    )(page_table, bias_rows, q3, kn3, vn3, g_sb, *([cache_k] * pps), *([cache_v] * pps))


def _split3(x):
    hi = x.astype(BF16)
    r = x - hi.astype(F32)
    mid = r.astype(BF16)
    lo = (r - mid.astype(F32)).astype(BF16)
    return hi, mid, lo


def _dn_local(blocks, group):
    n = blocks[0][3].shape[0]
    units = [(bi, h) for bi in range(len(blocks)) for h in range(DN_HEADS)]
    hs = [slice(h * DN_DIM, (h + 1) * DN_DIM) for h in range(DN_HEADS)]
    row = lax.broadcasted_iota(jnp.int32, (n, n), 0)
    col = lax.broadcasted_iota(jnp.int32, (n, n), 1)
    shift = group.bit_length() - 1
    same = jnp.where((row >> shift) == (col >> shift), 1.0, 0.0)
    low = jnp.where(row >= col, same, 0.0)
    strict = jnp.where(row > col, same, 0.0)
    low16 = low.astype(BF16)
    same16 = same.astype(BF16)
    gparts = [_split3(blk[3]) for blk in blocks]
    g_all = [sum(_dot(low16, m) for m in gp) for gp in gparts]
    t_all = [sum(_dot(same16, m) for m in gp) for gp in gparts]
    g_row = [jnp.broadcast_to(g_all[bi][:, DN_HEADS + h:DN_HEADS + h + 1], (n, n)) for bi, h in units]
    gtot = [jnp.broadcast_to(t_all[bi][:, DN_HEADS + h:DN_HEADS + h + 1], (n, n)) for bi, h in units]
    un = range(len(units))
    decay = [low * jnp.exp(jnp.minimum(g_row[u] - g_row[u].T, 0.0)) for u in un]
    beta = [blocks[bi][3][:, h:h + 1] for bi, h in units]
    k = [blocks[bi][1][:, hs[h]] for bi, h in units]
    k16 = [k[u].astype(BF16) for u in un]
    kb = [k[u] * beta[u] for u in un]
    a_mat = [strict * (_dot_nt(kb[u].astype(BF16), k16[u]) * decay[u]) for u in un]
    eye = jnp.where(row == col, 1.0, 0.0)
    inv = [eye - a_mat[u] * jnp.where((row >> 1) == (col >> 1), 1.0, 0.0) for u in un]
    for lb in range(1, shift):
        sel = jnp.where((row >> (lb + 1)) == (col >> (lb + 1)),
                        jnp.where(((row >> lb) & 1) > ((col >> lb) & 1), 1.0, 0.0), 0.0)
        inv16 = [inv[u].astype(BF16) for u in un]
        x = [_dot((a_mat[u] * sel).astype(BF16), inv16[u]) for u in un]
        inv = [inv[u] - _dot(inv16[u], x[u].astype(BF16)) for u in un]
    eg = [jnp.exp(g_row[u]) for u in un]
    v = [blocks[bi][2][:, hs[h]] for bi, h in units]
    rhs = [jnp.concatenate([v[u] * beta[u], kb[u] * eg[u]], axis=-1).astype(BF16) for u in un]
    sol = [_dot(inv[u].astype(BF16), rhs[u]) for u in un]
    q = [blocks[bi][0][:, hs[h]] for bi, h in units]
    intra = [_dot_nt(q[u].astype(BF16), k16[u]) * decay[u] for u in un]
    usol = [sol[u][:, :DN_DIM] for u in un]
    kcum = [sol[u][:, DN_DIM:] for u in un]
    qdec = [q[u] * eg[u] for u in un]
    kdec = [k[u] * jnp.exp(gtot[u] - g_row[u]) for u in un]
    return usol, kcum, intra, qdec, kdec, gtot


def _dn_seq_kernel(q_ref, k_ref, v_ref, gate_ref, o_ref, sout_ref, s_ref, *, group):
    @pl.when(pl.program_id(0) == 0)
    def _():
        s_ref[...] = jnp.zeros_like(s_ref)

    nb, n = q_ref.shape[0], q_ref.shape[1]
    blocks = [(q_ref.at[bi], k_ref.at[bi], v_ref.at[bi], gate_ref[bi]) for bi in range(nb)]
    usol, kcum, intra, qdec, kdec, gtot = _dn_local(blocks, group)
    units = [(bi, h) for bi in range(nb) for h in range(DN_HEADS)]
    un = range(len(units))
    s = [s_ref[bi, h] for bi, h in units]
    vnews = [[] for _ in un]
    outs = [[] for _ in un]
    for c in range(n // group):
        rs = slice(c * group, (c + 1) * group)
        s16 = [s[u].astype(BF16) for u in un]
        lhs = [jnp.concatenate([kcum[u][rs], qdec[u][rs]], axis=0).astype(BF16) for u in un]
        both = [_dot(lhs[u], s16[u]) for u in un]
        vnew = [usol[u][rs] - both[u][:group] for u in un]
        upd = [_dot_tn(kdec[u][rs].astype(BF16), vnew[u].astype(BF16)) for u in un]
        s = [s[u] * jnp.exp(gtot[u][c * group:c * group + 1, :]) + upd[u] for u in un]
        for u in un:
            vnews[u].append(vnew[u])
            outs[u].append(both[u][group:])
    for u, (bi, h) in enumerate(units):
        s_ref[bi, h] = s[u]
        vfull = jnp.concatenate(vnews[u], axis=0).astype(BF16)
        o_ref[bi, :, h * DN_DIM:(h + 1) * DN_DIM] = (jnp.concatenate(outs[u], axis=0)
                                                     + _dot(intra[u].astype(BF16), vfull))

    @pl.when(pl.program_id(0) == pl.num_programs(0) - 1)
    def _():
        sout_ref[...] = s_ref[...]


def _dn_seq(qd, kd, vd, gates):
    b, t, w = qd.shape
    n = DN_BLOCK
    row = lambda c: pl.BlockSpec((b, n, c), lambda j: (0, j, 0))
    state = (b, DN_HEADS, DN_DIM, DN_DIM)
    return pl.pallas_call(
        functools.partial(_dn_seq_kernel, group=min(64, n)),
        grid=(t // n,),
        in_specs=[row(w), row(w), row(w), row(LANES)],
        out_specs=[row(w), pl.BlockSpec(state, lambda j: (0, 0, 0, 0))],
        out_shape=[jax.ShapeDtypeStruct((b, t, w), F32), jax.ShapeDtypeStruct(state, F32)],
        scratch_shapes=[pltpu.VMEM(state, F32)],
        compiler_params=_cparams(("arbitrary",)),
        name="deltanet_seq",
    )(qd, kd, vd, gates)


def _dn_grouped_kernel(q_ref, k_ref, v_ref, gate_ref, s0_ref, o_ref, sout_ref,
                       u_ref, kc_ref, qd_ref, kd_ref, gt_ref, vn_ref, oq_ref, *, group):
    n = q_ref.shape[0]
    heads = range(DN_HEADS)
    row1 = lax.broadcasted_iota(jnp.int32, (n, 1), 0)
    shift = group.bit_length() - 1
    u, kcum, intra, qdec, kdec, gtot = _dn_local([(q_ref, k_ref, v_ref, gate_ref[...])], group)
    for h in heads:
        u_ref[h] = u[h]
        kc_ref[h] = kcum[h]
        qd_ref[h] = qdec[h]
        kd_ref[h] = kdec[h]
        gt_ref[h] = gtot[h]
    vn_ref[...] = jnp.zeros_like(vn_ref)

    def per_seq(bi, carry):
        rs = pl.ds(pl.multiple_of(bi * group, group), group)
        mine = (row1 >> shift) == bi
        for h in heads:
            s = s0_ref[bi, h]
            vnew = u_ref[h, rs, :] - _dot(kc_ref[h, rs, :], s)
            oq_ref[h, rs, :] = _dot(qd_ref[h, rs, :], s)
            vn_ref[h, rs, :] = vnew
            kmask = jnp.where(mine, kd_ref[h], 0.0).astype(BF16)
            sout_ref[bi, h] = (s * jnp.exp(gt_ref[h, rs, :][0:1, :])
                               + _dot_tn(kmask, vn_ref[h].astype(BF16)))
        return carry

    lax.fori_loop(0, n // group, per_seq, 0)
    for h in heads:
        o_ref[:, h * DN_DIM:(h + 1) * DN_DIM] = (oq_ref[h]
                                                 + _dot(intra[h].astype(BF16), vn_ref[h].astype(BF16)))


def _dn_grouped(qd, kd, vd, gates, s0, *, group):
    nrows, w = qd.shape
    n = DN_BLOCK
    per = n // group
    row = lambda c: pl.BlockSpec((n, c), lambda i: (i, 0))
    st = pl.BlockSpec((per, DN_HEADS, DN_DIM, DN_DIM), lambda i: (i, 0, 0, 0))
    return pl.pallas_call(
        functools.partial(_dn_grouped_kernel, group=group),
        grid=(nrows // n,),
        in_specs=[row(w), row(w), row(w), row(LANES), st],
        out_specs=[row(w), st],
        out_shape=[jax.ShapeDtypeStruct((nrows, w), F32), jax.ShapeDtypeStruct(s0.shape, F32)],
        scratch_shapes=[pltpu.VMEM((DN_HEADS, n, DN_DIM), F32)] * 7,
        compiler_params=_cparams(("parallel",)),
        name="deltanet_grouped",
    )(qd, kd, vd, gates, s0)


def _outproj_kernel(osb_ref, odn_ref, zg_ref, gdn_ref, x_ref, gt_ref, sc_ref, sh_ref, gn_ref, w_ref,
                    x1_ref, h2_ref):
    parts = [osb_ref[...].astype(BF16)]
    for hh in range(DN_HEADS):
        sl = slice(hh * DN_DIM, (hh + 1) * DN_DIM)
        parts.append((_rms_rows(odn_ref[:, sl], gdn_ref[...]) * zg_ref[:, sl]).astype(BF16))
    mix = _dot(jnp.concatenate(parts, axis=1), w_ref[...])
    x1 = x_ref[...] + gt_ref[...] * mix
    x1_ref[...] = x1
    h2_ref[...] = (_rms_rows(x1, gn_ref[...]) * (1.0 + sc_ref[...]) + sh_ref[...]).astype(BF16)


def _mod_specs(x, tm, per_row):
    d = x.shape[-1]
    if per_row:
        n = x.shape[0]
        row = lambda c: pl.BlockSpec((tm, c), lambda i: (i, 0))
        return (n // tm,), row, row(d), (n,), ("arbitrary",)
    b, t, _ = x.shape
    row = lambda c: pl.BlockSpec((None, tm, c), lambda i, j: (i, j, 0))
    per_b = pl.BlockSpec((None, 1, d), lambda i, j: (i, 0, 0))
    return (b, t // tm), row, per_b, (b, t), ("parallel", "arbitrary")


def _outproj(osb, odn, zg, gdn, x, gt, sc, sh, gn, w, *, tm, per_row):
    d = x.shape[-1]
    grid, row, mod, lead, sem = _mod_specs(x, tm, per_row)
    return pl.pallas_call(
        _outproj_kernel,
        grid=grid,
        in_specs=[row(SB_WIDTH), row(DN_WIDTH), row(DN_WIDTH), _resident(gdn.shape), row(d), mod, mod, mod,
                  _resident(gn.shape), _resident(w.shape)],
        out_specs=[row(d), row(d)],
        out_shape=[jax.ShapeDtypeStruct(lead + (d,), F32), jax.ShapeDtypeStruct(lead + (d,), BF16)],
        compiler_params=_cparams(sem),
        name="outproj_rows" if per_row else "outproj_seq",
    )(osb, odn, zg, gdn, x, gt, sc, sh, gn, w)


def _ffn_kernel(*refs, grouped, cw):
    if grouped:
        h_ref, x1_ref, gt_ref, wu_ref, wc_ref, wd_ref, buf_ref, y_ref, nbuf_ref, act_ref = refs
        carry_ref = None
    else:
        h_ref, x1_ref, gt_ref, wu_ref, wc_ref, wd_ref, y_ref, nbuf_ref, act_ref, carry_ref = refs
        buf_ref = None

        @pl.when(pl.program_id(1) == 0)
        def _():
            carry_ref[...] = jnp.zeros_like(carry_ref)

    hb = h_ref[...]
    tm = hb.shape[0]
    dff = wd_ref.shape[0]
    taps = wc_ref.shape[0]
    for c in range(dff // cw):
        halves = []
        for base in (0, dff):
            cs = slice(base + c * cw, base + (c + 1) * cw)
            cur = _dot(hb, wu_ref[:, cs])
            if grouped:
                y = _causal_conv(cur, wc_ref[:, cs], None, True, buf_ref[:, cs])
                nbuf_ref[:, cs] = pltpu.roll(cur, tm - (SUBLANES - (taps - 1)), 0)
            else:
                y = _causal_conv(cur, wc_ref[:, cs], carry_ref[:, cs], False, None)
                carry_ref[:, cs] = cur[tm - SUBLANES:tm]
                nbuf_ref[:, cs] = cur[tm - SUBLANES:tm]
            halves.append(y)
        act_ref[:, c * cw:(c + 1) * cw] = (_silu(halves[1]) * halves[0]).astype(BF16)
    y_ref[...] = x1_ref[...] + gt_ref[...] * _dot(act_ref[...], wd_ref[...])


def _ffn(h2, x1, gt, wu, wc, wd, bufp, *, tm, per_row, cw=256):
    d = x1.shape[-1]
    up = wu.shape[1]
    grid, row, mod, lead, sem = _mod_specs(x1, tm, per_row)
    in_specs = [row(d), row(d), mod, _resident(wu.shape), _resident(wc.shape), _resident(wd.shape)]
    args = [h2, x1, gt, wu, wc, wd]
    if per_row:
        in_specs.append(row(up))
        args.append(bufp)
        nb_shape, nb_spec, scratch = lead + (up,), row(up), [pltpu.VMEM((tm, up // 2), BF16)]
    else:
        nb_shape = (lead[0], SUBLANES, up)
        nb_spec = pl.BlockSpec((None, SUBLANES, up), lambda i, j: (i, 0, 0))
        scratch = [pltpu.VMEM((tm, up // 2), BF16), pltpu.VMEM((SUBLANES, up), F32)]
    return pl.pallas_call(
        functools.partial(_ffn_kernel, grouped=per_row, cw=cw),
        grid=grid, in_specs=in_specs,
        out_specs=[row(d), nb_spec],
        out_shape=[jax.ShapeDtypeStruct(lead + (d,), F32), jax.ShapeDtypeStruct(nb_shape, F32)],
        scratch_shapes=scratch,
        compiler_params=_cparams(sem),
        name="convffn_rows" if per_row else "convffn_seq",
    )(*args)


def _prep_weights(g_attn_norm, w_in, g_q, g_k, sb_bias, g_sb_out, w_dn_conv, a_log, dt_bias, g_dn_out,
                  w_out, g_ffn_norm, w_up, w_ffn_conv, w_down):
    d = w_in.shape[0]
    o1 = 3 * SB_WIDTH
    o2 = o1 + 3 * DN_WIDTH
    o3 = o2 + 2 * DN_HEADS
    pad = jnp.zeros((d, LANES - 2 * DN_HEADS), w_in.dtype)
    w1 = jnp.concatenate([w_in[:, :o2], w_in[:, o3:], w_in[:, o2:o3], pad], axis=1).astype(BF16)
    lane_pad = lambda v: jnp.zeros((1, LANES), F32).at[0, DN_HEADS:2 * DN_HEADS].set(v)
    return dict(
        gn1=g_attn_norm.reshape(1, d), w1=w1,
        gq=jnp.tile(g_q, SB_HEADS).reshape(1, SB_WIDTH), gk=jnp.tile(g_k, SB_HEADS).reshape(1, SB_WIDTH),
        bias=sb_bias.astype(F32), g_sb=jnp.tile(g_sb_out, SB_HEADS).reshape(1, SB_WIDTH),
        wc_dn=w_dn_conv, al=lane_pad(a_log), dt=lane_pad(dt_bias), gdn=g_dn_out.reshape(1, DN_DIM),
        w_out=w_out.astype(BF16), gn2=g_ffn_norm.reshape(1, d), w_up=w_up.astype(BF16),
        wc_ffn=w_ffn_conv, w_down=w_down.astype(BF16))


def _layer_prompt(x, mod, p):
    b, t, d = x.shape
    sh1, sc1, gt1, sh2, sc2, gt2 = [m.reshape(b, 1, d) for m in jnp.split(mod, 6, axis=-1)]
    qb, kb, vb, kf, vf, qd, kd, vd, zg, gates, dnbuf = _inproj(
        x, sc1, sh1, p["gn1"], p["w1"], p["gq"], p["gk"], p["wc_dn"], p["al"], p["dt"], None, tm=512)
    osb = _sb_prompt(qb, kb, vb, p["bias"], p["g_sb"][:, :LANES], tb=256, unit=1)
    odn, s_new = _dn_seq(qd, kd, vd, gates)
    x1, h2 = _outproj(osb, odn, zg, p["gdn"], x, gt1, sc2, sh2, p["gn2"], p["w_out"], tm=512, per_row=False)
    y, ffbuf = _ffn(h2, x1, gt2, p["w_up"], p["wc_ffn"], p["w_down"], None, tm=512, per_row=False)
    k_dn = p["wc_dn"].shape[0] - 1
    k_ff = p["wc_ffn"].shape[0] - 1
    return (y, kf.reshape(b, t, SB_HEADS, SB_DIM), vf.reshape(b, t, SB_HEADS, SB_DIM), s_new,
            dnbuf[:, SUBLANES - k_dn:], ffbuf[:, SUBLANES - k_ff:])


def _layer_sample(x, mod, page_table, cache_k, cache_v, s0, dn_buf, ffn_buf, p):
    bs, n_q, d = x.shape
    assert n_q == SUBLANES
    n = bs * n_q
    rep = lambda m: jnp.broadcast_to(m[:, None, :], (bs, n_q, d)).reshape(n, d)
    sh1, sc1, gt1, sh2, sc2, gt2 = [rep(m) for m in jnp.split(mod, 6, axis=-1)]
    padrows = lambda buf: jnp.pad(buf, ((0, 0), (0, n_q - buf.shape[1]), (0, 0))).reshape(n, buf.shape[2])
    xf = x.reshape(n, d)
    qb, kb, vb, kf, vf, qd, kd, vd, zg, gates, dnbuf = _inproj(
        xf, sc1, sh1, p["gn1"], p["w1"], p["gq"], p["gk"], p["wc_dn"], p["al"], p["dt"], padrows(dn_buf), tm=128)
    del kb, vb
    w = SB_WIDTH
    bias_rows = jnp.repeat(p["bias"], n_q).reshape(SB_HEADS * n_q, 1)
    n_phys, page = cache_k.shape[0], cache_k.shape[1]
    pool_t = lambda c: jnp.transpose(c, (0, 2, 3, 1)).reshape(n_phys, w, page)
    osb = _sb_sample(page_table, bias_rows, qb.reshape(bs, n_q, w), kf.reshape(bs, n_q, w),
                     vf.reshape(bs, n_q, w), p["g_sb"], pool_t(cache_k), pool_t(cache_v))
    odn, s_new = _dn_grouped(qd, kd, vd, gates, s0, group=n_q)
    x1, h2 = _outproj(osb.reshape(n, w), odn, zg, p["gdn"], xf, gt1, sc2, sh2, p["gn2"], p["w_out"],
                      tm=128, per_row=True)
    y, ffbuf = _ffn(h2, x1, gt2, p["w_up"], p["wc_ffn"], p["w_down"], padrows(ffn_buf), tm=128, per_row=True)
    k_dn = p["wc_dn"].shape[0] - 1
    k_ff = p["wc_ffn"].shape[0] - 1
    return (y.reshape(bs, n_q, d), kf.reshape(bs, n_q, SB_HEADS, SB_DIM), vf.reshape(bs, n_q, SB_HEADS, SB_DIM),
            s_new, dnbuf.reshape(bs, n_q, -1)[:, :k_dn], ffbuf.reshape(bs, n_q, -1)[:, :k_ff])


def kernel(x_prompt, x_sample, c_prompt, c_sample, cache_k, cache_v, page_table, state_delta, state_dn_conv, state_ffn_conv, w_ada, b_ada, g_attn_norm, w_in, g_q, g_k, sb_bias, g_sb_out, w_dn_conv, a_log, dt_bias, g_dn_out, w_out, g_ffn_norm, w_up, w_ffn_conv, w_down):
    depth = w_ada.shape[0]
    bp = x_prompt.shape[0]
    yp, ys = x_prompt, x_sample
    outs = [[] for _ in range(10)]
    c_all = jnp.concatenate([c_prompt, c_sample], axis=0)
    pad_rows = (-c_all.shape[0]) % SUBLANES
    c_all = jnp.pad(c_all, ((0, pad_rows), (0, 0)))
    for l in range(depth):
        p = _prep_weights(g_attn_norm[l], w_in[l], g_q[l], g_k[l], sb_bias[l], g_sb_out[l], w_dn_conv[l],
                          a_log[l], dt_bias[l], g_dn_out[l], w_out[l], g_ffn_norm[l], w_up[l],
                          w_ffn_conv[l], w_down[l])
        mod = _ada(c_all, w_ada[l], b_ada[l])
        yp, kp, vp, sp, dcp, fcp = _layer_prompt(yp, mod[:bp], p)
        ys, ks, vs, ss, dcs, fcs = _layer_sample(ys, mod[bp:bp + x_sample.shape[0]], page_table,
                                                 cache_k[l], cache_v[l], state_delta[l],
                                                 state_dn_conv[l], state_ffn_conv[l], p)
        for lst, val in zip(outs, (kp, vp, ks, vs, sp, ss, dcp, dcs, fcp, fcs)):
            lst.append(val)
    return (yp, ys) + tuple(jnp.stack(o) for o in outs)
```

```python
import functools

import jax
import jax.numpy as jnp
from jax import lax
from jax.experimental import pallas as pl
from jax.experimental.pallas import tpu as pltpu

F32 = jnp.float32
BF16 = jnp.bfloat16

NORM_EPS = 1e-6
LOG2E = 1.4426950408889634
LANES = 128
SUBLANES = 8
VMEM_LIMIT = 56 * 1024 * 1024

SB_HEADS = 8
SB_DIM = 64
SB_WIDTH = SB_HEADS * SB_DIM
DN_HEADS = 4
DN_DIM = 128
DN_WIDTH = DN_HEADS * DN_DIM
DN_BLOCK = 128


def _dot(a, b):
    return jnp.dot(a, b, preferred_element_type=F32)


def _dot_nt(a, b):
    return lax.dot_general(a, b, (((1,), (1,)), ((), ())), preferred_element_type=F32)


def _dot_tn(a, b):
    return lax.dot_general(a, b, (((0,), (0,)), ((), ())), preferred_element_type=F32)


def _silu(x):
    return x * jax.nn.sigmoid(x)


def _exp_neg_abs(x):
    return jnp.exp2(jnp.abs(x) * (-LOG2E))


def _softplus_parts(z):
    l = jnp.log(1.0 + jnp.exp(-jnp.abs(z)))
    return jnp.maximum(z, 0.0) + l, jnp.minimum(z, 0.0) - l


def _rms_rows(x, g):
    ms = jnp.mean(x * x, axis=-1, keepdims=True)
    return x * lax.rsqrt(ms + NORM_EPS) * g


def _half_lane_rms(blk, lo):
    sq = blk * blk
    s_lo = jnp.sum(jnp.where(lo, sq, 0.0), axis=-1, keepdims=True)
    s_hi = jnp.sum(jnp.where(lo, 0.0, sq), axis=-1, keepdims=True)
    ms = jnp.where(lo, s_lo, s_hi) * (1.0 / SB_DIM)
    return blk * lax.rsqrt(ms + NORM_EPS)


def _cparams(sem):
    return pltpu.CompilerParams(dimension_semantics=sem, vmem_limit_bytes=VMEM_LIMIT)


def _resident(shape):
    nd = len(shape)
    return pl.BlockSpec(shape, lambda *_: (0,) * nd, pipeline_mode=pl.Buffered(1))


def _ada_kernel(c_ref, w_ref, b_ref, o_ref):
    a = _silu(c_ref[...]).astype(BF16)
    o_ref[...] = _dot(a, w_ref[...].astype(BF16)) + b_ref[...]


def _ada(c, w, b):
    m, d = c.shape
    n = w.shape[1]
    tn = 512
    return pl.pallas_call(
        _ada_kernel,
        grid=(n // tn,),
        in_specs=[pl.BlockSpec((m, d), lambda j: (0, 0)),
                  pl.BlockSpec((d, tn), lambda j: (0, j)),
                  pl.BlockSpec((1, tn), lambda j: (0, j))],
        out_specs=pl.BlockSpec((m, tn), lambda j: (0, j)),
        out_shape=jax.ShapeDtypeStruct((m, n), F32),
        compiler_params=_cparams(("parallel",)),
        name="adaln_mod",
    )(c, w, b.reshape(1, n))


C_SB = 0
C_DN = 3 * SB_WIDTH
C_Z = C_DN + 3 * DN_WIDTH
C_G = C_Z + DN_WIDTH
C_END = C_G + LANES


def _causal_conv(cur, taps, prev_rows, grouped, bufp):
    w = taps.shape[0]
    tm = cur.shape[0]
    y = taps[w - 1:w, :] * cur
    if grouped:
        rowm = lax.broadcasted_iota(jnp.int32, (tm, 1), 0) & (SUBLANES - 1)
        for s in range(1, w):
            r = pltpu.roll(cur, s, 0)
            back = (w - 1) - s
            bs = bufp if back == 0 else pltpu.roll(bufp, tm - back, 0)
            y = y + taps[w - 1 - s:w - s, :] * jnp.where(rowm >= s, r, bs)
        return y
    row8 = lax.broadcasted_iota(jnp.int32, (SUBLANES, 1), 0)
    yh = taps[w - 1:w, :] * cur[0:SUBLANES]
    for s in range(1, w):
        r = pltpu.roll(cur, s, 0)
        y = y + taps[w - 1 - s:w - s, :] * r
        head = jnp.where(row8 < s, pltpu.roll(prev_rows, s, 0), r[0:SUBLANES])
        yh = yh + taps[w - 1 - s:w - s, :] * head
    return jnp.concatenate([yh, y[SUBLANES:]], axis=0)


def _inproj_kernel(*refs, grouped):
    if grouped:
        (x_ref, sc_ref, sh_ref, gn_ref, w_ref, gq_ref, gk_ref, wc_ref, al_ref, dt_ref, buf_ref,
         qb_ref, kb_ref, vb_ref, kf_ref, vf_ref, qd_ref, kd_ref, vd_ref, zg_ref, gate_ref, nbuf_ref) = refs
        carry_ref = None
    else:
        (x_ref, sc_ref, sh_ref, gn_ref, w_ref, gq_ref, gk_ref, wc_ref, al_ref, dt_ref,
         qb_ref, kb_ref, vb_ref, kf_ref, vf_ref, qd_ref, kd_ref, vd_ref, zg_ref, gate_ref, nbuf_ref,
         carry_ref) = refs
        buf_ref = None

        @pl.when(pl.program_id(1) == 0)
        def _():
            carry_ref[...] = jnp.zeros_like(carry_ref)

    x = x_ref[...]
    h = _rms_rows(x, gn_ref[...]) * (1.0 + sc_ref[...]) + sh_ref[...]
    hb = h.astype(BF16)
    tm = x.shape[0]
    lane = lax.broadcasted_iota(jnp.int32, (1, LANES), 1)
    lo = lane < SB_DIM

    for c in range(SB_WIDTH // LANES):
        sl = slice(c * LANES, (c + 1) * LANES)
        pq = _dot(hb, w_ref[:, C_SB + c * LANES:C_SB + (c + 1) * LANES])
        qn = _half_lane_rms(pq, lo) * gq_ref[:, sl]
        qb_ref[:, sl] = (qn * (SB_DIM ** -0.5)).astype(BF16)
        pk = _dot(hb, w_ref[:, C_SB + SB_WIDTH + c * LANES:C_SB + SB_WIDTH + (c + 1) * LANES])
        kn = _half_lane_rms(pk, lo) * gk_ref[:, sl]
        kf_ref[:, sl] = kn
        kb_ref[:, sl] = kn.astype(BF16)
    pv = _dot(hb, w_ref[:, C_SB + 2 * SB_WIDTH:C_SB + 3 * SB_WIDTH])
    vf_ref[...] = pv
    vb_ref[...] = pv.astype(BF16)

    cur = _dot(hb, w_ref[:, C_DN:C_Z])
    if grouped:
        y = _causal_conv(cur, wc_ref[...], None, True, buf_ref[...])
        nbuf_ref[...] = pltpu.roll(cur, tm - (SUBLANES - (wc_ref.shape[0] - 1)), 0)
    else:
        y = _causal_conv(cur, wc_ref[...], carry_ref[...], False, None)
        carry_ref[...] = cur[tm - SUBLANES:tm]
        nbuf_ref[...] = cur[tm - SUBLANES:tm]
    a = _silu(y)
    for hh in range(DN_HEADS):
        sl = slice(hh * DN_DIM, (hh + 1) * DN_DIM)
        qh = a[:, hh * DN_DIM:(hh + 1) * DN_DIM]
        kh = a[:, DN_WIDTH + hh * DN_DIM:DN_WIDTH + (hh + 1) * DN_DIM]
        qd_ref[:, sl] = qh * (lax.rsqrt(jnp.sum(qh * qh, axis=-1, keepdims=True) + NORM_EPS) * (DN_DIM ** -0.5))
        kd_ref[:, sl] = kh * lax.rsqrt(jnp.sum(kh * kh, axis=-1, keepdims=True) + NORM_EPS)
    vd_ref[...] = a[:, 2 * DN_WIDTH:3 * DN_WIDTH]

    zg_ref[...] = _silu(_dot(hb, w_ref[:, C_Z:C_G]))

    gb = _dot(hb, w_ref[:, C_G:C_END])
    sp, _ = _softplus_parts(gb + dt_ref[...])
    gate_ref[...] = jnp.where(lane < DN_HEADS, jax.nn.sigmoid(gb), -jnp.exp(al_ref[...]) * sp)


def _inproj(x, sc, sh, gn, w1, gq, gk, wc, al, dt, bufp, *, tm):
    grouped = bufp is not None
    d = x.shape[-1]
    if grouped:
        n = x.shape[0]
        grid = (n // tm,)
        row = lambda c: pl.BlockSpec((tm, c), lambda i: (i, 0))
        in_specs = [row(d), row(d), row(d)]
        lead = (n,)
        sem = ("arbitrary",)
        nb_shape, nb_spec = (n, 3 * DN_WIDTH), row(3 * DN_WIDTH)
        scratch = []
    else:
        b, t, _ = x.shape
        grid = (b, t // tm)
        row = lambda c: pl.BlockSpec((None, tm, c), lambda i, j: (i, j, 0))
        per_b = pl.BlockSpec((None, 1, d), lambda i, j: (i, 0, 0))
        in_specs = [row(d), per_b, per_b]
        lead = (b, t)
        sem = ("parallel", "arbitrary")
        nb_shape = (b, SUBLANES, 3 * DN_WIDTH)
        nb_spec = pl.BlockSpec((None, SUBLANES, 3 * DN_WIDTH), lambda i, j: (i, 0, 0))
        scratch = [pltpu.VMEM((SUBLANES, 3 * DN_WIDTH), F32)]
    in_specs += [_resident(gn.shape), _resident(w1.shape), _resident(gq.shape), _resident(gk.shape),
                 _resident(wc.shape), _resident(al.shape), _resident(dt.shape)]
    args = [x, sc, sh, gn, w1, gq, gk, wc, al, dt]
    if grouped:
        in_specs.append(row(3 * DN_WIDTH))
        args.append(bufp)
    out_cols = [(SB_WIDTH, BF16)] * 3 + [(SB_WIDTH, F32)] * 2 + [(DN_WIDTH, F32)] * 4 + [(LANES, F32)]
    out_shape = [jax.ShapeDtypeStruct(lead + (c,), dt_) for c, dt_ in out_cols]
    out_specs = [row(c) for c, _ in out_cols]
    out_shape.append(jax.ShapeDtypeStruct(nb_shape, F32))
    out_specs.append(nb_spec)
    return pl.pallas_call(
        functools.partial(_inproj_kernel, grouped=grouped),
        grid=grid, in_specs=in_specs, out_specs=out_specs, out_shape=out_shape,
        scratch_shapes=scratch,
        compiler_params=_cparams(sem),
        name="inproj_grouped" if grouped else "inproj_seq",
    )(*args)


def _sb_prompt_kernel(bias_ref, q_ref, k_ref, v_ref, g_ref, o_ref, acc_ref, run_ref, sp_ref, lb_ref,
                      *, tb, unit):
    p = pl.program_id(1)
    i = pl.program_id(2)
    lane = lax.broadcasted_iota(jnp.int32, (1, LANES), 1)
    lo = lane < SB_DIM
    q = q_ref[...]
    zero = jnp.zeros_like(q)
    q2 = jnp.concatenate([jnp.where(lo, q, zero), jnp.where(lo, zero, q)], axis=0)
    head_row = lax.broadcasted_iota(jnp.int32, (2 * tb, LANES), 0) < tb
    lane2 = lax.broadcasted_iota(jnp.int32, (2 * tb, LANES), 1)
    b_parts = [m.astype(F32) for m in _split3(jnp.where(head_row, bias_ref[2 * p], bias_ref[2 * p + 1]))]
    bias_lanes = jnp.where(lane2 == 0, b_parts[0],
                           jnp.where(lane2 == 1, b_parts[1], jnp.where(lane2 == 2, b_parts[2], 0.0)))
    q2x = jnp.concatenate([q2, bias_lanes.astype(BF16)], axis=1)
    ones_lanes = jnp.where(lax.broadcasted_iota(jnp.int32, (tb, LANES), 1) < 3, 1.0, 0.0).astype(BF16)

    def with_ones(ks):
        n = ks.shape[0] // tb
        ones = ones_lanes if n == 1 else jnp.concatenate([ones_lanes] * n, axis=0)
        return jnp.concatenate([ks, ones], axis=1)
    row = lax.broadcasted_iota(jnp.int32, (tb, tb), 0)
    col = lax.broadcasted_iota(jnp.int32, (tb, tb), 1)
    later_keys = jnp.where(row > col, 1.0, 0.0).astype(BF16)
    qpos = lax.broadcasted_iota(jnp.int32, (2 * tb, tb), 0) & (tb - 1)
    valid = lax.broadcasted_iota(jnp.int32, (2 * tb, tb), 1) < qpos

    acc_ref[...] = jnp.zeros_like(acc_ref)
    run_ref[...] = jnp.zeros_like(run_ref)

    def block(j, masked):
        start = pl.multiple_of(j * tb, tb)
        ks = k_ref[pl.ds(start, tb), :]
        vs = v_ref[pl.ds(start, tb), :]
        z = _dot_nt(q2x, with_ones(ks))
        sp = jnp.maximum(z, 0.0) + jnp.log(1.0 + _exp_neg_abs(z))
        lb = z - sp
        if masked:
            sp = jnp.where(valid, sp, 0.0)
        later = _dot(sp.astype(BF16), later_keys)
        run = run_ref[...]
        att = jnp.concatenate(
            [jnp.exp(lb[:, c * LANES:(c + 1) * LANES] - later[:, c * LANES:(c + 1) * LANES] - run)
             for c in range(tb // LANES)], axis=1)
        if masked:
            att = jnp.where(valid, att, 0.0)
        acc_ref[...] += _dot(att.astype(BF16), vs)
        run_ref[...] = run + (later[:, 0:1] + sp[:, 0:1])

    def stage1(jhi, slot):
        start = pl.multiple_of((jhi - (unit - 1)) * tb, tb)
        z = _dot_nt(q2x, with_ones(k_ref[pl.ds(start, unit * tb), :]))
        sp = jnp.maximum(z, 0.0) + jnp.log(1.0 + _exp_neg_abs(z))
        lb_ref[slot] = z - sp
        sp_ref[slot] = sp.astype(BF16)

    def stage2(jhi, slot):
        start = pl.multiple_of((jhi - (unit - 1)) * tb, tb)
        sp16 = sp_ref[slot]
        cols = [slice((unit - 1 - s) * tb, (unit - s) * tb) for s in range(unit)]
        later = _dot(jnp.concatenate([sp16[:, cs] for cs in cols], axis=0), later_keys)
        run = run_ref[...]
        att = [None] * unit
        for s, cs in enumerate(cols):
            lat = later[s * 2 * tb:(s + 1) * 2 * tb]
            att[unit - 1 - s] = jnp.concatenate(
                [jnp.exp(lb_ref[slot, :, cs.start + c * LANES:cs.start + (c + 1) * LANES]
                         - lat[:, c * LANES:(c + 1) * LANES] - run) for c in range(tb // LANES)], axis=1)
            run = run + (lat[:, 0:1] + sp16[:, cs.start:cs.start + 1].astype(F32))
        run_ref[...] = run
        acc_ref[...] += _dot(jnp.concatenate(att, axis=1).astype(BF16), v_ref[pl.ds(start, unit * tb), :])

    left = i % unit
    n_units = i // unit
    top = i - 1 - left
    lone = n_units & 1
    first = top - lone * unit
    n_pairs = n_units >> 1

    @pl.when(n_pairs >= 1)
    def _():
        stage1(first, 0)
        block(i, True)

    @pl.when(n_pairs < 1)
    def _():
        block(i, True)

    for r in range(unit - 1):
        @pl.when(left > r)
        def _(r=r):
            block(i - 1 - r, False)

    @pl.when(lone == 1)
    def _():
        stage1(top, 1)
        stage2(top, 1)

    @pl.when(n_pairs >= 1)
    def _():
        def body(u, carry):
            jhi = first - 2 * u * unit
            stage2(jhi, 0)
            stage1(jhi - unit, 1)
            stage2(jhi - unit, 1)
            stage1(jhi - 2 * unit, 0)
            return carry

        lax.fori_loop(0, n_pairs - 1, body, 0)
        jhi = first - 2 * (n_pairs - 1) * unit
        stage2(jhi, 0)
        stage1(jhi - unit, 1)
        stage2(jhi - unit, 1)

    o = jnp.where(lo, acc_ref[0:tb], acc_ref[tb:2 * tb])
    o_ref[...] = (_half_lane_rms(o, lo) * g_ref[...]).astype(o_ref.dtype)


def _sb_prompt(qb, kb, vb, bias, g2, *, tb, unit):
    b, t, w = qb.shape
    pairs = w // LANES
    return pl.pallas_call(
        functools.partial(_sb_prompt_kernel, tb=tb, unit=unit),
        grid_spec=pltpu.PrefetchScalarGridSpec(
            num_scalar_prefetch=0,
            grid=(b, pairs, t // tb),
            in_specs=[pl.BlockSpec(memory_space=pltpu.SMEM),
                      pl.BlockSpec((None, tb, LANES), lambda i, p, j: (i, j, p)),
                      pl.BlockSpec((None, t, LANES), lambda i, p, j: (i, 0, p)),
                      pl.BlockSpec((None, t, LANES), lambda i, p, j: (i, 0, p)),
                      pl.BlockSpec((1, LANES), lambda i, p, j: (0, 0))],
            out_specs=pl.BlockSpec((None, tb, LANES), lambda i, p, j: (i, j, p)),
            scratch_shapes=[pltpu.VMEM((2 * tb, LANES), F32), pltpu.VMEM((2 * tb, LANES), F32),
                            pltpu.VMEM((2, 2 * tb, unit * tb), BF16), pltpu.VMEM((2, 2 * tb, unit * tb), F32)]),
        out_shape=jax.ShapeDtypeStruct((b, t, w), BF16),
        compiler_params=_cparams(("parallel", "parallel", "arbitrary")),
        name="sb_attn_prompt",
    )(bias, qb, kb, vb, g2)


def _page_copy(pt_ref, pool_ref, buf_ref, sem_ref, seq, slot, pg):
    return pltpu.make_async_copy(pool_ref.at[pt_ref[seq, pg]], buf_ref.at[slot, pg], sem_ref.at[slot])


def _sb_sample_kernel(pt_ref, bias_ref, q_ref, kn_ref, vn_ref, g_ref, kpool_ref, vpool_ref, o_ref,
                      kbuf_ref, vbuf_ref, ksem_ref, vsem_ref, newk_ref, newv_ref, *, n_pages, n_q):
    seq = pl.program_id(0)
    n_seq = pl.num_programs(0)
    slot = seq & 1
    rows = SB_HEADS * n_q
    page = kbuf_ref.shape[3]
    width = SB_WIDTH
    pools = ((kpool_ref, kbuf_ref, ksem_ref), (vpool_ref, vbuf_ref, vsem_ref))

    @pl.when(seq == 0)
    def _():
        for pool, buf, sem in pools:
            for pg in range(n_pages):
                _page_copy(pt_ref, pool, buf, sem, 0, 0, pg).start()

    @pl.when(seq + 1 < n_seq)
    def _():
        for pool, buf, sem in pools:
            for pg in range(n_pages):
                _page_copy(pt_ref, pool, buf, sem, seq + 1, 1 - slot, pg).start()

    r1 = lax.broadcasted_iota(jnp.int32, (rows, 1), 0)
    later_keys = jnp.where(lax.broadcasted_iota(jnp.int32, (page, page), 0)
                           > lax.broadcasted_iota(jnp.int32, (page, page), 1), 1.0, 0.0).astype(BF16)
    head_of_lane = lax.broadcasted_iota(jnp.int32, (rows, width), 1) // SB_DIM
    head_of_row = lax.broadcasted_iota(jnp.int32, (rows, width), 0) // n_q
    qrep = jnp.concatenate([q_ref[...].astype(F32)] * SB_HEADS, axis=0)
    qx = jnp.where(head_of_lane == head_of_row, qrep, 0.0).astype(BF16)
    bias = bias_ref[...]

    newk_ref[...] = jnp.zeros_like(newk_ref)
    newv_ref[...] = jnp.zeros_like(newv_ref)
    newk_ref[0:n_q, :] = kn_ref[...]
    newv_ref[0:n_q, :] = vn_ref[...]
    z = _dot_nt(qx, newk_ref[...].astype(BF16)) + bias
    sp, lb = _softplus_parts(z)
    valid = lax.broadcasted_iota(jnp.int32, (rows, page), 1) < (r1 & (n_q - 1))
    sp = jnp.where(valid, sp, 0.0)
    att = jnp.where(valid, jnp.exp(lb - _dot(sp.astype(BF16), later_keys)), 0.0)
    acc = _dot(att.astype(BF16), newv_ref[...].astype(BF16))
    run = jnp.sum(sp, axis=-1, keepdims=True)

    for pool, buf, sem in pools:
        for pg in range(n_pages):
            _page_copy(pt_ref, pool, buf, sem, seq, slot, pg).wait()

    order = [n_pages - 1 - r for r in range(n_pages)]
    k_all = jnp.concatenate([kbuf_ref[slot, pg].astype(BF16) for pg in order], axis=1)
    z = _dot(qx, k_all) + bias
    sp, lb = _softplus_parts(z)
    sp16 = sp.astype(BF16)
    later = _dot(jnp.concatenate([sp16[:, r * page:(r + 1) * page] for r in range(n_pages)], axis=0),
                 later_keys)
    args = []
    for r in range(n_pages):
        cs = slice(r * page, (r + 1) * page)
        args.append(lb[:, cs] - later[r * rows:(r + 1) * rows] - run)
        run = run + jnp.sum(sp[:, cs], axis=-1, keepdims=True)
    att = jnp.exp(jnp.concatenate(args, axis=1)).astype(BF16)
    v_all = jnp.concatenate([vbuf_ref[slot, pg].astype(BF16) for pg in order], axis=1)
    acc = acc + _dot_nt(att, v_all)

    lane_head = lax.broadcasted_iota(jnp.int32, (n_q, width), 1) // SB_DIM
    o = jnp.zeros((n_q, width), F32)
    for hh in range(SB_HEADS):
        o = o + jnp.where(lane_head == hh, acc[hh * n_q:(hh + 1) * n_q, :], 0.0)
    lo = lax.broadcasted_iota(jnp.int32, (1, LANES), 1) < SB_DIM
    for c in range(width // LANES):
        sl = slice(c * LANES, (c + 1) * LANES)
        o_ref[:, sl] = _half_lane_rms(o[:, sl], lo) * g_ref[:, sl]


def _sb_sample(page_table, bias_rows, q3, kn3, vn3, g_sb, cache_k, cache_v):
    bs, n_q, w = q3.shape
    n_pages = page_table.shape[1]
    page = cache_k.shape[2]
    rows = SB_HEADS * n_q
    tok = lambda: pl.BlockSpec((None, n_q, w), lambda i, pt: (i, 0, 0))
    const = lambda shp: pl.BlockSpec(shp, lambda i, pt: (0,) * len(shp))
    hbm = lambda: pl.BlockSpec(memory_space=pl.ANY)
    return pl.pallas_call(
        functools.partial(_sb_sample_kernel, n_pages=n_pages, n_q=n_q),
        grid_spec=pltpu.PrefetchScalarGridSpec(
            num_scalar_prefetch=1,
            grid=(bs,),
            in_specs=[const((rows, 1)), tok(), tok(), tok(), const((1, w)), hbm(), hbm()],
            out_specs=tok(),
            scratch_shapes=[pltpu.VMEM((2, n_pages, w, page), F32), pltpu.VMEM((2, n_pages, w, page), F32),
                            pltpu.SemaphoreType.DMA((2,)), pltpu.SemaphoreType.DMA((2,)),
                            pltpu.VMEM((page, w), F32), pltpu.VMEM((page, w), F32)]),
        out_shape=jax.ShapeDtypeStruct((bs, n_q, w), F32),
        compiler_params=_cparams(("arbitrary",)),
        name="sb_attn_paged",
    )(page_table, bias_rows, q3, kn3, vn3, g_sb, cache_k, cache_v)


def _split3(x):
    hi = x.astype(BF16)
    r = x - hi.astype(F32)
    mid = r.astype(BF16)
    lo = (r - mid.astype(F32)).astype(BF16)
    return hi, mid, lo


def _dn_local(blocks, group):
    n = blocks[0][3].shape[0]
    units = [(bi, h) for bi in range(len(blocks)) for h in range(DN_HEADS)]
    hs = [slice(h * DN_DIM, (h + 1) * DN_DIM) for h in range(DN_HEADS)]
    row = lax.broadcasted_iota(jnp.int32, (n, n), 0)
    col = lax.broadcasted_iota(jnp.int32, (n, n), 1)
    shift = group.bit_length() - 1
    same = jnp.where((row >> shift) == (col >> shift), 1.0, 0.0)
    low = jnp.where(row >= col, same, 0.0)
    strict = jnp.where(row > col, same, 0.0)
    low16 = low.astype(BF16)
    same16 = same.astype(BF16)
    gparts = [_split3(blk[3]) for blk in blocks]
    g_all = [sum(_dot(low16, m) for m in gp) for gp in gparts]
    t_all = [sum(_dot(same16, m) for m in gp) for gp in gparts]
    g_row = [jnp.broadcast_to(g_all[bi][:, DN_HEADS + h:DN_HEADS + h + 1], (n, n)) for bi, h in units]
    gtot = [jnp.broadcast_to(t_all[bi][:, DN_HEADS + h:DN_HEADS + h + 1], (n, n)) for bi, h in units]
    un = range(len(units))
    decay = [low * jnp.exp(jnp.minimum(g_row[u] - g_row[u].T, 0.0)) for u in un]
    beta = [blocks[bi][3][:, h:h + 1] for bi, h in units]
    k = [blocks[bi][1][:, hs[h]] for bi, h in units]
    k16 = [k[u].astype(BF16) for u in un]
    kb = [k[u] * beta[u] for u in un]
    a_mat = [strict * (_dot_nt(kb[u].astype(BF16), k16[u]) * decay[u]) for u in un]
    eye = jnp.where(row == col, 1.0, 0.0)
    inv = [eye - a_mat[u] * jnp.where((row >> 1) == (col >> 1), 1.0, 0.0) for u in un]
    for lb in range(1, shift):
        sel = jnp.where((row >> (lb + 1)) == (col >> (lb + 1)),
                        jnp.where(((row >> lb) & 1) > ((col >> lb) & 1), 1.0, 0.0), 0.0)
        inv16 = [inv[u].astype(BF16) for u in un]
        x = [_dot((a_mat[u] * sel).astype(BF16), inv16[u]) for u in un]
        inv = [inv[u] - _dot(inv16[u], x[u].astype(BF16)) for u in un]
    eg = [jnp.exp(g_row[u]) for u in un]
    v = [blocks[bi][2][:, hs[h]] for bi, h in units]
    rhs = [jnp.concatenate([v[u] * beta[u], kb[u] * eg[u]], axis=-1).astype(BF16) for u in un]
    sol = [_dot(inv[u].astype(BF16), rhs[u]) for u in un]
    q = [blocks[bi][0][:, hs[h]] for bi, h in units]
    intra = [_dot_nt(q[u].astype(BF16), k16[u]) * decay[u] for u in un]
    usol = [sol[u][:, :DN_DIM] for u in un]
    kcum = [sol[u][:, DN_DIM:] for u in un]
    qdec = [q[u] * eg[u] for u in un]
    kdec = [k[u] * jnp.exp(gtot[u] - g_row[u]) for u in un]
    return usol, kcum, intra, qdec, kdec, gtot


def _dn_seq_kernel(q_ref, k_ref, v_ref, gate_ref, o_ref, sout_ref, s_ref, *, group):
    @pl.when(pl.program_id(0) == 0)
    def _():
        s_ref[...] = jnp.zeros_like(s_ref)

    nb, n = q_ref.shape[0], q_ref.shape[1]
    blocks = [(q_ref.at[bi], k_ref.at[bi], v_ref.at[bi], gate_ref[bi]) for bi in range(nb)]
    usol, kcum, intra, qdec, kdec, gtot = _dn_local(blocks, group)
    units = [(bi, h) for bi in range(nb) for h in range(DN_HEADS)]
    un = range(len(units))
    s = [s_ref[bi, h] for bi, h in units]
    vnews = [[] for _ in un]
    outs = [[] for _ in un]
    for c in range(n // group):
        rs = slice(c * group, (c + 1) * group)
        s16 = [s[u].astype(BF16) for u in un]
        lhs = [jnp.concatenate([kcum[u][rs], qdec[u][rs]], axis=0).astype(BF16) for u in un]
        both = [_dot(lhs[u], s16[u]) for u in un]
        vnew = [usol[u][rs] - both[u][:group] for u in un]
        upd = [_dot_tn(kdec[u][rs].astype(BF16), vnew[u].astype(BF16)) for u in un]
        s = [s[u] * jnp.exp(gtot[u][c * group:c * group + 1, :]) + upd[u] for u in un]
        for u in un:
            vnews[u].append(vnew[u])
            outs[u].append(both[u][group:])
    for u, (bi, h) in enumerate(units):
        s_ref[bi, h] = s[u]
        vfull = jnp.concatenate(vnews[u], axis=0).astype(BF16)
        o_ref[bi, :, h * DN_DIM:(h + 1) * DN_DIM] = (jnp.concatenate(outs[u], axis=0)
                                                     + _dot(intra[u].astype(BF16), vfull))

    @pl.when(pl.program_id(0) == pl.num_programs(0) - 1)
    def _():
        sout_ref[...] = s_ref[...]


def _dn_seq(qd, kd, vd, gates):
    b, t, w = qd.shape
    n = DN_BLOCK
    row = lambda c: pl.BlockSpec((b, n, c), lambda j: (0, j, 0))
    state = (b, DN_HEADS, DN_DIM, DN_DIM)
    return pl.pallas_call(
        functools.partial(_dn_seq_kernel, group=min(64, n)),
        grid=(t // n,),
        in_specs=[row(w), row(w), row(w), row(LANES)],
        out_specs=[row(w), pl.BlockSpec(state, lambda j: (0, 0, 0, 0))],
        out_shape=[jax.ShapeDtypeStruct((b, t, w), F32), jax.ShapeDtypeStruct(state, F32)],
        scratch_shapes=[pltpu.VMEM(state, F32)],
        compiler_params=_cparams(("arbitrary",)),
        name="deltanet_seq",
    )(qd, kd, vd, gates)


def _dn_grouped_kernel(q_ref, k_ref, v_ref, gate_ref, s0_ref, o_ref, sout_ref,
                       u_ref, kc_ref, qd_ref, kd_ref, gt_ref, vn_ref, oq_ref, *, group):
    n = q_ref.shape[0]
    heads = range(DN_HEADS)
    row1 = lax.broadcasted_iota(jnp.int32, (n, 1), 0)
    shift = group.bit_length() - 1
    u, kcum, intra, qdec, kdec, gtot = _dn_local([(q_ref, k_ref, v_ref, gate_ref[...])], group)
    for h in heads:
        u_ref[h] = u[h]
        kc_ref[h] = kcum[h]
        qd_ref[h] = qdec[h]
        kd_ref[h] = kdec[h]
        gt_ref[h] = gtot[h]
    vn_ref[...] = jnp.zeros_like(vn_ref)

    def per_seq(bi, carry):
        rs = pl.ds(pl.multiple_of(bi * group, group), group)
        mine = (row1 >> shift) == bi
        s = [s0_ref[bi, h] for h in heads]
        lhs = [jnp.concatenate([kc_ref[h, rs, :], qd_ref[h, rs, :]], axis=0).astype(BF16) for h in heads]
        both = [_dot(lhs[h], s[h].astype(BF16)) for h in heads]
        for h in heads:
            oq_ref[h, rs, :] = both[h][group:]
            vn_ref[h, rs, :] = u_ref[h, rs, :] - both[h][:group]
        kmask = [jnp.where(mine, kd_ref[h], 0.0).astype(BF16) for h in heads]
        upd = [_dot_tn(kmask[h], vn_ref[h].astype(BF16)) for h in heads]
        for h in heads:
            sout_ref[bi, h] = s[h] * jnp.exp(gt_ref[h, rs, :][0:1, :]) + upd[h]
        return carry

    lax.fori_loop(0, n // group, per_seq, 0)
    for h in heads:
        o_ref[:, h * DN_DIM:(h + 1) * DN_DIM] = (oq_ref[h]
                                                 + _dot(intra[h].astype(BF16), vn_ref[h].astype(BF16)))


def _dn_grouped(qd, kd, vd, gates, s0, *, group):
    nrows, w = qd.shape
    n = DN_BLOCK
    per = n // group
    row = lambda c: pl.BlockSpec((n, c), lambda i: (i, 0))
    st = pl.BlockSpec((per, DN_HEADS, DN_DIM, DN_DIM), lambda i: (i, 0, 0, 0))
    return pl.pallas_call(
        functools.partial(_dn_grouped_kernel, group=group),
        grid=(nrows // n,),
        in_specs=[row(w), row(w), row(w), row(LANES), st],
        out_specs=[row(w), st],
        out_shape=[jax.ShapeDtypeStruct((nrows, w), F32), jax.ShapeDtypeStruct(s0.shape, F32)],
        scratch_shapes=[pltpu.VMEM((DN_HEADS, n, DN_DIM), F32)] * 7,
        compiler_params=_cparams(("parallel",)),
        name="deltanet_grouped",
    )(qd, kd, vd, gates, s0)


def _outproj_kernel(osb_ref, odn_ref, zg_ref, gdn_ref, x_ref, gt_ref, sc_ref, sh_ref, gn_ref, w_ref,
                    x1_ref, h2_ref):
    parts = [osb_ref[...].astype(BF16)]
    for hh in range(DN_HEADS):
        sl = slice(hh * DN_DIM, (hh + 1) * DN_DIM)
        parts.append((_rms_rows(odn_ref[:, sl], gdn_ref[...]) * zg_ref[:, sl]).astype(BF16))
    mix = _dot(jnp.concatenate(parts, axis=1), w_ref[...])
    x1 = x_ref[...] + gt_ref[...] * mix
    x1_ref[...] = x1
    h2_ref[...] = (_rms_rows(x1, gn_ref[...]) * (1.0 + sc_ref[...]) + sh_ref[...]).astype(BF16)


def _mod_specs(x, tm, per_row):
    d = x.shape[-1]
    if per_row:
        n = x.shape[0]
        row = lambda c: pl.BlockSpec((tm, c), lambda i: (i, 0))
        return (n // tm,), row, row(d), (n,), ("arbitrary",)
    b, t, _ = x.shape
    row = lambda c: pl.BlockSpec((None, tm, c), lambda i, j: (i, j, 0))
    per_b = pl.BlockSpec((None, 1, d), lambda i, j: (i, 0, 0))
    return (b, t // tm), row, per_b, (b, t), ("parallel", "arbitrary")


def _outproj(osb, odn, zg, gdn, x, gt, sc, sh, gn, w, *, tm, per_row):
    d = x.shape[-1]
    grid, row, mod, lead, sem = _mod_specs(x, tm, per_row)
    return pl.pallas_call(
        _outproj_kernel,
        grid=grid,
        in_specs=[row(SB_WIDTH), row(DN_WIDTH), row(DN_WIDTH), _resident(gdn.shape), row(d), mod, mod, mod,
                  _resident(gn.shape), _resident(w.shape)],
        out_specs=[row(d), row(d)],
        out_shape=[jax.ShapeDtypeStruct(lead + (d,), F32), jax.ShapeDtypeStruct(lead + (d,), BF16)],
        compiler_params=_cparams(sem),
        name="outproj_rows" if per_row else "outproj_seq",
    )(osb, odn, zg, gdn, x, gt, sc, sh, gn, w)


def _ffn_kernel(*refs, grouped, cw):
    if grouped:
        h_ref, x1_ref, gt_ref, wu_ref, wc_ref, wd_ref, buf_ref, y_ref, nbuf_ref, act_ref = refs
        carry_ref = None
    else:
        h_ref, x1_ref, gt_ref, wu_ref, wc_ref, wd_ref, y_ref, nbuf_ref, act_ref, carry_ref = refs
        buf_ref = None

        @pl.when(pl.program_id(1) == 0)
        def _():
            carry_ref[...] = jnp.zeros_like(carry_ref)

    hb = h_ref[...]
    tm = hb.shape[0]
    dff = wd_ref.shape[0]
    taps = wc_ref.shape[0]
    for c in range(dff // cw):
        halves = []
        for base in (0, dff):
            cs = slice(base + c * cw, base + (c + 1) * cw)
            cur = _dot(hb, wu_ref[:, cs])
            if grouped:
                y = _causal_conv(cur, wc_ref[:, cs], None, True, buf_ref[:, cs])
                nbuf_ref[:, cs] = pltpu.roll(cur, tm - (SUBLANES - (taps - 1)), 0)
            else:
                y = _causal_conv(cur, wc_ref[:, cs], carry_ref[:, cs], False, None)
                carry_ref[:, cs] = cur[tm - SUBLANES:tm]
                nbuf_ref[:, cs] = cur[tm - SUBLANES:tm]
            halves.append(y)
        act_ref[:, c * cw:(c + 1) * cw] = (_silu(halves[1]) * halves[0]).astype(BF16)
    y_ref[...] = x1_ref[...] + gt_ref[...] * _dot(act_ref[...], wd_ref[...])


def _ffn(h2, x1, gt, wu, wc, wd, bufp, *, tm, per_row, cw=256):
    d = x1.shape[-1]
    up = wu.shape[1]
    grid, row, mod, lead, sem = _mod_specs(x1, tm, per_row)
    in_specs = [row(d), row(d), mod, _resident(wu.shape), _resident(wc.shape), _resident(wd.shape)]
    args = [h2, x1, gt, wu, wc, wd]
    if per_row:
        in_specs.append(row(up))
        args.append(bufp)
        nb_shape, nb_spec, scratch = lead + (up,), row(up), [pltpu.VMEM((tm, up // 2), BF16)]
    else:
        nb_shape = (lead[0], SUBLANES, up)
        nb_spec = pl.BlockSpec((None, SUBLANES, up), lambda i, j: (i, 0, 0))
        scratch = [pltpu.VMEM((tm, up // 2), BF16), pltpu.VMEM((SUBLANES, up), F32)]
    return pl.pallas_call(
        functools.partial(_ffn_kernel, grouped=per_row, cw=cw),
        grid=grid, in_specs=in_specs,
        out_specs=[row(d), nb_spec],
        out_shape=[jax.ShapeDtypeStruct(lead + (d,), F32), jax.ShapeDtypeStruct(nb_shape, F32)],
        scratch_shapes=scratch,
        compiler_params=_cparams(sem),
        name="convffn_rows" if per_row else "convffn_seq",
    )(*args)


def _prep_weights(g_attn_norm, w_in, g_q, g_k, sb_bias, g_sb_out, w_dn_conv, a_log, dt_bias, g_dn_out,
                  w_out, g_ffn_norm, w_up, w_ffn_conv, w_down):
    d = w_in.shape[0]
    o1 = 3 * SB_WIDTH
    o2 = o1 + 3 * DN_WIDTH
    o3 = o2 + 2 * DN_HEADS
    pad = jnp.zeros((d, LANES - 2 * DN_HEADS), w_in.dtype)
    w1 = jnp.concatenate([w_in[:, :o2], w_in[:, o3:], w_in[:, o2:o3], pad], axis=1).astype(BF16)
    lane_pad = lambda v: jnp.zeros((1, LANES), F32).at[0, DN_HEADS:2 * DN_HEADS].set(v)
    return dict(
        gn1=g_attn_norm.reshape(1, d), w1=w1,
        gq=jnp.tile(g_q, SB_HEADS).reshape(1, SB_WIDTH), gk=jnp.tile(g_k, SB_HEADS).reshape(1, SB_WIDTH),
        bias=sb_bias.astype(F32), g_sb=jnp.tile(g_sb_out, SB_HEADS).reshape(1, SB_WIDTH),
        wc_dn=w_dn_conv, al=lane_pad(a_log), dt=lane_pad(dt_bias), gdn=g_dn_out.reshape(1, DN_DIM),
        w_out=w_out.astype(BF16), gn2=g_ffn_norm.reshape(1, d), w_up=w_up.astype(BF16),
        wc_ffn=w_ffn_conv, w_down=w_down.astype(BF16))


def _layer_prompt(x, mod, p):
    b, t, d = x.shape
    sh1, sc1, gt1, sh2, sc2, gt2 = [m.reshape(b, 1, d) for m in jnp.split(mod, 6, axis=-1)]
    qb, kb, vb, kf, vf, qd, kd, vd, zg, gates, dnbuf = _inproj(
        x, sc1, sh1, p["gn1"], p["w1"], p["gq"], p["gk"], p["wc_dn"], p["al"], p["dt"], None, tm=512)
    osb = _sb_prompt(qb, kb, vb, p["bias"], p["g_sb"][:, :LANES], tb=256, unit=1)
    odn, s_new = _dn_seq(qd, kd, vd, gates)
    x1, h2 = _outproj(osb, odn, zg, p["gdn"], x, gt1, sc2, sh2, p["gn2"], p["w_out"], tm=512, per_row=False)
    y, ffbuf = _ffn(h2, x1, gt2, p["w_up"], p["wc_ffn"], p["w_down"], None, tm=512, per_row=False)
    k_dn = p["wc_dn"].shape[0] - 1
    k_ff = p["wc_ffn"].shape[0] - 1
    return (y, kf.reshape(b, t, SB_HEADS, SB_DIM), vf.reshape(b, t, SB_HEADS, SB_DIM), s_new,
            dnbuf[:, SUBLANES - k_dn:], ffbuf[:, SUBLANES - k_ff:])


def _layer_sample(x, mod, page_table, cache_k, cache_v, s0, dn_buf, ffn_buf, p):
    bs, n_q, d = x.shape
    assert n_q == SUBLANES
    n = bs * n_q
    rep = lambda m: jnp.broadcast_to(m[:, None, :], (bs, n_q, d)).reshape(n, d)
    sh1, sc1, gt1, sh2, sc2, gt2 = [rep(m) for m in jnp.split(mod, 6, axis=-1)]
    padrows = lambda buf: jnp.pad(buf, ((0, 0), (0, n_q - buf.shape[1]), (0, 0))).reshape(n, buf.shape[2])
    xf = x.reshape(n, d)
    qb, kb, vb, kf, vf, qd, kd, vd, zg, gates, dnbuf = _inproj(
        xf, sc1, sh1, p["gn1"], p["w1"], p["gq"], p["gk"], p["wc_dn"], p["al"], p["dt"], padrows(dn_buf), tm=256)
    del kb, vb
    w = SB_WIDTH
    bias_rows = jnp.repeat(p["bias"], n_q).reshape(SB_HEADS * n_q, 1)
    n_phys, page = cache_k.shape[0], cache_k.shape[1]
    pool_t = lambda c: jnp.transpose(c, (0, 2, 3, 1)).reshape(n_phys, w, page)
    osb = _sb_sample(page_table, bias_rows, qb.reshape(bs, n_q, w), kf.reshape(bs, n_q, w),
                     vf.reshape(bs, n_q, w), p["g_sb"], pool_t(cache_k), pool_t(cache_v))
    odn, s_new = _dn_grouped(qd, kd, vd, gates, s0, group=n_q)
    x1, h2 = _outproj(osb.reshape(n, w), odn, zg, p["gdn"], xf, gt1, sc2, sh2, p["gn2"], p["w_out"],
                      tm=256, per_row=True)
    y, ffbuf = _ffn(h2, x1, gt2, p["w_up"], p["wc_ffn"], p["w_down"], padrows(ffn_buf), tm=256, per_row=True)
    k_dn = p["wc_dn"].shape[0] - 1
    k_ff = p["wc_ffn"].shape[0] - 1
    return (y.reshape(bs, n_q, d), kf.reshape(bs, n_q, SB_HEADS, SB_DIM), vf.reshape(bs, n_q, SB_HEADS, SB_DIM),
            s_new, dnbuf.reshape(bs, n_q, -1)[:, :k_dn], ffbuf.reshape(bs, n_q, -1)[:, :k_ff])


def kernel(x_prompt, x_sample, c_prompt, c_sample, cache_k, cache_v, page_table, state_delta, state_dn_conv, state_ffn_conv, w_ada, b_ada, g_attn_norm, w_in, g_q, g_k, sb_bias, g_sb_out, w_dn_conv, a_log, dt_bias, g_dn_out, w_out, g_ffn_norm, w_up, w_ffn_conv, w_down):
    depth = w_ada.shape[0]
    bp = x_prompt.shape[0]
    yp, ys = x_prompt, x_sample
    outs = [[] for _ in range(10)]
    c_all = jnp.concatenate([c_prompt, c_sample], axis=0)
    pad_rows = (-c_all.shape[0]) % SUBLANES
    c_all = jnp.pad(c_all, ((0, pad_rows), (0, 0)))
    for l in range(depth):
        p = _prep_weights(g_attn_norm[l], w_in[l], g_q[l], g_k[l], sb_bias[l], g_sb_out[l], w_dn_conv[l],
                          a_log[l], dt_bias[l], g_dn_out[l], w_out[l], g_ffn_norm[l], w_up[l],
                          w_ffn_conv[l], w_down[l])
        mod = _ada(c_all, w_ada[l], b_ada[l])
        yp, kp, vp, sp, dcp, fcp = _layer_prompt(yp, mod[:bp], p)
        ys, ks, vs, ss, dcs, fcs = _layer_sample(ys, mod[bp:bp + x_sample.shape[0]], page_table,
                                                 cache_k[l], cache_v[l], state_delta[l],
                                                 state_dn_conv[l], state_ffn_conv[l], p)
        for lst, val in zip(outs, (kp, vp, ks, vs, sp, ss, dcp, dcs, fcp, fcs)):
            lst.append(val)
    return (yp, ys) + tuple(jnp.stack(o) for o in outs)
```

```python
import functools

import jax
import jax.numpy as jnp
from jax import lax
from jax.experimental import pallas as pl
from jax.experimental.pallas import tpu as pltpu

F32 = jnp.float32
BF16 = jnp.bfloat16

NORM_EPS = 1e-6
LOG2E = 1.4426950408889634
LANES = 128
SUBLANES = 8
VMEM_LIMIT = 56 * 1024 * 1024

SB_HEADS = 8
SB_DIM = 64
SB_WIDTH = SB_HEADS * SB_DIM
DN_HEADS = 4
DN_DIM = 128
DN_WIDTH = DN_HEADS * DN_DIM
DN_BLOCK = 128


def _dot(a, b):
    return jnp.dot(a, b, preferred_element_type=F32)


def _dot_nt(a, b):
    return lax.dot_general(a, b, (((1,), (1,)), ((), ())), preferred_element_type=F32)


def _dot_tn(a, b):
    return lax.dot_general(a, b, (((0,), (0,)), ((), ())), preferred_element_type=F32)


def _silu(x):
    return x * jax.nn.sigmoid(x)


def _exp_neg_abs(x):
    return jnp.exp2(jnp.abs(x) * (-LOG2E))


def _softplus_parts(z):
    l = jnp.log(1.0 + jnp.exp(-jnp.abs(z)))
    return jnp.maximum(z, 0.0) + l, jnp.minimum(z, 0.0) - l


def _rms_rows(x, g):
    ms = jnp.mean(x * x, axis=-1, keepdims=True)
    return x * lax.rsqrt(ms + NORM_EPS) * g


def _half_lane_rms(blk, lo):
    sq = blk * blk
    s_lo = jnp.sum(jnp.where(lo, sq, 0.0), axis=-1, keepdims=True)
    s_hi = jnp.sum(jnp.where(lo, 0.0, sq), axis=-1, keepdims=True)
    ms = jnp.where(lo, s_lo, s_hi) * (1.0 / SB_DIM)
    return blk * lax.rsqrt(ms + NORM_EPS)


def _cparams(sem):
    return pltpu.CompilerParams(dimension_semantics=sem, vmem_limit_bytes=VMEM_LIMIT)


def _resident(shape):
    nd = len(shape)
    return pl.BlockSpec(shape, lambda *_: (0,) * nd, pipeline_mode=pl.Buffered(1))


def _ada_kernel(c_ref, w_ref, b_ref, o_ref):
    a = _silu(c_ref[...]).astype(BF16)
    o_ref[...] = _dot(a, w_ref[...].astype(BF16)) + b_ref[...]


def _ada(c, w, b):
    m, d = c.shape
    n = w.shape[1]
    tn = 512
    return pl.pallas_call(
        _ada_kernel,
        grid=(n // tn,),
        in_specs=[pl.BlockSpec((m, d), lambda j: (0, 0)),
                  pl.BlockSpec((d, tn), lambda j: (0, j)),
                  pl.BlockSpec((1, tn), lambda j: (0, j))],
        out_specs=pl.BlockSpec((m, tn), lambda j: (0, j)),
        out_shape=jax.ShapeDtypeStruct((m, n), F32),
        compiler_params=_cparams(("parallel",)),
        name="adaln_mod",
    )(c, w, b.reshape(1, n))


C_SB = 0
C_DN = 3 * SB_WIDTH
C_Z = C_DN + 3 * DN_WIDTH
C_G = C_Z + DN_WIDTH
C_END = C_G + LANES


def _causal_conv(cur, taps, prev_rows, grouped, bufp):
    w = taps.shape[0]
    tm = cur.shape[0]
    y = taps[w - 1:w, :] * cur
    if grouped:
        rowm = lax.broadcasted_iota(jnp.int32, (tm, 1), 0) & (SUBLANES - 1)
        for s in range(1, w):
            r = pltpu.roll(cur, s, 0)
            back = (w - 1) - s
            bs = bufp if back == 0 else pltpu.roll(bufp, tm - back, 0)
            y = y + taps[w - 1 - s:w - s, :] * jnp.where(rowm >= s, r, bs)
        return y
    row8 = lax.broadcasted_iota(jnp.int32, (SUBLANES, 1), 0)
    yh = taps[w - 1:w, :] * cur[0:SUBLANES]
    for s in range(1, w):
        r = pltpu.roll(cur, s, 0)
        y = y + taps[w - 1 - s:w - s, :] * r
        head = jnp.where(row8 < s, pltpu.roll(prev_rows, s, 0), r[0:SUBLANES])
        yh = yh + taps[w - 1 - s:w - s, :] * head
    return jnp.concatenate([yh, y[SUBLANES:]], axis=0)


def _inproj_kernel(*refs, grouped):
    if grouped:
        (x_ref, sc_ref, sh_ref, gn_ref, w_ref, gq_ref, gk_ref, wc_ref, al_ref, dt_ref, buf_ref,
         qb_ref, kb_ref, vb_ref, kf_ref, vf_ref, qd_ref, kd_ref, vd_ref, zg_ref, gate_ref, nbuf_ref) = refs
        carry_ref = None
    else:
        (x_ref, sc_ref, sh_ref, gn_ref, w_ref, gq_ref, gk_ref, wc_ref, al_ref, dt_ref,
         qb_ref, kb_ref, vb_ref, kf_ref, vf_ref, qd_ref, kd_ref, vd_ref, zg_ref, gate_ref, nbuf_ref,
         carry_ref) = refs
        buf_ref = None

        @pl.when(pl.program_id(1) == 0)
        def _():
            carry_ref[...] = jnp.zeros_like(carry_ref)

    x = x_ref[...]
    h = _rms_rows(x, gn_ref[...]) * (1.0 + sc_ref[...]) + sh_ref[...]
    hb = h.astype(BF16)
    tm = x.shape[0]
    lane = lax.broadcasted_iota(jnp.int32, (1, LANES), 1)
    lo = lane < SB_DIM

    for c in range(SB_WIDTH // LANES):
        sl = slice(c * LANES, (c + 1) * LANES)
        pq = _dot(hb, w_ref[:, C_SB + c * LANES:C_SB + (c + 1) * LANES])
        qn = _half_lane_rms(pq, lo) * gq_ref[:, sl]
        qb_ref[:, sl] = (qn * (SB_DIM ** -0.5)).astype(BF16)
        pk = _dot(hb, w_ref[:, C_SB + SB_WIDTH + c * LANES:C_SB + SB_WIDTH + (c + 1) * LANES])
        kn = _half_lane_rms(pk, lo) * gk_ref[:, sl]
        kf_ref[:, sl] = kn
        kb_ref[:, sl] = kn.astype(BF16)
    pv = _dot(hb, w_ref[:, C_SB + 2 * SB_WIDTH:C_SB + 3 * SB_WIDTH])
    vf_ref[...] = pv
    vb_ref[...] = pv.astype(BF16)

    cur = _dot(hb, w_ref[:, C_DN:C_Z])
    if grouped:
        y = _causal_conv(cur, wc_ref[...], None, True, buf_ref[...])
        nbuf_ref[...] = pltpu.roll(cur, tm - (SUBLANES - (wc_ref.shape[0] - 1)), 0)
    else:
        y = _causal_conv(cur, wc_ref[...], carry_ref[...], False, None)
        carry_ref[...] = cur[tm - SUBLANES:tm]
        nbuf_ref[...] = cur[tm - SUBLANES:tm]
    a = _silu(y)
    for hh in range(DN_HEADS):
        sl = slice(hh * DN_DIM, (hh + 1) * DN_DIM)
        qh = a[:, hh * DN_DIM:(hh + 1) * DN_DIM]
        kh = a[:, DN_WIDTH + hh * DN_DIM:DN_WIDTH + (hh + 1) * DN_DIM]
        qd_ref[:, sl] = qh * (lax.rsqrt(jnp.sum(qh * qh, axis=-1, keepdims=True) + NORM_EPS) * (DN_DIM ** -0.5))
        kd_ref[:, sl] = kh * lax.rsqrt(jnp.sum(kh * kh, axis=-1, keepdims=True) + NORM_EPS)
    vd_ref[...] = a[:, 2 * DN_WIDTH:3 * DN_WIDTH]

    zg_ref[...] = _silu(_dot(hb, w_ref[:, C_Z:C_G]))

    gb = _dot(hb, w_ref[:, C_G:C_END])
    sp, _ = _softplus_parts(gb + dt_ref[...])
    gate_ref[...] = jnp.where(lane < DN_HEADS, jax.nn.sigmoid(gb), -jnp.exp(al_ref[...]) * sp)


def _inproj(x, sc, sh, gn, w1, gq, gk, wc, al, dt, bufp, *, tm):
    grouped = bufp is not None
    d = x.shape[-1]
    if grouped:
        n = x.shape[0]
        grid = (n // tm,)
        row = lambda c: pl.BlockSpec((tm, c), lambda i: (i, 0))
        in_specs = [row(d), row(d), row(d)]
        lead = (n,)
        sem = ("arbitrary",)
        nb_shape, nb_spec = (n, 3 * DN_WIDTH), row(3 * DN_WIDTH)
        scratch = []
    else:
        b, t, _ = x.shape
        grid = (b, t // tm)
        row = lambda c: pl.BlockSpec((None, tm, c), lambda i, j: (i, j, 0))
        per_b = pl.BlockSpec((None, 1, d), lambda i, j: (i, 0, 0))
        in_specs = [row(d), per_b, per_b]
        lead = (b, t)
        sem = ("parallel", "arbitrary")
        nb_shape = (b, SUBLANES, 3 * DN_WIDTH)
        nb_spec = pl.BlockSpec((None, SUBLANES, 3 * DN_WIDTH), lambda i, j: (i, 0, 0))
        scratch = [pltpu.VMEM((SUBLANES, 3 * DN_WIDTH), F32)]
    in_specs += [_resident(gn.shape), _resident(w1.shape), _resident(gq.shape), _resident(gk.shape),
                 _resident(wc.shape), _resident(al.shape), _resident(dt.shape)]
    args = [x, sc, sh, gn, w1, gq, gk, wc, al, dt]
    if grouped:
        in_specs.append(row(3 * DN_WIDTH))
        args.append(bufp)
    out_cols = [(SB_WIDTH, BF16)] * 3 + [(SB_WIDTH, F32)] * 2 + [(DN_WIDTH, F32)] * 4 + [(LANES, F32)]
    out_shape = [jax.ShapeDtypeStruct(lead + (c,), dt_) for c, dt_ in out_cols]
    out_specs = [row(c) for c, _ in out_cols]
    out_shape.append(jax.ShapeDtypeStruct(nb_shape, F32))
    out_specs.append(nb_spec)
    return pl.pallas_call(
        functools.partial(_inproj_kernel, grouped=grouped),
        grid=grid, in_specs=in_specs, out_specs=out_specs, out_shape=out_shape,
        scratch_shapes=scratch,
        compiler_params=_cparams(sem),
        name="inproj_grouped" if grouped else "inproj_seq",
    )(*args)


def _sb_prompt_kernel(bias_ref, q_ref, k_ref, v_ref, g_ref, o_ref, acc_ref, run_ref, sp_ref, lb_ref,
                      *, tb, unit, bounded):
    softplus = ((lambda z: jnp.log(1.0 + jnp.exp(z))) if bounded else
                (lambda z: jnp.maximum(z, 0.0) + jnp.log(1.0 + _exp_neg_abs(z))))
    p = pl.program_id(1)
    i = pl.program_id(2)
    lane = lax.broadcasted_iota(jnp.int32, (1, LANES), 1)
    lo = lane < SB_DIM
    q = q_ref[...]
    zero = jnp.zeros_like(q)
    q2 = jnp.concatenate([jnp.where(lo, q, zero), jnp.where(lo, zero, q)], axis=0)
    head_row = lax.broadcasted_iota(jnp.int32, (2 * tb, LANES), 0) < tb
    lane2 = lax.broadcasted_iota(jnp.int32, (2 * tb, LANES), 1)
    b_parts = [m.astype(F32) for m in _split3(jnp.where(head_row, bias_ref[2 * p], bias_ref[2 * p + 1]))]
    bias_lanes = jnp.where(lane2 == 0, b_parts[0],
                           jnp.where(lane2 == 1, b_parts[1], jnp.where(lane2 == 2, b_parts[2], 0.0)))
    q2x = jnp.concatenate([q2, bias_lanes.astype(BF16)], axis=1)
    ones_lanes = jnp.where(lax.broadcasted_iota(jnp.int32, (tb, LANES), 1) < 3, 1.0, 0.0).astype(BF16)

    def with_ones(ks):
        n = ks.shape[0] // tb
        ones = ones_lanes if n == 1 else jnp.concatenate([ones_lanes] * n, axis=0)
        return jnp.concatenate([ks, ones], axis=1)
    row = lax.broadcasted_iota(jnp.int32, (tb, tb), 0)
    col = lax.broadcasted_iota(jnp.int32, (tb, tb), 1)
    later_keys = jnp.where(row > col, 1.0, 0.0).astype(BF16)
    qpos = lax.broadcasted_iota(jnp.int32, (2 * tb, tb), 0) & (tb - 1)
    valid = lax.broadcasted_iota(jnp.int32, (2 * tb, tb), 1) < qpos

    acc_ref[...] = jnp.zeros_like(acc_ref)
    run_ref[...] = jnp.zeros_like(run_ref)

    def block(j, masked):
        start = pl.multiple_of(j * tb, tb)
        ks = k_ref[pl.ds(start, tb), :]
        vs = v_ref[pl.ds(start, tb), :]
        z = _dot_nt(q2x, with_ones(ks))
        sp = softplus(z)
        lb = z - sp
        if masked:
            sp = jnp.where(valid, sp, 0.0)
        later = _dot(sp.astype(BF16), later_keys)
        run = run_ref[...]
        att = jnp.concatenate(
            [jnp.exp(lb[:, c * LANES:(c + 1) * LANES] - later[:, c * LANES:(c + 1) * LANES] - run)
             for c in range(tb // LANES)], axis=1)
        if masked:
            att = jnp.where(valid, att, 0.0)
        acc_ref[...] += _dot(att.astype(BF16), vs)
        run_ref[...] = run + (later[:, 0:1] + sp[:, 0:1])

    def stage1(jhi, slot):
        start = pl.multiple_of((jhi - (unit - 1)) * tb, tb)
        z = _dot_nt(q2x, with_ones(k_ref[pl.ds(start, unit * tb), :]))
        sp = softplus(z)
        lb_ref[slot] = z - sp
        sp_ref[slot] = sp.astype(BF16)

    def stage2(jhi, slot):
        start = pl.multiple_of((jhi - (unit - 1)) * tb, tb)
        sp16 = sp_ref[slot]
        cols = [slice((unit - 1 - s) * tb, (unit - s) * tb) for s in range(unit)]
        later = _dot(jnp.concatenate([sp16[:, cs] for cs in cols], axis=0), later_keys)
        run = run_ref[...]
        att = [None] * unit
        for s, cs in enumerate(cols):
            lat = later[s * 2 * tb:(s + 1) * 2 * tb]
            att[unit - 1 - s] = jnp.concatenate(
                [jnp.exp(lb_ref[slot, :, cs.start + c * LANES:cs.start + (c + 1) * LANES]
                         - lat[:, c * LANES:(c + 1) * LANES] - run) for c in range(tb // LANES)], axis=1)
            run = run + (lat[:, 0:1] + sp16[:, cs.start:cs.start + 1].astype(F32))
        run_ref[...] = run
        acc_ref[...] += _dot(jnp.concatenate(att, axis=1).astype(BF16), v_ref[pl.ds(start, unit * tb), :])

    left = i % unit
    n_units = i // unit
    top = i - 1 - left
    lone = n_units & 1
    first = top - lone * unit
    n_pairs = n_units >> 1

    @pl.when(n_pairs >= 1)
    def _():
        stage1(first, 0)
        block(i, True)

    @pl.when(n_pairs < 1)
    def _():
        block(i, True)

    for r in range(unit - 1):
        @pl.when(left > r)
        def _(r=r):
            block(i - 1 - r, False)

    @pl.when(lone == 1)
    def _():
        stage1(top, 1)
        stage2(top, 1)

    @pl.when(n_pairs >= 1)
    def _():
        def body(u, carry):
            jhi = first - 2 * u * unit
            stage2(jhi, 0)
            stage1(jhi - unit, 1)
            stage2(jhi - unit, 1)
            stage1(jhi - 2 * unit, 0)
            return carry

        lax.fori_loop(0, n_pairs - 1, body, 0)
        jhi = first - 2 * (n_pairs - 1) * unit
        stage2(jhi, 0)
        stage1(jhi - unit, 1)
        stage2(jhi - unit, 1)

    o = jnp.where(lo, acc_ref[0:tb], acc_ref[tb:2 * tb])
    o_ref[...] = (_half_lane_rms(o, lo) * g_ref[...]).astype(o_ref.dtype)


def _sb_prompt(qb, kb, vb, bias, g2, *, tb, unit, bounded):
    b, t, w = qb.shape
    pairs = w // LANES
    return pl.pallas_call(
        functools.partial(_sb_prompt_kernel, tb=tb, unit=unit, bounded=bounded),
        grid_spec=pltpu.PrefetchScalarGridSpec(
            num_scalar_prefetch=0,
            grid=(b, pairs, t // tb),
            in_specs=[pl.BlockSpec(memory_space=pltpu.SMEM),
                      pl.BlockSpec((None, tb, LANES), lambda i, p, j: (i, j, p)),
                      pl.BlockSpec((None, t, LANES), lambda i, p, j: (i, 0, p)),
                      pl.BlockSpec((None, t, LANES), lambda i, p, j: (i, 0, p)),
                      pl.BlockSpec((1, LANES), lambda i, p, j: (0, 0))],
            out_specs=pl.BlockSpec((None, tb, LANES), lambda i, p, j: (i, j, p)),
            scratch_shapes=[pltpu.VMEM((2 * tb, LANES), F32), pltpu.VMEM((2 * tb, LANES), F32),
                            pltpu.VMEM((2, 2 * tb, unit * tb), BF16), pltpu.VMEM((2, 2 * tb, unit * tb), F32)]),
        out_shape=jax.ShapeDtypeStruct((b, t, w), BF16),
        compiler_params=_cparams(("parallel", "parallel", "arbitrary")),
        name="sb_attn_prompt_bounded" if bounded else "sb_attn_prompt",
    )(bias, qb, kb, vb, g2)


def _page_copy(pt_ref, pool_ref, buf_ref, sem_ref, seq, slot, pg):
    return pltpu.make_async_copy(pool_ref.at[pt_ref[seq, pg]], buf_ref.at[slot, pg], sem_ref.at[slot])


def _sb_sample_kernel(pt_ref, bias_ref, q_ref, kn_ref, vn_ref, g_ref, kpool_ref, vpool_ref, o_ref,
                      kbuf_ref, vbuf_ref, ksem_ref, vsem_ref, newk_ref, newv_ref, *, n_pages, n_q):
    seq = pl.program_id(0)
    n_seq = pl.num_programs(0)
    slot = seq & 1
    rows = SB_HEADS * n_q
    page = kbuf_ref.shape[3]
    width = SB_WIDTH
    pools = ((kpool_ref, kbuf_ref, ksem_ref), (vpool_ref, vbuf_ref, vsem_ref))

    @pl.when(seq == 0)
    def _():
        for pool, buf, sem in pools:
            for pg in range(n_pages):
                _page_copy(pt_ref, pool, buf, sem, 0, 0, pg).start()

    @pl.when(seq + 1 < n_seq)
    def _():
        for pool, buf, sem in pools:
            for pg in range(n_pages):
                _page_copy(pt_ref, pool, buf, sem, seq + 1, 1 - slot, pg).start()

    r1 = lax.broadcasted_iota(jnp.int32, (rows, 1), 0)
    later_keys = jnp.where(lax.broadcasted_iota(jnp.int32, (page, page), 0)
                           > lax.broadcasted_iota(jnp.int32, (page, page), 1), 1.0, 0.0).astype(BF16)
    head_of_lane = lax.broadcasted_iota(jnp.int32, (rows, width), 1) // SB_DIM
    head_of_row = lax.broadcasted_iota(jnp.int32, (rows, width), 0) // n_q
    qrep = jnp.concatenate([q_ref[...].astype(F32)] * SB_HEADS, axis=0)
    qx = jnp.where(head_of_lane == head_of_row, qrep, 0.0).astype(BF16)
    bias = bias_ref[...]

    newk_ref[...] = jnp.zeros_like(newk_ref)
    newv_ref[...] = jnp.zeros_like(newv_ref)
    newk_ref[0:n_q, :] = kn_ref[...]
    newv_ref[0:n_q, :] = vn_ref[...]
    z = _dot_nt(qx, newk_ref[...].astype(BF16)) + bias
    sp, lb = _softplus_parts(z)
    valid = lax.broadcasted_iota(jnp.int32, (rows, page), 1) < (r1 & (n_q - 1))
    sp = jnp.where(valid, sp, 0.0)
    att = jnp.where(valid, jnp.exp(lb - _dot(sp.astype(BF16), later_keys)), 0.0)
    acc = _dot(att.astype(BF16), newv_ref[...].astype(BF16))
    run = jnp.sum(sp, axis=-1, keepdims=True)

    for pool, buf, sem in pools:
        for pg in range(n_pages):
            _page_copy(pt_ref, pool, buf, sem, seq, slot, pg).wait()

    order = [n_pages - 1 - r for r in range(n_pages)]
    k_all = jnp.concatenate([kbuf_ref[slot, pg].astype(BF16) for pg in order], axis=1)
    z = _dot(qx, k_all) + bias
    sp, lb = _softplus_parts(z)
    sp16 = sp.astype(BF16)
    later = _dot(jnp.concatenate([sp16[:, r * page:(r + 1) * page] for r in range(n_pages)], axis=0),
                 later_keys)
    args = []
    for r in range(n_pages):
        cs = slice(r * page, (r + 1) * page)
        args.append(lb[:, cs] - later[r * rows:(r + 1) * rows] - run)
        run = run + jnp.sum(sp[:, cs], axis=-1, keepdims=True)
    att = jnp.exp(jnp.concatenate(args, axis=1)).astype(BF16)
    v_all = jnp.concatenate([vbuf_ref[slot, pg].astype(BF16) for pg in order], axis=1)
    acc = acc + _dot_nt(att, v_all)

    lane_head = lax.broadcasted_iota(jnp.int32, (n_q, width), 1) // SB_DIM
    o = jnp.zeros((n_q, width), F32)
    for hh in range(SB_HEADS):
        o = o + jnp.where(lane_head == hh, acc[hh * n_q:(hh + 1) * n_q, :], 0.0)
    lo = lax.broadcasted_iota(jnp.int32, (1, LANES), 1) < SB_DIM
    for c in range(width // LANES):
        sl = slice(c * LANES, (c + 1) * LANES)
        o_ref[:, sl] = _half_lane_rms(o[:, sl], lo) * g_ref[:, sl]


def _sb_sample(page_table, bias_rows, q3, kn3, vn3, g_sb, cache_k, cache_v):
    bs, n_q, w = q3.shape
    n_pages = page_table.shape[1]
    page = cache_k.shape[2]
    rows = SB_HEADS * n_q
    tok = lambda: pl.BlockSpec((None, n_q, w), lambda i, pt: (i, 0, 0))
    const = lambda shp: pl.BlockSpec(shp, lambda i, pt: (0,) * len(shp))
    hbm = lambda: pl.BlockSpec(memory_space=pl.ANY)
    return pl.pallas_call(
        functools.partial(_sb_sample_kernel, n_pages=n_pages, n_q=n_q),
        grid_spec=pltpu.PrefetchScalarGridSpec(
            num_scalar_prefetch=1,
            grid=(bs,),
            in_specs=[const((rows, 1)), tok(), tok(), tok(), const((1, w)), hbm(), hbm()],
            out_specs=tok(),
            scratch_shapes=[pltpu.VMEM((2, n_pages, w, page), F32), pltpu.VMEM((2, n_pages, w, page), F32),
                            pltpu.SemaphoreType.DMA((2,)), pltpu.SemaphoreType.DMA((2,)),
                            pltpu.VMEM((page, w), F32), pltpu.VMEM((page, w), F32)]),
        out_shape=jax.ShapeDtypeStruct((bs, n_q, w), F32),
        compiler_params=_cparams(("arbitrary",)),
        name="sb_attn_paged",
    )(page_table, bias_rows, q3, kn3, vn3, g_sb, cache_k, cache_v)


def _split3(x):
    hi = x.astype(BF16)
    r = x - hi.astype(F32)
    mid = r.astype(BF16)
    lo = (r - mid.astype(F32)).astype(BF16)
    return hi, mid, lo


def _dn_local(blocks, group):
    n = blocks[0][3].shape[0]
    units = [(bi, h) for bi in range(len(blocks)) for h in range(DN_HEADS)]
    hs = [slice(h * DN_DIM, (h + 1) * DN_DIM) for h in range(DN_HEADS)]
    row = lax.broadcasted_iota(jnp.int32, (n, n), 0)
    col = lax.broadcasted_iota(jnp.int32, (n, n), 1)
    shift = group.bit_length() - 1
    same = jnp.where((row >> shift) == (col >> shift), 1.0, 0.0)
    low = jnp.where(row >= col, same, 0.0)
    strict = jnp.where(row > col, same, 0.0)
    low16 = low.astype(BF16)
    same16 = same.astype(BF16)
    gparts = [_split3(blk[3]) for blk in blocks]
    g_all = [sum(_dot(low16, m) for m in gp) for gp in gparts]
    t_all = [sum(_dot(same16, m) for m in gp) for gp in gparts]
    g_row = [jnp.broadcast_to(g_all[bi][:, DN_HEADS + h:DN_HEADS + h + 1], (n, n)) for bi, h in units]
    gtot = [jnp.broadcast_to(t_all[bi][:, DN_HEADS + h:DN_HEADS + h + 1], (n, n)) for bi, h in units]
    un = range(len(units))
    decay = [low * jnp.exp(jnp.minimum(g_row[u] - g_row[u].T, 0.0)) for u in un]
    beta = [blocks[bi][3][:, h:h + 1] for bi, h in units]
    k = [blocks[bi][1][:, hs[h]] for bi, h in units]
    k16 = [k[u].astype(BF16) for u in un]
    kb = [k[u] * beta[u] for u in un]
    a_mat = [strict * (_dot_nt(kb[u].astype(BF16), k16[u]) * decay[u]) for u in un]
    eye = jnp.where(row == col, 1.0, 0.0)
    inv = [eye - a_mat[u] * jnp.where((row >> 1) == (col >> 1), 1.0, 0.0) for u in un]
    for lb in range(1, shift):
        sel = jnp.where((row >> (lb + 1)) == (col >> (lb + 1)),
                        jnp.where(((row >> lb) & 1) > ((col >> lb) & 1), 1.0, 0.0), 0.0)
        inv16 = [inv[u].astype(BF16) for u in un]
        x = [_dot((a_mat[u] * sel).astype(BF16), inv16[u]) for u in un]
        inv = [inv[u] - _dot(inv16[u], x[u].astype(BF16)) for u in un]
    eg = [jnp.exp(g_row[u]) for u in un]
    v = [blocks[bi][2][:, hs[h]] for bi, h in units]
    rhs = [jnp.concatenate([v[u] * beta[u], kb[u] * eg[u]], axis=-1).astype(BF16) for u in un]
    sol = [_dot(inv[u].astype(BF16), rhs[u]) for u in un]
    q = [blocks[bi][0][:, hs[h]] for bi, h in units]
    intra = [_dot_nt(q[u].astype(BF16), k16[u]) * decay[u] for u in un]
    usol = [sol[u][:, :DN_DIM] for u in un]
    kcum = [sol[u][:, DN_DIM:] for u in un]
    qdec = [q[u] * eg[u] for u in un]
    kdec = [k[u] * jnp.exp(gtot[u] - g_row[u]) for u in un]
    return usol, kcum, intra, qdec, kdec, gtot


def _dn_seq_kernel(q_ref, k_ref, v_ref, gate_ref, o_ref, sout_ref, s_ref, *, group):
    @pl.when(pl.program_id(0) == 0)
    def _():
        s_ref[...] = jnp.zeros_like(s_ref)

    nb, n = q_ref.shape[0], q_ref.shape[1]
    blocks = [(q_ref.at[bi], k_ref.at[bi], v_ref.at[bi], gate_ref[bi]) for bi in range(nb)]
    usol, kcum, intra, qdec, kdec, gtot = _dn_local(blocks, group)
    units = [(bi, h) for bi in range(nb) for h in range(DN_HEADS)]
    un = range(len(units))
    s = [s_ref[bi, h] for bi, h in units]
    vnews = [[] for _ in un]
    outs = [[] for _ in un]
    for c in range(n // group):
        rs = slice(c * group, (c + 1) * group)
        s16 = [s[u].astype(BF16) for u in un]
        lhs = [jnp.concatenate([kcum[u][rs], qdec[u][rs]], axis=0).astype(BF16) for u in un]
        both = [_dot(lhs[u], s16[u]) for u in un]
        vnew = [usol[u][rs] - both[u][:group] for u in un]
        upd = [_dot_tn(kdec[u][rs].astype(BF16), vnew[u].astype(BF16)) for u in un]
        s = [s[u] * jnp.exp(gtot[u][c * group:c * group + 1, :]) + upd[u] for u in un]
        for u in un:
            vnews[u].append(vnew[u])
            outs[u].append(both[u][group:])
    for u, (bi, h) in enumerate(units):
        s_ref[bi, h] = s[u]
        vfull = jnp.concatenate(vnews[u], axis=0).astype(BF16)
        o_ref[bi, :, h * DN_DIM:(h + 1) * DN_DIM] = (jnp.concatenate(outs[u], axis=0)
                                                     + _dot(intra[u].astype(BF16), vfull))

    @pl.when(pl.program_id(0) == pl.num_programs(0) - 1)
    def _():
        sout_ref[...] = s_ref[...]


def _dn_seq(qd, kd, vd, gates):
    b, t, w = qd.shape
    n = DN_BLOCK
    row = lambda c: pl.BlockSpec((b, n, c), lambda j: (0, j, 0))
    state = (b, DN_HEADS, DN_DIM, DN_DIM)
    return pl.pallas_call(
        functools.partial(_dn_seq_kernel, group=min(64, n)),
        grid=(t // n,),
        in_specs=[row(w), row(w), row(w), row(LANES)],
        out_specs=[row(w), pl.BlockSpec(state, lambda j: (0, 0, 0, 0))],
        out_shape=[jax.ShapeDtypeStruct((b, t, w), F32), jax.ShapeDtypeStruct(state, F32)],
        scratch_shapes=[pltpu.VMEM(state, F32)],
        compiler_params=_cparams(("arbitrary",)),
        name="deltanet_seq",
    )(qd, kd, vd, gates)


def _dn_grouped_kernel(q_ref, k_ref, v_ref, gate_ref, s0_ref, o_ref, sout_ref,
                       u_ref, kc_ref, qd_ref, kd_ref, gt_ref, vn_ref, oq_ref, *, group):
    n = q_ref.shape[0]
    heads = range(DN_HEADS)
    row1 = lax.broadcasted_iota(jnp.int32, (n, 1), 0)
    shift = group.bit_length() - 1
    u, kcum, intra, qdec, kdec, gtot = _dn_local([(q_ref, k_ref, v_ref, gate_ref[...])], group)
    for h in heads:
        u_ref[h] = u[h]
        kc_ref[h] = kcum[h]
        qd_ref[h] = qdec[h]
        kd_ref[h] = kdec[h]
        gt_ref[h] = gtot[h]
    vn_ref[...] = jnp.zeros_like(vn_ref)

    def per_seq(bi, carry):
        rs = pl.ds(pl.multiple_of(bi * group, group), group)
        mine = (row1 >> shift) == bi
        s = [s0_ref[bi, h] for h in heads]
        lhs = [jnp.concatenate([kc_ref[h, rs, :], qd_ref[h, rs, :]], axis=0).astype(BF16) for h in heads]
        both = [_dot(lhs[h], s[h].astype(BF16)) for h in heads]
        for h in heads:
            oq_ref[h, rs, :] = both[h][group:]
            vn_ref[h, rs, :] = u_ref[h, rs, :] - both[h][:group]
        kmask = [jnp.where(mine, kd_ref[h], 0.0).astype(BF16) for h in heads]
        upd = [_dot_tn(kmask[h], vn_ref[h].astype(BF16)) for h in heads]
        for h in heads:
            sout_ref[bi, h] = s[h] * jnp.exp(gt_ref[h, rs, :][0:1, :]) + upd[h]
        return carry

    lax.fori_loop(0, n // group, per_seq, 0)
    for h in heads:
        o_ref[:, h * DN_DIM:(h + 1) * DN_DIM] = (oq_ref[h]
                                                 + _dot(intra[h].astype(BF16), vn_ref[h].astype(BF16)))


def _dn_grouped(qd, kd, vd, gates, s0, *, group):
    nrows, w = qd.shape
    n = DN_BLOCK
    per = n // group
    row = lambda c: pl.BlockSpec((n, c), lambda i: (i, 0))
    st = pl.BlockSpec((per, DN_HEADS, DN_DIM, DN_DIM), lambda i: (i, 0, 0, 0))
    return pl.pallas_call(
        functools.partial(_dn_grouped_kernel, group=group),
        grid=(nrows // n,),
        in_specs=[row(w), row(w), row(w), row(LANES), st],
        out_specs=[row(w), st],
        out_shape=[jax.ShapeDtypeStruct((nrows, w), F32), jax.ShapeDtypeStruct(s0.shape, F32)],
        scratch_shapes=[pltpu.VMEM((DN_HEADS, n, DN_DIM), F32)] * 7,
        compiler_params=_cparams(("parallel",)),
        name="deltanet_grouped",
    )(qd, kd, vd, gates, s0)


def _outproj_kernel(osb_ref, odn_ref, zg_ref, gdn_ref, x_ref, gt_ref, sc_ref, sh_ref, gn_ref, w_ref,
                    x1_ref, h2_ref):
    parts = [osb_ref[...].astype(BF16)]
    for hh in range(DN_HEADS):
        sl = slice(hh * DN_DIM, (hh + 1) * DN_DIM)
        parts.append((_rms_rows(odn_ref[:, sl], gdn_ref[...]) * zg_ref[:, sl]).astype(BF16))
    mix = _dot(jnp.concatenate(parts, axis=1), w_ref[...])
    x1 = x_ref[...] + gt_ref[...] * mix
    x1_ref[...] = x1
    h2_ref[...] = (_rms_rows(x1, gn_ref[...]) * (1.0 + sc_ref[...]) + sh_ref[...]).astype(BF16)


def _mod_specs(x, tm, per_row):
    d = x.shape[-1]
    if per_row:
        n = x.shape[0]
        row = lambda c: pl.BlockSpec((tm, c), lambda i: (i, 0))
        return (n // tm,), row, row(d), (n,), ("arbitrary",)
    b, t, _ = x.shape
    row = lambda c: pl.BlockSpec((None, tm, c), lambda i, j: (i, j, 0))
    per_b = pl.BlockSpec((None, 1, d), lambda i, j: (i, 0, 0))
    return (b, t // tm), row, per_b, (b, t), ("parallel", "arbitrary")


def _outproj(osb, odn, zg, gdn, x, gt, sc, sh, gn, w, *, tm, per_row):
    d = x.shape[-1]
    grid, row, mod, lead, sem = _mod_specs(x, tm, per_row)
    return pl.pallas_call(
        _outproj_kernel,
        grid=grid,
        in_specs=[row(SB_WIDTH), row(DN_WIDTH), row(DN_WIDTH), _resident(gdn.shape), row(d), mod, mod, mod,
                  _resident(gn.shape), _resident(w.shape)],
        out_specs=[row(d), row(d)],
        out_shape=[jax.ShapeDtypeStruct(lead + (d,), F32), jax.ShapeDtypeStruct(lead + (d,), BF16)],
        compiler_params=_cparams(sem),
        name="outproj_rows" if per_row else "outproj_seq",
    )(osb, odn, zg, gdn, x, gt, sc, sh, gn, w)


def _ffn_kernel(*refs, grouped, cw):
    if grouped:
        h_ref, x1_ref, gt_ref, wu_ref, wc_ref, wd_ref, buf_ref, y_ref, nbuf_ref, act_ref = refs
        carry_ref = None
    else:
        h_ref, x1_ref, gt_ref, wu_ref, wc_ref, wd_ref, y_ref, nbuf_ref, act_ref, carry_ref = refs
        buf_ref = None

        @pl.when(pl.program_id(1) == 0)
        def _():
            carry_ref[...] = jnp.zeros_like(carry_ref)

    hb = h_ref[...]
    tm = hb.shape[0]
    dff = wd_ref.shape[0]
    taps = wc_ref.shape[0]
    for c in range(dff // cw):
        halves = []
        for base in (0, dff):
            cs = slice(base + c * cw, base + (c + 1) * cw)
            cur = _dot(hb, wu_ref[:, cs])
            if grouped:
                y = _causal_conv(cur, wc_ref[:, cs], None, True, buf_ref[:, cs])
                nbuf_ref[:, cs] = pltpu.roll(cur, tm - (SUBLANES - (taps - 1)), 0)
            else:
                y = _causal_conv(cur, wc_ref[:, cs], carry_ref[:, cs], False, None)
                carry_ref[:, cs] = cur[tm - SUBLANES:tm]
                nbuf_ref[:, cs] = cur[tm - SUBLANES:tm]
            halves.append(y)
        act_ref[:, c * cw:(c + 1) * cw] = (_silu(halves[1]) * halves[0]).astype(BF16)
    y_ref[...] = x1_ref[...] + gt_ref[...] * _dot(act_ref[...], wd_ref[...])


def _ffn(h2, x1, gt, wu, wc, wd, bufp, *, tm, per_row, cw=256):
    d = x1.shape[-1]
    up = wu.shape[1]
    grid, row, mod, lead, sem = _mod_specs(x1, tm, per_row)
    in_specs = [row(d), row(d), mod, _resident(wu.shape), _resident(wc.shape), _resident(wd.shape)]
    args = [h2, x1, gt, wu, wc, wd]
    if per_row:
        in_specs.append(row(up))
        args.append(bufp)
        nb_shape, nb_spec, scratch = lead + (up,), row(up), [pltpu.VMEM((tm, up // 2), BF16)]
    else:
        nb_shape = (lead[0], SUBLANES, up)
        nb_spec = pl.BlockSpec((None, SUBLANES, up), lambda i, j: (i, 0, 0))
        scratch = [pltpu.VMEM((tm, up // 2), BF16), pltpu.VMEM((SUBLANES, up), F32)]
    return pl.pallas_call(
        functools.partial(_ffn_kernel, grouped=per_row, cw=cw),
        grid=grid, in_specs=in_specs,
        out_specs=[row(d), nb_spec],
        out_shape=[jax.ShapeDtypeStruct(lead + (d,), F32), jax.ShapeDtypeStruct(nb_shape, F32)],
        scratch_shapes=scratch,
        compiler_params=_cparams(sem),
        name="convffn_rows" if per_row else "convffn_seq",
    )(*args)


def _prep_weights(g_attn_norm, w_in, g_q, g_k, sb_bias, g_sb_out, w_dn_conv, a_log, dt_bias, g_dn_out,
                  w_out, g_ffn_norm, w_up, w_ffn_conv, w_down):
    d = w_in.shape[0]
    o1 = 3 * SB_WIDTH
    o2 = o1 + 3 * DN_WIDTH
    o3 = o2 + 2 * DN_HEADS
    pad = jnp.zeros((d, LANES - 2 * DN_HEADS), w_in.dtype)
    w1 = jnp.concatenate([w_in[:, :o2], w_in[:, o3:], w_in[:, o2:o3], pad], axis=1).astype(BF16)
    lane_pad = lambda v: jnp.zeros((1, LANES), F32).at[0, DN_HEADS:2 * DN_HEADS].set(v)
    return dict(
        gn1=g_attn_norm.reshape(1, d), w1=w1,
        gq=jnp.tile(g_q, SB_HEADS).reshape(1, SB_WIDTH), gk=jnp.tile(g_k, SB_HEADS).reshape(1, SB_WIDTH),
        bias=sb_bias.astype(F32), g_sb=jnp.tile(g_sb_out, SB_HEADS).reshape(1, SB_WIDTH),
        wc_dn=w_dn_conv, al=lane_pad(a_log), dt=lane_pad(dt_bias), gdn=g_dn_out.reshape(1, DN_DIM),
        w_out=w_out.astype(BF16), gn2=g_ffn_norm.reshape(1, d), w_up=w_up.astype(BF16),
        wc_ffn=w_ffn_conv, w_down=w_down.astype(BF16))


def _layer_prompt(x, mod, p):
    b, t, d = x.shape
    sh1, sc1, gt1, sh2, sc2, gt2 = [m.reshape(b, 1, d) for m in jnp.split(mod, 6, axis=-1)]
    qb, kb, vb, kf, vf, qd, kd, vd, zg, gates, dnbuf = _inproj(
        x, sc1, sh1, p["gn1"], p["w1"], p["gq"], p["gk"], p["wc_dn"], p["al"], p["dt"], None, tm=512)
    z_bound = 8.1 * jnp.max(jnp.abs(p["gq"])) * jnp.max(jnp.abs(p["gk"])) + jnp.max(jnp.abs(p["bias"]))
    attend = lambda bounded: functools.partial(_sb_prompt, tb=256, unit=1, bounded=bounded)
    osb = lax.cond(z_bound < 80.0, attend(True), attend(False),
                   qb, kb, vb, p["bias"], p["g_sb"][:, :LANES])
    odn, s_new = _dn_seq(qd, kd, vd, gates)
    x1, h2 = _outproj(osb, odn, zg, p["gdn"], x, gt1, sc2, sh2, p["gn2"], p["w_out"], tm=512, per_row=False)
    y, ffbuf = _ffn(h2, x1, gt2, p["w_up"], p["wc_ffn"], p["w_down"], None, tm=512, per_row=False)
    k_dn = p["wc_dn"].shape[0] - 1
    k_ff = p["wc_ffn"].shape[0] - 1
    return (y, kf.reshape(b, t, SB_HEADS, SB_DIM), vf.reshape(b, t, SB_HEADS, SB_DIM), s_new,
            dnbuf[:, SUBLANES - k_dn:], ffbuf[:, SUBLANES - k_ff:])


def _layer_sample(x, mod, page_table, cache_k, cache_v, s0, dn_buf, ffn_buf, p):
    bs, n_q, d = x.shape
    assert n_q == SUBLANES
    n = bs * n_q
    rep = lambda m: jnp.broadcast_to(m[:, None, :], (bs, n_q, d)).reshape(n, d)
    sh1, sc1, gt1, sh2, sc2, gt2 = [rep(m) for m in jnp.split(mod, 6, axis=-1)]
    padrows = lambda buf: jnp.pad(buf, ((0, 0), (0, n_q - buf.shape[1]), (0, 0))).reshape(n, buf.shape[2])
    xf = x.reshape(n, d)
    qb, kb, vb, kf, vf, qd, kd, vd, zg, gates, dnbuf = _inproj(
        xf, sc1, sh1, p["gn1"], p["w1"], p["gq"], p["gk"], p["wc_dn"], p["al"], p["dt"], padrows(dn_buf), tm=256)
    del kb, vb
    w = SB_WIDTH
    bias_rows = jnp.repeat(p["bias"], n_q).reshape(SB_HEADS * n_q, 1)
    n_phys, page = cache_k.shape[0], cache_k.shape[1]
    pool_t = lambda c: jnp.transpose(c, (0, 2, 3, 1)).reshape(n_phys, w, page)
    osb = _sb_sample(page_table, bias_rows, qb.reshape(bs, n_q, w), kf.reshape(bs, n_q, w),
                     vf.reshape(bs, n_q, w), p["g_sb"], pool_t(cache_k), pool_t(cache_v))
    odn, s_new = _dn_grouped(qd, kd, vd, gates, s0, group=n_q)
    x1, h2 = _outproj(osb.reshape(n, w), odn, zg, p["gdn"], xf, gt1, sc2, sh2, p["gn2"], p["w_out"],
                      tm=256, per_row=True)
    y, ffbuf = _ffn(h2, x1, gt2, p["w_up"], p["wc_ffn"], p["w_down"], padrows(ffn_buf), tm=256, per_row=True)
    k_dn = p["wc_dn"].shape[0] - 1
    k_ff = p["wc_ffn"].shape[0] - 1
    return (y.reshape(bs, n_q, d), kf.reshape(bs, n_q, SB_HEADS, SB_DIM), vf.reshape(bs, n_q, SB_HEADS, SB_DIM),
            s_new, dnbuf.reshape(bs, n_q, -1)[:, :k_dn], ffbuf.reshape(bs, n_q, -1)[:, :k_ff])


def kernel(x_prompt, x_sample, c_prompt, c_sample, cache_k, cache_v, page_table, state_delta, state_dn_conv, state_ffn_conv, w_ada, b_ada, g_attn_norm, w_in, g_q, g_k, sb_bias, g_sb_out, w_dn_conv, a_log, dt_bias, g_dn_out, w_out, g_ffn_norm, w_up, w_ffn_conv, w_down):
    depth = w_ada.shape[0]
    bp = x_prompt.shape[0]
    yp, ys = x_prompt, x_sample
    outs = [[] for _ in range(10)]
    c_all = jnp.concatenate([c_prompt, c_sample], axis=0)
    pad_rows = (-c_all.shape[0]) % SUBLANES
    c_all = jnp.pad(c_all, ((0, pad_rows), (0, 0)))
    for l in range(depth):
        p = _prep_weights(g_attn_norm[l], w_in[l], g_q[l], g_k[l], sb_bias[l], g_sb_out[l], w_dn_conv[l],
                          a_log[l], dt_bias[l], g_dn_out[l], w_out[l], g_ffn_norm[l], w_up[l],
                          w_ffn_conv[l], w_down[l])
        mod = _ada(c_all, w_ada[l], b_ada[l])
        yp, kp, vp, sp, dcp, fcp = _layer_prompt(yp, mod[:bp], p)
        ys, ks, vs, ss, dcs, fcs = _layer_sample(ys, mod[bp:bp + x_sample.shape[0]], page_table,
                                                 cache_k[l], cache_v[l], state_delta[l],
                                                 state_dn_conv[l], state_ffn_conv[l], p)
        for lst, val in zip(outs, (kp, vp, ks, vs, sp, ss, dcp, dcs, fcp, fcs)):
            lst.append(val)
    return (yp, ys) + tuple(jnp.stack(o) for o in outs)
```

```python
import functools

import jax
import jax.numpy as jnp
from jax import lax
from jax.experimental import pallas as pl
from jax.experimental.pallas import tpu as pltpu

F32 = jnp.float32
BF16 = jnp.bfloat16

NORM_EPS = 1e-6
LOG2E = 1.4426950408889634
LANES = 128
SUBLANES = 8
VMEM_LIMIT = 56 * 1024 * 1024

SB_HEADS = 8
SB_DIM = 64
SB_WIDTH = SB_HEADS * SB_DIM
DN_HEADS = 4
DN_DIM = 128
DN_WIDTH = DN_HEADS * DN_DIM
DN_BLOCK = 128


def _dot(a, b):
    return jnp.dot(a, b, preferred_element_type=F32)


def _dot_nt(a, b):
    return lax.dot_general(a, b, (((1,), (1,)), ((), ())), preferred_element_type=F32)


def _dot_tn(a, b):
    return lax.dot_general(a, b, (((0,), (0,)), ((), ())), preferred_element_type=F32)


def _silu(x):
    return x * jax.nn.sigmoid(x)


def _exp_neg_abs(x):
    return jnp.exp2(jnp.abs(x) * (-LOG2E))


def _softplus_parts(z):
    l = jnp.log(1.0 + jnp.exp(-jnp.abs(z)))
    return jnp.maximum(z, 0.0) + l, jnp.minimum(z, 0.0) - l


def _rms_rows(x, g):
    ms = jnp.mean(x * x, axis=-1, keepdims=True)
    return x * lax.rsqrt(ms + NORM_EPS) * g


def _half_lane_rms(blk, lo):
    sq = blk * blk
    s_lo = jnp.sum(jnp.where(lo, sq, 0.0), axis=-1, keepdims=True)
    s_hi = jnp.sum(jnp.where(lo, 0.0, sq), axis=-1, keepdims=True)
    ms = jnp.where(lo, s_lo, s_hi) * (1.0 / SB_DIM)
    return blk * lax.rsqrt(ms + NORM_EPS)


def _cparams(sem):
    return pltpu.CompilerParams(dimension_semantics=sem, vmem_limit_bytes=VMEM_LIMIT)


def _resident(shape):
    nd = len(shape)
    return pl.BlockSpec(shape, lambda *_: (0,) * nd, pipeline_mode=pl.Buffered(1))


def _ada_kernel(c_ref, w_ref, b_ref, o_ref):
    a = _silu(c_ref[...]).astype(BF16)
    o_ref[...] = _dot(a, w_ref[...].astype(BF16)) + b_ref[...]


def _ada(c, w, b):
    m, d = c.shape
    n = w.shape[1]
    tn = 512
    return pl.pallas_call(
        _ada_kernel,
        grid=(n // tn,),
        in_specs=[pl.BlockSpec((m, d), lambda j: (0, 0)),
                  pl.BlockSpec((d, tn), lambda j: (0, j)),
                  pl.BlockSpec((1, tn), lambda j: (0, j))],
        out_specs=pl.BlockSpec((m, tn), lambda j: (0, j)),
        out_shape=jax.ShapeDtypeStruct((m, n), F32),
        compiler_params=_cparams(("parallel",)),
        name="adaln_mod",
    )(c, w, b.reshape(1, n))


C_SB = 0
C_DN = 3 * SB_WIDTH
C_Z = C_DN + 3 * DN_WIDTH
C_G = C_Z + DN_WIDTH
C_END = C_G + LANES


def _causal_conv(cur, taps, prev_rows, grouped, bufp):
    w = taps.shape[0]
    tm = cur.shape[0]
    y = taps[w - 1:w, :] * cur
    if grouped:
        rowm = lax.broadcasted_iota(jnp.int32, (tm, 1), 0) & (SUBLANES - 1)
        for s in range(1, w):
            r = pltpu.roll(cur, s, 0)
            back = (w - 1) - s
            bs = bufp if back == 0 else pltpu.roll(bufp, tm - back, 0)
            y = y + taps[w - 1 - s:w - s, :] * jnp.where(rowm >= s, r, bs)
        return y
    row8 = lax.broadcasted_iota(jnp.int32, (SUBLANES, 1), 0)
    yh = taps[w - 1:w, :] * cur[0:SUBLANES]
    for s in range(1, w):
        r = pltpu.roll(cur, s, 0)
        y = y + taps[w - 1 - s:w - s, :] * r
        head = jnp.where(row8 < s, pltpu.roll(prev_rows, s, 0), r[0:SUBLANES])
        yh = yh + taps[w - 1 - s:w - s, :] * head
    return jnp.concatenate([yh, y[SUBLANES:]], axis=0)


def _inproj_kernel(*refs, grouped):
    if grouped:
        (x_ref, sc_ref, sh_ref, gn_ref, w_ref, gq_ref, gk_ref, wc_ref, al_ref, dt_ref, buf_ref,
         qb_ref, kb_ref, vb_ref, kf_ref, vf_ref, qd_ref, kd_ref, vd_ref, zg_ref, gate_ref, nbuf_ref) = refs
        carry_ref = None
    else:
        (x_ref, sc_ref, sh_ref, gn_ref, w_ref, gq_ref, gk_ref, wc_ref, al_ref, dt_ref,
         qb_ref, kb_ref, vb_ref, kf_ref, vf_ref, qd_ref, kd_ref, vd_ref, zg_ref, gate_ref, nbuf_ref,
         carry_ref) = refs
        buf_ref = None

        @pl.when(pl.program_id(1) == 0)
        def _():
            carry_ref[...] = jnp.zeros_like(carry_ref)

    x = x_ref[...]
    h = _rms_rows(x, gn_ref[...]) * (1.0 + sc_ref[...]) + sh_ref[...]
    hb = h.astype(BF16)
    tm = x.shape[0]
    lane = lax.broadcasted_iota(jnp.int32, (1, LANES), 1)
    lo = lane < SB_DIM

    for c in range(SB_WIDTH // LANES):
        sl = slice(c * LANES, (c + 1) * LANES)
        pq = _dot(hb, w_ref[:, C_SB + c * LANES:C_SB + (c + 1) * LANES])
        qn = _half_lane_rms(pq, lo) * gq_ref[:, sl]
        qb_ref[:, sl] = (qn * (SB_DIM ** -0.5)).astype(BF16)
        pk = _dot(hb, w_ref[:, C_SB + SB_WIDTH + c * LANES:C_SB + SB_WIDTH + (c + 1) * LANES])
        kn = _half_lane_rms(pk, lo) * gk_ref[:, sl]
        kf_ref[:, sl] = kn
        kb_ref[:, sl] = kn.astype(BF16)
    pv = _dot(hb, w_ref[:, C_SB + 2 * SB_WIDTH:C_SB + 3 * SB_WIDTH])
    vf_ref[...] = pv
    vb_ref[...] = pv.astype(BF16)

    cur = _dot(hb, w_ref[:, C_DN:C_Z])
    if grouped:
        y = _causal_conv(cur, wc_ref[...], None, True, buf_ref[...])
        nbuf_ref[...] = pltpu.roll(cur, tm - (SUBLANES - (wc_ref.shape[0] - 1)), 0)
    else:
        y = _causal_conv(cur, wc_ref[...], carry_ref[...], False, None)
        carry_ref[...] = cur[tm - SUBLANES:tm]
        nbuf_ref[...] = cur[tm - SUBLANES:tm]
    a = _silu(y)
    for hh in range(DN_HEADS):
        sl = slice(hh * DN_DIM, (hh + 1) * DN_DIM)
        qh = a[:, hh * DN_DIM:(hh + 1) * DN_DIM]
        kh = a[:, DN_WIDTH + hh * DN_DIM:DN_WIDTH + (hh + 1) * DN_DIM]
        qd_ref[:, sl] = qh * (lax.rsqrt(jnp.sum(qh * qh, axis=-1, keepdims=True) + NORM_EPS) * (DN_DIM ** -0.5))
        kd_ref[:, sl] = kh * lax.rsqrt(jnp.sum(kh * kh, axis=-1, keepdims=True) + NORM_EPS)
    vd_ref[...] = a[:, 2 * DN_WIDTH:3 * DN_WIDTH]

    zg_ref[...] = _silu(_dot(hb, w_ref[:, C_Z:C_G]))

    gb = _dot(hb, w_ref[:, C_G:C_END])
    sp, _ = _softplus_parts(gb + dt_ref[...])
    gate_ref[...] = jnp.where(lane < DN_HEADS, jax.nn.sigmoid(gb), -jnp.exp(al_ref[...]) * sp)


def _inproj(x, sc, sh, gn, w1, gq, gk, wc, al, dt, bufp, *, tm):
    grouped = bufp is not None
    d = x.shape[-1]
    if grouped:
        n = x.shape[0]
        grid = (n // tm,)
        row = lambda c: pl.BlockSpec((tm, c), lambda i: (i, 0))
        in_specs = [row(d), row(d), row(d)]
        lead = (n,)
        sem = ("arbitrary",)
        nb_shape, nb_spec = (n, 3 * DN_WIDTH), row(3 * DN_WIDTH)
        scratch = []
    else:
        b, t, _ = x.shape
        grid = (b, t // tm)
        row = lambda c: pl.BlockSpec((None, tm, c), lambda i, j: (i, j, 0))
        per_b = pl.BlockSpec((None, 1, d), lambda i, j: (i, 0, 0))
        in_specs = [row(d), per_b, per_b]
        lead = (b, t)
        sem = ("parallel", "arbitrary")
        nb_shape = (b, SUBLANES, 3 * DN_WIDTH)
        nb_spec = pl.BlockSpec((None, SUBLANES, 3 * DN_WIDTH), lambda i, j: (i, 0, 0))
        scratch = [pltpu.VMEM((SUBLANES, 3 * DN_WIDTH), F32)]
    in_specs += [_resident(gn.shape), _resident(w1.shape), _resident(gq.shape), _resident(gk.shape),
                 _resident(wc.shape), _resident(al.shape), _resident(dt.shape)]
    args = [x, sc, sh, gn, w1, gq, gk, wc, al, dt]
    if grouped:
        in_specs.append(row(3 * DN_WIDTH))
        args.append(bufp)
    out_cols = [(SB_WIDTH, BF16)] * 3 + [(SB_WIDTH, F32)] * 2 + [(DN_WIDTH, F32)] * 4 + [(LANES, F32)]
    out_shape = [jax.ShapeDtypeStruct(lead + (c,), dt_) for c, dt_ in out_cols]
    out_specs = [row(c) for c, _ in out_cols]
    out_shape.append(jax.ShapeDtypeStruct(nb_shape, F32))
    out_specs.append(nb_spec)
    return pl.pallas_call(
        functools.partial(_inproj_kernel, grouped=grouped),
        grid=grid, in_specs=in_specs, out_specs=out_specs, out_shape=out_shape,
        scratch_shapes=scratch,
        compiler_params=_cparams(sem),
        name="inproj_grouped" if grouped else "inproj_seq",
    )(*args)


def _sb_prompt_kernel(bias_ref, q_ref, k_ref, v_ref, g_ref, o_ref, acc_ref, run_ref, sp_ref, lb_ref,
                      *, tq, tb, unit, bounded):
    softplus = ((lambda z: jnp.log(1.0 + jnp.exp(z))) if bounded else
                (lambda z: jnp.maximum(z, 0.0) + jnp.log(1.0 + _exp_neg_abs(z))))
    p = pl.program_id(1)
    i = pl.program_id(2)
    lane = lax.broadcasted_iota(jnp.int32, (1, LANES), 1)
    lo = lane < SB_DIM
    q = q_ref[...]
    zero = jnp.zeros_like(q)
    q2 = jnp.concatenate([jnp.where(lo, q, zero), jnp.where(lo, zero, q)], axis=0)
    head_row = lax.broadcasted_iota(jnp.int32, (2 * tq, LANES), 0) < tq
    lane2 = lax.broadcasted_iota(jnp.int32, (2 * tq, LANES), 1)
    b_parts = [m.astype(F32) for m in _split3(jnp.where(head_row, bias_ref[2 * p], bias_ref[2 * p + 1]))]
    bias_lanes = jnp.where(lane2 == 0, b_parts[0],
                           jnp.where(lane2 == 1, b_parts[1], jnp.where(lane2 == 2, b_parts[2], 0.0)))
    q2x = jnp.concatenate([q2, bias_lanes.astype(BF16)], axis=1)
    ones_lanes = jnp.where(lax.broadcasted_iota(jnp.int32, (tb, LANES), 1) < 3, 1.0, 0.0).astype(BF16)

    def with_ones(ks):
        n = ks.shape[0] // tb
        ones = ones_lanes if n == 1 else jnp.concatenate([ones_lanes] * n, axis=0)
        return jnp.concatenate([ks, ones], axis=1)
    row = lax.broadcasted_iota(jnp.int32, (tb, tb), 0)
    col = lax.broadcasted_iota(jnp.int32, (tb, tb), 1)
    later_keys = jnp.where(row > col, 1.0, 0.0).astype(BF16)
    qrow = lax.broadcasted_iota(jnp.int32, (2 * tq, tb), 0) & (tq - 1)
    kcol = lax.broadcasted_iota(jnp.int32, (2 * tq, tb), 1)

    acc_ref[...] = jnp.zeros_like(acc_ref)
    run_ref[...] = jnp.zeros_like(run_ref)

    def block(j, masked):
        start = pl.multiple_of(j * tb, tb)
        ks = k_ref[pl.ds(start, tb), :]
        vs = v_ref[pl.ds(start, tb), :]
        z = _dot_nt(q2x, with_ones(ks))
        sp = softplus(z)
        lb = z - sp
        if masked:
            valid = kcol + (j * tb - i * tq) < qrow
            sp = jnp.where(valid, sp, 0.0)
        later = _dot(sp.astype(BF16), later_keys)
        run = run_ref[...]
        att = jnp.concatenate(
            [jnp.exp(lb[:, c * LANES:(c + 1) * LANES] - later[:, c * LANES:(c + 1) * LANES] - run)
             for c in range(tb // LANES)], axis=1)
        if masked:
            att = jnp.where(valid, att, 0.0)
        acc_ref[...] += _dot(att.astype(BF16), vs)
        run_ref[...] = run + (later[:, 0:1] + sp[:, 0:1])

    def stage1(jhi, slot):
        start = pl.multiple_of((jhi - (unit - 1)) * tb, tb)
        z = _dot_nt(q2x, with_ones(k_ref[pl.ds(start, unit * tb), :]))
        sp = softplus(z)
        lb_ref[slot] = z - sp
        sp_ref[slot] = sp.astype(BF16)

    def stage2(jhi, slot):
        start = pl.multiple_of((jhi - (unit - 1)) * tb, tb)
        sp16 = sp_ref[slot]
        cols = [slice((unit - 1 - s) * tb, (unit - s) * tb) for s in range(unit)]
        later = _dot(jnp.concatenate([sp16[:, cs] for cs in cols], axis=0), later_keys)
        run = run_ref[...]
        att = [None] * unit
        for s, cs in enumerate(cols):
            lat = later[s * 2 * tq:(s + 1) * 2 * tq]
            att[unit - 1 - s] = jnp.concatenate(
                [jnp.exp(lb_ref[slot, :, cs.start + c * LANES:cs.start + (c + 1) * LANES]
                         - lat[:, c * LANES:(c + 1) * LANES] - run) for c in range(tb // LANES)], axis=1)
            run = run + (lat[:, 0:1] + sp16[:, cs.start:cs.start + 1].astype(F32))
        run_ref[...] = run
        acc_ref[...] += _dot(jnp.concatenate(att, axis=1).astype(BF16), v_ref[pl.ds(start, unit * tb), :])

    n_diag = tq // tb
    n_full = n_diag * i
    left = n_full % unit
    n_units = n_full // unit
    top = n_full - 1 - left
    lone = n_units & 1
    first = top - lone * unit
    n_pairs = n_units >> 1

    def diagonal():
        for d in range(n_diag):
            block(n_full + n_diag - 1 - d, True)

    @pl.when(n_pairs >= 1)
    def _():
        stage1(first, 0)
        diagonal()

    @pl.when(n_pairs < 1)
    def _():
        diagonal()

    for r in range(unit - 1):
        @pl.when(left > r)
        def _(r=r):
            block(n_full - 1 - r, False)

    @pl.when(lone == 1)
    def _():
        stage1(top, 1)
        stage2(top, 1)

    @pl.when(n_pairs >= 1)
    def _():
        def body(u, carry):
            jhi = first - 2 * u * unit
            stage2(jhi, 0)
            stage1(jhi - unit, 1)
            stage2(jhi - unit, 1)
            stage1(jhi - 2 * unit, 0)
            return carry

        lax.fori_loop(0, n_pairs - 1, body, 0)
        jhi = first - 2 * (n_pairs - 1) * unit
        stage2(jhi, 0)
        stage1(jhi - unit, 1)
        stage2(jhi - unit, 1)

    o = jnp.where(lo, acc_ref[0:tq], acc_ref[tq:2 * tq])
    o_ref[...] = (_half_lane_rms(o, lo) * g_ref[...]).astype(o_ref.dtype)


def _sb_prompt(qb, kb, vb, bias, g2, *, tq, tb, unit, bounded):
    b, t, w = qb.shape
    pairs = w // LANES
    return pl.pallas_call(
        functools.partial(_sb_prompt_kernel, tq=tq, tb=tb, unit=unit, bounded=bounded),
        grid_spec=pltpu.PrefetchScalarGridSpec(
            num_scalar_prefetch=0,
            grid=(b, pairs, t // tq),
            in_specs=[pl.BlockSpec(memory_space=pltpu.SMEM),
                      pl.BlockSpec((None, tq, LANES), lambda i, p, j: (i, j, p)),
                      pl.BlockSpec((None, t, LANES), lambda i, p, j: (i, 0, p)),
                      pl.BlockSpec((None, t, LANES), lambda i, p, j: (i, 0, p)),
                      pl.BlockSpec((1, LANES), lambda i, p, j: (0, 0))],
            out_specs=pl.BlockSpec((None, tq, LANES), lambda i, p, j: (i, j, p)),
            scratch_shapes=[pltpu.VMEM((2 * tq, LANES), F32), pltpu.VMEM((2 * tq, LANES), F32),
                            pltpu.VMEM((2, 2 * tq, unit * tb), BF16), pltpu.VMEM((2, 2 * tq, unit * tb), F32)]),
        out_shape=jax.ShapeDtypeStruct((b, t, w), BF16),
        compiler_params=_cparams(("parallel", "parallel", "arbitrary")),
        name="sb_attn_prompt_bounded" if bounded else "sb_attn_prompt",
    )(bias, qb, kb, vb, g2)


def _page_copy(pt_ref, pool_ref, buf_ref, sem_ref, seq, slot, pg):
    return pltpu.make_async_copy(pool_ref.at[pt_ref[seq, pg]], buf_ref.at[slot, pg], sem_ref.at[slot])


def _sb_sample_kernel(pt_ref, bias_ref, q_ref, kn_ref, vn_ref, g_ref, kpool_ref, vpool_ref, o_ref,
                      kbuf_ref, vbuf_ref, ksem_ref, vsem_ref, newk_ref, newv_ref, *, n_pages, n_q):
    seq = pl.program_id(0)
    n_seq = pl.num_programs(0)
    slot = seq & 1
    rows = SB_HEADS * n_q
    page = kbuf_ref.shape[3]
    width = SB_WIDTH
    pools = ((kpool_ref, kbuf_ref, ksem_ref), (vpool_ref, vbuf_ref, vsem_ref))

    @pl.when(seq == 0)
    def _():
        for pool, buf, sem in pools:
            for pg in range(n_pages):
                _page_copy(pt_ref, pool, buf, sem, 0, 0, pg).start()

    @pl.when(seq + 1 < n_seq)
    def _():
        for pool, buf, sem in pools:
            for pg in range(n_pages):
                _page_copy(pt_ref, pool, buf, sem, seq + 1, 1 - slot, pg).start()

    r1 = lax.broadcasted_iota(jnp.int32, (rows, 1), 0)
    later_keys = jnp.where(lax.broadcasted_iota(jnp.int32, (page, page), 0)
                           > lax.broadcasted_iota(jnp.int32, (page, page), 1), 1.0, 0.0).astype(BF16)
    head_of_lane = lax.broadcasted_iota(jnp.int32, (rows, width), 1) // SB_DIM
    head_of_row = lax.broadcasted_iota(jnp.int32, (rows, width), 0) // n_q
    qrep = jnp.concatenate([q_ref[...].astype(F32)] * SB_HEADS, axis=0)
    qx = jnp.where(head_of_lane == head_of_row, qrep, 0.0).astype(BF16)
    bias = bias_ref[...]

    newk_ref[...] = jnp.zeros_like(newk_ref)
    newv_ref[...] = jnp.zeros_like(newv_ref)
    newk_ref[0:n_q, :] = kn_ref[...]
    newv_ref[0:n_q, :] = vn_ref[...]
    z = _dot_nt(qx, newk_ref[...].astype(BF16)) + bias
    sp, lb = _softplus_parts(z)
    valid = lax.broadcasted_iota(jnp.int32, (rows, page), 1) < (r1 & (n_q - 1))
    sp = jnp.where(valid, sp, 0.0)
    att = jnp.where(valid, jnp.exp(lb - _dot(sp.astype(BF16), later_keys)), 0.0)
    acc = _dot(att.astype(BF16), newv_ref[...].astype(BF16))
    run = jnp.sum(sp, axis=-1, keepdims=True)

    for pool, buf, sem in pools:
        for pg in range(n_pages):
            _page_copy(pt_ref, pool, buf, sem, seq, slot, pg).wait()

    order = [n_pages - 1 - r for r in range(n_pages)]
    k_all = jnp.concatenate([kbuf_ref[slot, pg].astype(BF16) for pg in order], axis=1)
    z = _dot(qx, k_all) + bias
    sp, lb = _softplus_parts(z)
    sp16 = sp.astype(BF16)
    later = _dot(jnp.concatenate([sp16[:, r * page:(r + 1) * page] for r in range(n_pages)], axis=0),
                 later_keys)
    args = []
    for r in range(n_pages):
        cs = slice(r * page, (r + 1) * page)
        args.append(lb[:, cs] - later[r * rows:(r + 1) * rows] - run)
        run = run + jnp.sum(sp[:, cs], axis=-1, keepdims=True)
    att = jnp.exp(jnp.concatenate(args, axis=1)).astype(BF16)
    v_all = jnp.concatenate([vbuf_ref[slot, pg].astype(BF16) for pg in order], axis=1)
    acc = acc + _dot_nt(att, v_all)

    lane_head = lax.broadcasted_iota(jnp.int32, (n_q, width), 1) // SB_DIM
    o = jnp.zeros((n_q, width), F32)
    for hh in range(SB_HEADS):
        o = o + jnp.where(lane_head == hh, acc[hh * n_q:(hh + 1) * n_q, :], 0.0)
    lo = lax.broadcasted_iota(jnp.int32, (1, LANES), 1) < SB_DIM
    for c in range(width // LANES):
        sl = slice(c * LANES, (c + 1) * LANES)
        o_ref[:, sl] = _half_lane_rms(o[:, sl], lo) * g_ref[:, sl]


def _sb_sample(page_table, bias_rows, q3, kn3, vn3, g_sb, cache_k, cache_v):
    bs, n_q, w = q3.shape
    n_pages = page_table.shape[1]
    page = cache_k.shape[2]
    rows = SB_HEADS * n_q
    tok = lambda: pl.BlockSpec((None, n_q, w), lambda i, pt: (i, 0, 0))
    const = lambda shp: pl.BlockSpec(shp, lambda i, pt: (0,) * len(shp))
    hbm = lambda: pl.BlockSpec(memory_space=pl.ANY)
    return pl.pallas_call(
        functools.partial(_sb_sample_kernel, n_pages=n_pages, n_q=n_q),
        grid_spec=pltpu.PrefetchScalarGridSpec(
            num_scalar_prefetch=1,
            grid=(bs,),
            in_specs=[const((rows, 1)), tok(), tok(), tok(), const((1, w)), hbm(), hbm()],
            out_specs=tok(),
            scratch_shapes=[pltpu.VMEM((2, n_pages, w, page), F32), pltpu.VMEM((2, n_pages, w, page), F32),
                            pltpu.SemaphoreType.DMA((2,)), pltpu.SemaphoreType.DMA((2,)),
                            pltpu.VMEM((page, w), F32), pltpu.VMEM((page, w), F32)]),
        out_shape=jax.ShapeDtypeStruct((bs, n_q, w), F32),
        compiler_params=_cparams(("arbitrary",)),
        name="sb_attn_paged",
    )(page_table, bias_rows, q3, kn3, vn3, g_sb, cache_k, cache_v)


def _split3(x):
    hi = x.astype(BF16)
    r = x - hi.astype(F32)
    mid = r.astype(BF16)
    lo = (r - mid.astype(F32)).astype(BF16)
    return hi, mid, lo


def _dn_local(blocks, group):
    n = blocks[0][3].shape[0]
    units = [(bi, h) for bi in range(len(blocks)) for h in range(DN_HEADS)]
    hs = [slice(h * DN_DIM, (h + 1) * DN_DIM) for h in range(DN_HEADS)]
    row = lax.broadcasted_iota(jnp.int32, (n, n), 0)
    col = lax.broadcasted_iota(jnp.int32, (n, n), 1)
    shift = group.bit_length() - 1
    same = jnp.where((row >> shift) == (col >> shift), 1.0, 0.0)
    low = jnp.where(row >= col, same, 0.0)
    strict = jnp.where(row > col, same, 0.0)
    low16 = low.astype(BF16)
    same16 = same.astype(BF16)
    gparts = [_split3(blk[3]) for blk in blocks]
    g_all = [sum(_dot(low16, m) for m in gp) for gp in gparts]
    t_all = [sum(_dot(same16, m) for m in gp) for gp in gparts]
    g_row = [jnp.broadcast_to(g_all[bi][:, DN_HEADS + h:DN_HEADS + h + 1], (n, n)) for bi, h in units]
    gtot = [jnp.broadcast_to(t_all[bi][:, DN_HEADS + h:DN_HEADS + h + 1], (n, n)) for bi, h in units]
    un = range(len(units))
    decay = [low * jnp.exp(jnp.minimum(g_row[u] - g_row[u].T, 0.0)) for u in un]
    beta = [blocks[bi][3][:, h:h + 1] for bi, h in units]
    k = [blocks[bi][1][:, hs[h]] for bi, h in units]
    k16 = [k[u].astype(BF16) for u in un]
    kb = [k[u] * beta[u] for u in un]
    a_mat = [strict * (_dot_nt(kb[u].astype(BF16), k16[u]) * decay[u]) for u in un]
    eye = jnp.where(row == col, 1.0, 0.0)
    inv = [eye - a_mat[u] * jnp.where((row >> 1) == (col >> 1), 1.0, 0.0) for u in un]
    for lb in range(1, shift):
        sel = jnp.where((row >> (lb + 1)) == (col >> (lb + 1)),
                        jnp.where(((row >> lb) & 1) > ((col >> lb) & 1), 1.0, 0.0), 0.0)
        inv16 = [inv[u].astype(BF16) for u in un]
        x = [_dot((a_mat[u] * sel).astype(BF16), inv16[u]) for u in un]
        inv = [inv[u] - _dot(inv16[u], x[u].astype(BF16)) for u in un]
    eg = [jnp.exp(g_row[u]) for u in un]
    v = [blocks[bi][2][:, hs[h]] for bi, h in units]
    rhs = [jnp.concatenate([v[u] * beta[u], kb[u] * eg[u]], axis=-1).astype(BF16) for u in un]
    sol = [_dot(inv[u].astype(BF16), rhs[u]) for u in un]
    q = [blocks[bi][0][:, hs[h]] for bi, h in units]
    intra = [_dot_nt(q[u].astype(BF16), k16[u]) * decay[u] for u in un]
    usol = [sol[u][:, :DN_DIM] for u in un]
    kcum = [sol[u][:, DN_DIM:] for u in un]
    qdec = [q[u] * eg[u] for u in un]
    kdec = [k[u] * jnp.exp(gtot[u] - g_row[u]) for u in un]
    return usol, kcum, intra, qdec, kdec, gtot


def _dn_seq_kernel(q_ref, k_ref, v_ref, gate_ref, o_ref, sout_ref, s_ref, *, group):
    @pl.when(pl.program_id(0) == 0)
    def _():
        s_ref[...] = jnp.zeros_like(s_ref)

    nb, n = q_ref.shape[0], q_ref.shape[1]
    blocks = [(q_ref.at[bi], k_ref.at[bi], v_ref.at[bi], gate_ref[bi]) for bi in range(nb)]
    usol, kcum, intra, qdec, kdec, gtot = _dn_local(blocks, group)
    units = [(bi, h) for bi in range(nb) for h in range(DN_HEADS)]
    un = range(len(units))
    s = [s_ref[bi, h] for bi, h in units]
    vnews = [[] for _ in un]
    outs = [[] for _ in un]
    for c in range(n // group):
        rs = slice(c * group, (c + 1) * group)
        s16 = [s[u].astype(BF16) for u in un]
        lhs = [jnp.concatenate([kcum[u][rs], qdec[u][rs]], axis=0).astype(BF16) for u in un]
        both = [_dot(lhs[u], s16[u]) for u in un]
        vnew = [usol[u][rs] - both[u][:group] for u in un]
        upd = [_dot_tn(kdec[u][rs].astype(BF16), vnew[u].astype(BF16)) for u in un]
        s = [s[u] * jnp.exp(gtot[u][c * group:c * group + 1, :]) + upd[u] for u in un]
        for u in un:
            vnews[u].append(vnew[u])
            outs[u].append(both[u][group:])
    for u, (bi, h) in enumerate(units):
        s_ref[bi, h] = s[u]
        vfull = jnp.concatenate(vnews[u], axis=0).astype(BF16)
        o_ref[bi, :, h * DN_DIM:(h + 1) * DN_DIM] = (jnp.concatenate(outs[u], axis=0)
                                                     + _dot(intra[u].astype(BF16), vfull))

    @pl.when(pl.program_id(0) == pl.num_programs(0) - 1)
    def _():
        sout_ref[...] = s_ref[...]


def _dn_seq(qd, kd, vd, gates):
    b, t, w = qd.shape
    n = DN_BLOCK
    row = lambda c: pl.BlockSpec((b, n, c), lambda j: (0, j, 0))
    state = (b, DN_HEADS, DN_DIM, DN_DIM)
    return pl.pallas_call(
        functools.partial(_dn_seq_kernel, group=min(64, n)),
        grid=(t // n,),
        in_specs=[row(w), row(w), row(w), row(LANES)],
        out_specs=[row(w), pl.BlockSpec(state, lambda j: (0, 0, 0, 0))],
        out_shape=[jax.ShapeDtypeStruct((b, t, w), F32), jax.ShapeDtypeStruct(state, F32)],
        scratch_shapes=[pltpu.VMEM(state, F32)],
        compiler_params=_cparams(("arbitrary",)),
        name="deltanet_seq",
    )(qd, kd, vd, gates)


def _dn_grouped_kernel(q_ref, k_ref, v_ref, gate_ref, s0_ref, o_ref, sout_ref,
                       u_ref, kc_ref, qd_ref, kd_ref, gt_ref, vn_ref, oq_ref, *, group):
    n = q_ref.shape[0]
    heads = range(DN_HEADS)
    row1 = lax.broadcasted_iota(jnp.int32, (n, 1), 0)
    shift = group.bit_length() - 1
    u, kcum, intra, qdec, kdec, gtot = _dn_local([(q_ref, k_ref, v_ref, gate_ref[...])], group)
    for h in heads:
        u_ref[h] = u[h]
        kc_ref[h] = kcum[h]
        qd_ref[h] = qdec[h]
        kd_ref[h] = kdec[h]
        gt_ref[h] = gtot[h]
    vn_ref[...] = jnp.zeros_like(vn_ref)

    def per_seq(bi, carry):
        rs = pl.ds(pl.multiple_of(bi * group, group), group)
        mine = (row1 >> shift) == bi
        s = [s0_ref[bi, h] for h in heads]
        lhs = [jnp.concatenate([kc_ref[h, rs, :], qd_ref[h, rs, :]], axis=0).astype(BF16) for h in heads]
        both = [_dot(lhs[h], s[h].astype(BF16)) for h in heads]
        for h in heads:
            oq_ref[h, rs, :] = both[h][group:]
            vn_ref[h, rs, :] = u_ref[h, rs, :] - both[h][:group]
        kmask = [jnp.where(mine, kd_ref[h], 0.0).astype(BF16) for h in heads]
        upd = [_dot_tn(kmask[h], vn_ref[h].astype(BF16)) for h in heads]
        for h in heads:
            sout_ref[bi, h] = s[h] * jnp.exp(gt_ref[h, rs, :][0:1, :]) + upd[h]
        return carry

    lax.fori_loop(0, n // group, per_seq, 0)
    for h in heads:
        o_ref[:, h * DN_DIM:(h + 1) * DN_DIM] = (oq_ref[h]
                                                 + _dot(intra[h].astype(BF16), vn_ref[h].astype(BF16)))


def _dn_grouped(qd, kd, vd, gates, s0, *, group):
    nrows, w = qd.shape
    n = DN_BLOCK
    per = n // group
    row = lambda c: pl.BlockSpec((n, c), lambda i: (i, 0))
    st = pl.BlockSpec((per, DN_HEADS, DN_DIM, DN_DIM), lambda i: (i, 0, 0, 0))
    return pl.pallas_call(
        functools.partial(_dn_grouped_kernel, group=group),
        grid=(nrows // n,),
        in_specs=[row(w), row(w), row(w), row(LANES), st],
        out_specs=[row(w), st],
        out_shape=[jax.ShapeDtypeStruct((nrows, w), F32), jax.ShapeDtypeStruct(s0.shape, F32)],
        scratch_shapes=[pltpu.VMEM((DN_HEADS, n, DN_DIM), F32)] * 7,
        compiler_params=_cparams(("parallel",)),
        name="deltanet_grouped",
    )(qd, kd, vd, gates, s0)


def _outproj_kernel(osb_ref, odn_ref, zg_ref, gdn_ref, x_ref, gt_ref, sc_ref, sh_ref, gn_ref, w_ref,
                    x1_ref, h2_ref):
    parts = [osb_ref[...].astype(BF16)]
    for hh in range(DN_HEADS):
        sl = slice(hh * DN_DIM, (hh + 1) * DN_DIM)
        parts.append((_rms_rows(odn_ref[:, sl], gdn_ref[...]) * zg_ref[:, sl]).astype(BF16))
    mix = _dot(jnp.concatenate(parts, axis=1), w_ref[...])
    x1 = x_ref[...] + gt_ref[...] * mix
    x1_ref[...] = x1
    h2_ref[...] = (_rms_rows(x1, gn_ref[...]) * (1.0 + sc_ref[...]) + sh_ref[...]).astype(BF16)


def _mod_specs(x, tm, per_row):
    d = x.shape[-1]
    if per_row:
        n = x.shape[0]
        row = lambda c: pl.BlockSpec((tm, c), lambda i: (i, 0))
        return (n // tm,), row, row(d), (n,), ("arbitrary",)
    b, t, _ = x.shape
    row = lambda c: pl.BlockSpec((None, tm, c), lambda i, j: (i, j, 0))
    per_b = pl.BlockSpec((None, 1, d), lambda i, j: (i, 0, 0))
    return (b, t // tm), row, per_b, (b, t), ("parallel", "arbitrary")


def _outproj(osb, odn, zg, gdn, x, gt, sc, sh, gn, w, *, tm, per_row):
    d = x.shape[-1]
    grid, row, mod, lead, sem = _mod_specs(x, tm, per_row)
    return pl.pallas_call(
        _outproj_kernel,
        grid=grid,
        in_specs=[row(SB_WIDTH), row(DN_WIDTH), row(DN_WIDTH), _resident(gdn.shape), row(d), mod, mod, mod,
                  _resident(gn.shape), _resident(w.shape)],
        out_specs=[row(d), row(d)],
        out_shape=[jax.ShapeDtypeStruct(lead + (d,), F32), jax.ShapeDtypeStruct(lead + (d,), BF16)],
        compiler_params=_cparams(sem),
        name="outproj_rows" if per_row else "outproj_seq",
    )(osb, odn, zg, gdn, x, gt, sc, sh, gn, w)


def _ffn_kernel(*refs, grouped, cw):
    if grouped:
        h_ref, x1_ref, gt_ref, wu_ref, wc_ref, wd_ref, buf_ref, y_ref, nbuf_ref, act_ref = refs
        carry_ref = None
    else:
        h_ref, x1_ref, gt_ref, wu_ref, wc_ref, wd_ref, y_ref, nbuf_ref, act_ref, carry_ref = refs
        buf_ref = None

        @pl.when(pl.program_id(1) == 0)
        def _():
            carry_ref[...] = jnp.zeros_like(carry_ref)

    hb = h_ref[...]
    tm = hb.shape[0]
    dff = wd_ref.shape[0]
    taps = wc_ref.shape[0]
    for c in range(dff // cw):
        halves = []
        for base in (0, dff):
            cs = slice(base + c * cw, base + (c + 1) * cw)
            cur = _dot(hb, wu_ref[:, cs])
            if grouped:
                y = _causal_conv(cur, wc_ref[:, cs], None, True, buf_ref[:, cs])
                nbuf_ref[:, cs] = pltpu.roll(cur, tm - (SUBLANES - (taps - 1)), 0)
            else:
                y = _causal_conv(cur, wc_ref[:, cs], carry_ref[:, cs], False, None)
                carry_ref[:, cs] = cur[tm - SUBLANES:tm]
                nbuf_ref[:, cs] = cur[tm - SUBLANES:tm]
            halves.append(y)
        act_ref[:, c * cw:(c + 1) * cw] = (_silu(halves[1]) * halves[0]).astype(BF16)
    y_ref[...] = x1_ref[...] + gt_ref[...] * _dot(act_ref[...], wd_ref[...])


def _ffn(h2, x1, gt, wu, wc, wd, bufp, *, tm, per_row, cw=256):
    d = x1.shape[-1]
    up = wu.shape[1]
    grid, row, mod, lead, sem = _mod_specs(x1, tm, per_row)
    in_specs = [row(d), row(d), mod, _resident(wu.shape), _resident(wc.shape), _resident(wd.shape)]
    args = [h2, x1, gt, wu, wc, wd]
    if per_row:
        in_specs.append(row(up))
        args.append(bufp)
        nb_shape, nb_spec, scratch = lead + (up,), row(up), [pltpu.VMEM((tm, up // 2), BF16)]
    else:
        nb_shape = (lead[0], SUBLANES, up)
        nb_spec = pl.BlockSpec((None, SUBLANES, up), lambda i, j: (i, 0, 0))
        scratch = [pltpu.VMEM((tm, up // 2), BF16), pltpu.VMEM((SUBLANES, up), F32)]
    return pl.pallas_call(
        functools.partial(_ffn_kernel, grouped=per_row, cw=cw),
        grid=grid, in_specs=in_specs,
        out_specs=[row(d), nb_spec],
        out_shape=[jax.ShapeDtypeStruct(lead + (d,), F32), jax.ShapeDtypeStruct(nb_shape, F32)],
        scratch_shapes=scratch,
        compiler_params=_cparams(sem),
        name="convffn_rows" if per_row else "convffn_seq",
    )(*args)


def _prep_weights(g_attn_norm, w_in, g_q, g_k, sb_bias, g_sb_out, w_dn_conv, a_log, dt_bias, g_dn_out,
                  w_out, g_ffn_norm, w_up, w_ffn_conv, w_down):
    d = w_in.shape[0]
    o1 = 3 * SB_WIDTH
    o2 = o1 + 3 * DN_WIDTH
    o3 = o2 + 2 * DN_HEADS
    pad = jnp.zeros((d, LANES - 2 * DN_HEADS), w_in.dtype)
    w1 = jnp.concatenate([w_in[:, :o2], w_in[:, o3:], w_in[:, o2:o3], pad], axis=1).astype(BF16)
    lane_pad = lambda v: jnp.zeros((1, LANES), F32).at[0, DN_HEADS:2 * DN_HEADS].set(v)
    return dict(
        gn1=g_attn_norm.reshape(1, d), w1=w1,
        gq=jnp.tile(g_q, SB_HEADS).reshape(1, SB_WIDTH), gk=jnp.tile(g_k, SB_HEADS).reshape(1, SB_WIDTH),
        bias=sb_bias.astype(F32), g_sb=jnp.tile(g_sb_out, SB_HEADS).reshape(1, SB_WIDTH),
        wc_dn=w_dn_conv, al=lane_pad(a_log), dt=lane_pad(dt_bias), gdn=g_dn_out.reshape(1, DN_DIM),
        w_out=w_out.astype(BF16), gn2=g_ffn_norm.reshape(1, d), w_up=w_up.astype(BF16),
        wc_ffn=w_ffn_conv, w_down=w_down.astype(BF16))


def _layer_prompt(x, mod, p):
    b, t, d = x.shape
    sh1, sc1, gt1, sh2, sc2, gt2 = [m.reshape(b, 1, d) for m in jnp.split(mod, 6, axis=-1)]
    qb, kb, vb, kf, vf, qd, kd, vd, zg, gates, dnbuf = _inproj(
        x, sc1, sh1, p["gn1"], p["w1"], p["gq"], p["gk"], p["wc_dn"], p["al"], p["dt"], None, tm=512)
    z_bound = 8.1 * jnp.max(jnp.abs(p["gq"])) * jnp.max(jnp.abs(p["gk"])) + jnp.max(jnp.abs(p["bias"]))
    attend = lambda bounded: functools.partial(_sb_prompt, tq=512, tb=256, unit=1, bounded=bounded)
    osb = lax.cond(z_bound < 80.0, attend(True), attend(False),
                   qb, kb, vb, p["bias"], p["g_sb"][:, :LANES])
    odn, s_new = _dn_seq(qd, kd, vd, gates)
    x1, h2 = _outproj(osb, odn, zg, p["gdn"], x, gt1, sc2, sh2, p["gn2"], p["w_out"], tm=512, per_row=False)
    y, ffbuf = _ffn(h2, x1, gt2, p["w_up"], p["wc_ffn"], p["w_down"], None, tm=512, per_row=False)
    k_dn = p["wc_dn"].shape[0] - 1
    k_ff = p["wc_ffn"].shape[0] - 1
    return (y, kf.reshape(b, t, SB_HEADS, SB_DIM), vf.reshape(b, t, SB_HEADS, SB_DIM), s_new,
            dnbuf[:, SUBLANES - k_dn:], ffbuf[:, SUBLANES - k_ff:])


def _layer_sample(x, mod, page_table, cache_k, cache_v, s0, dn_buf, ffn_buf, p):
    bs, n_q, d = x.shape
    assert n_q == SUBLANES
    n = bs * n_q
    rep = lambda m: jnp.broadcast_to(m[:, None, :], (bs, n_q, d)).reshape(n, d)
    sh1, sc1, gt1, sh2, sc2, gt2 = [rep(m) for m in jnp.split(mod, 6, axis=-1)]
    padrows = lambda buf: jnp.pad(buf, ((0, 0), (0, n_q - buf.shape[1]), (0, 0))).reshape(n, buf.shape[2])
    xf = x.reshape(n, d)
    qb, kb, vb, kf, vf, qd, kd, vd, zg, gates, dnbuf = _inproj(
        xf, sc1, sh1, p["gn1"], p["w1"], p["gq"], p["gk"], p["wc_dn"], p["al"], p["dt"], padrows(dn_buf), tm=256)
    del kb, vb
    w = SB_WIDTH
    bias_rows = jnp.repeat(p["bias"], n_q).reshape(SB_HEADS * n_q, 1)
    n_phys, page = cache_k.shape[0], cache_k.shape[1]
    pool_t = lambda c: jnp.transpose(c, (0, 2, 3, 1)).reshape(n_phys, w, page)
    osb = _sb_sample(page_table, bias_rows, qb.reshape(bs, n_q, w), kf.reshape(bs, n_q, w),
                     vf.reshape(bs, n_q, w), p["g_sb"], pool_t(cache_k), pool_t(cache_v))
    odn, s_new = _dn_grouped(qd, kd, vd, gates, s0, group=n_q)
    x1, h2 = _outproj(osb.reshape(n, w), odn, zg, p["gdn"], xf, gt1, sc2, sh2, p["gn2"], p["w_out"],
                      tm=256, per_row=True)
    y, ffbuf = _ffn(h2, x1, gt2, p["w_up"], p["wc_ffn"], p["w_down"], padrows(ffn_buf), tm=256, per_row=True)
    k_dn = p["wc_dn"].shape[0] - 1
    k_ff = p["wc_ffn"].shape[0] - 1
    return (y.reshape(bs, n_q, d), kf.reshape(bs, n_q, SB_HEADS, SB_DIM), vf.reshape(bs, n_q, SB_HEADS, SB_DIM),
            s_new, dnbuf.reshape(bs, n_q, -1)[:, :k_dn], ffbuf.reshape(bs, n_q, -1)[:, :k_ff])


def kernel(x_prompt, x_sample, c_prompt, c_sample, cache_k, cache_v, page_table, state_delta, state_dn_conv, state_ffn_conv, w_ada, b_ada, g_attn_norm, w_in, g_q, g_k, sb_bias, g_sb_out, w_dn_conv, a_log, dt_bias, g_dn_out, w_out, g_ffn_norm, w_up, w_ffn_conv, w_down):
    depth = w_ada.shape[0]
    bp = x_prompt.shape[0]
    yp, ys = x_prompt, x_sample
    outs = [[] for _ in range(10)]
    c_all = jnp.concatenate([c_prompt, c_sample], axis=0)
    pad_rows = (-c_all.shape[0]) % SUBLANES
    c_all = jnp.pad(c_all, ((0, pad_rows), (0, 0)))
    for l in range(depth):
        p = _prep_weights(g_attn_norm[l], w_in[l], g_q[l], g_k[l], sb_bias[l], g_sb_out[l], w_dn_conv[l],
                          a_log[l], dt_bias[l], g_dn_out[l], w_out[l], g_ffn_norm[l], w_up[l],
                          w_ffn_conv[l], w_down[l])
        mod = _ada(c_all, w_ada[l], b_ada[l])
        yp, kp, vp, sp, dcp, fcp = _layer_prompt(yp, mod[:bp], p)
        ys, ks, vs, ss, dcs, fcs = _layer_sample(ys, mod[bp:bp + x_sample.shape[0]], page_table,
                                                 cache_k[l], cache_v[l], state_delta[l],
                                                 state_dn_conv[l], state_ffn_conv[l], p)
        for lst, val in zip(outs, (kp, vp, ks, vs, sp, ss, dcp, dcs, fcp, fcs)):
            lst.append(val)
    return (yp, ys) + tuple(jnp.stack(o) for o in outs)
```

```python
import functools

import jax
import jax.numpy as jnp
from jax import lax
from jax.experimental import pallas as pl
from jax.experimental.pallas import tpu as pltpu

F32 = jnp.float32
BF16 = jnp.bfloat16

NORM_EPS = 1e-6
LOG2E = 1.4426950408889634
LANES = 128
SUBLANES = 8
VMEM_LIMIT = 56 * 1024 * 1024

SB_HEADS = 8
SB_DIM = 64
SB_WIDTH = SB_HEADS * SB_DIM
DN_HEADS = 4
DN_DIM = 128
DN_WIDTH = DN_HEADS * DN_DIM
DN_BLOCK = 128


def _dot(a, b):
    return jnp.dot(a, b, preferred_element_type=F32)


def _dot_nt(a, b):
    return lax.dot_general(a, b, (((1,), (1,)), ((), ())), preferred_element_type=F32)


def _dot_tn(a, b):
    return lax.dot_general(a, b, (((0,), (0,)), ((), ())), preferred_element_type=F32)


def _silu(x):
    return x * jax.nn.sigmoid(x)


def _exp_neg_abs(x):
    return jnp.exp2(jnp.abs(x) * (-LOG2E))


def _softplus_parts(z):
    l = jnp.log(1.0 + jnp.exp(-jnp.abs(z)))
    return jnp.maximum(z, 0.0) + l, jnp.minimum(z, 0.0) - l


def _rms_rows(x, g):
    ms = jnp.mean(x * x, axis=-1, keepdims=True)
    return x * lax.rsqrt(ms + NORM_EPS) * g


def _half_lane_rms(blk, lo):
    sq = blk * blk
    s_lo = jnp.sum(jnp.where(lo, sq, 0.0), axis=-1, keepdims=True)
    s_hi = jnp.sum(jnp.where(lo, 0.0, sq), axis=-1, keepdims=True)
    ms = jnp.where(lo, s_lo, s_hi) * (1.0 / SB_DIM)
    return blk * lax.rsqrt(ms + NORM_EPS)


def _cparams(sem):
    return pltpu.CompilerParams(dimension_semantics=sem, vmem_limit_bytes=VMEM_LIMIT)


def _resident(shape):
    nd = len(shape)
    return pl.BlockSpec(shape, lambda *_: (0,) * nd, pipeline_mode=pl.Buffered(1))


def _ada_kernel(c_ref, w_ref, b_ref, o_ref):
    a = _silu(c_ref[...]).astype(BF16)
    o_ref[...] = _dot(a, w_ref[...].astype(BF16)) + b_ref[...]


def _ada(c, w, b):
    m, d = c.shape
    n = w.shape[1]
    tn = 512
    return pl.pallas_call(
        _ada_kernel,
        grid=(n // tn,),
        in_specs=[pl.BlockSpec((m, d), lambda j: (0, 0)),
                  pl.BlockSpec((d, tn), lambda j: (0, j)),
                  pl.BlockSpec((1, tn), lambda j: (0, j))],
        out_specs=pl.BlockSpec((m, tn), lambda j: (0, j)),
        out_shape=jax.ShapeDtypeStruct((m, n), F32),
        compiler_params=_cparams(("parallel",)),
        name="adaln_mod",
    )(c, w, b.reshape(1, n))


C_SB = 0
C_DN = 3 * SB_WIDTH
C_Z = C_DN + 3 * DN_WIDTH
C_G = C_Z + DN_WIDTH
C_END = C_G + LANES


def _causal_conv(cur, taps, prev_rows, grouped, bufp):
    w = taps.shape[0]
    tm = cur.shape[0]
    y = taps[w - 1:w, :] * cur
    if grouped:
        rowm = lax.broadcasted_iota(jnp.int32, (tm, 1), 0) & (SUBLANES - 1)
        for s in range(1, w):
            r = pltpu.roll(cur, s, 0)
            back = (w - 1) - s
            bs = bufp if back == 0 else pltpu.roll(bufp, tm - back, 0)
            y = y + taps[w - 1 - s:w - s, :] * jnp.where(rowm >= s, r, bs)
        return y
    row8 = lax.broadcasted_iota(jnp.int32, (SUBLANES, 1), 0)
    yh = taps[w - 1:w, :] * cur[0:SUBLANES]
    for s in range(1, w):
        r = pltpu.roll(cur, s, 0)
        y = y + taps[w - 1 - s:w - s, :] * r
        head = jnp.where(row8 < s, pltpu.roll(prev_rows, s, 0), r[0:SUBLANES])
        yh = yh + taps[w - 1 - s:w - s, :] * head
    return jnp.concatenate([yh, y[SUBLANES:]], axis=0)


def _inproj_kernel(*refs, grouped):
    if grouped:
        (x_ref, sc_ref, sh_ref, gn_ref, w_ref, gq_ref, gk_ref, wc_ref, al_ref, dt_ref, buf_ref,
         qb_ref, kb_ref, vb_ref, kf_ref, vf_ref, qd_ref, kd_ref, vd_ref, zg_ref, gate_ref, nbuf_ref) = refs
        carry_ref = None
    else:
        (x_ref, sc_ref, sh_ref, gn_ref, w_ref, gq_ref, gk_ref, wc_ref, al_ref, dt_ref,
         qb_ref, kb_ref, vb_ref, kf_ref, vf_ref, qd_ref, kd_ref, vd_ref, zg_ref, gate_ref, nbuf_ref,
         carry_ref) = refs
        buf_ref = None

        @pl.when(pl.program_id(1) == 0)
        def _():
            carry_ref[...] = jnp.zeros_like(carry_ref)

    x = x_ref[...]
    h = _rms_rows(x, gn_ref[...]) * (1.0 + sc_ref[...]) + sh_ref[...]
    hb = h.astype(BF16)
    tm = x.shape[0]
    lane = lax.broadcasted_iota(jnp.int32, (1, LANES), 1)
    lo = lane < SB_DIM

    pq = _dot(hb, w_ref[:, C_SB:C_SB + SB_WIDTH])
    pk = _dot(hb, w_ref[:, C_SB + SB_WIDTH:C_SB + 2 * SB_WIDTH])
    for c in range(SB_WIDTH // LANES):
        sl = slice(c * LANES, (c + 1) * LANES)
        qn = _half_lane_rms(pq[:, sl], lo) * gq_ref[:, sl]
        qb_ref[:, sl] = (qn * (SB_DIM ** -0.5)).astype(BF16)
        kn = _half_lane_rms(pk[:, sl], lo) * gk_ref[:, sl]
        kf_ref[:, sl] = kn
        kb_ref[:, sl] = kn.astype(BF16)
    pv = _dot(hb, w_ref[:, C_SB + 2 * SB_WIDTH:C_SB + 3 * SB_WIDTH])
    vf_ref[...] = pv
    vb_ref[...] = pv.astype(BF16)

    cur = _dot(hb, w_ref[:, C_DN:C_Z])
    if grouped:
        y = _causal_conv(cur, wc_ref[...], None, True, buf_ref[...])
        nbuf_ref[...] = pltpu.roll(cur, tm - (SUBLANES - (wc_ref.shape[0] - 1)), 0)
    else:
        y = _causal_conv(cur, wc_ref[...], carry_ref[...], False, None)
        carry_ref[...] = cur[tm - SUBLANES:tm]
        nbuf_ref[...] = cur[tm - SUBLANES:tm]
    a = _silu(y)
    for hh in range(DN_HEADS):
        sl = slice(hh * DN_DIM, (hh + 1) * DN_DIM)
        qh = a[:, hh * DN_DIM:(hh + 1) * DN_DIM]
        kh = a[:, DN_WIDTH + hh * DN_DIM:DN_WIDTH + (hh + 1) * DN_DIM]
        qd_ref[:, sl] = qh * (lax.rsqrt(jnp.sum(qh * qh, axis=-1, keepdims=True) + NORM_EPS) * (DN_DIM ** -0.5))
        kd_ref[:, sl] = kh * lax.rsqrt(jnp.sum(kh * kh, axis=-1, keepdims=True) + NORM_EPS)
    vd_ref[...] = a[:, 2 * DN_WIDTH:3 * DN_WIDTH]

    zgb = _dot(hb, w_ref[:, C_Z:C_END])
    zg_ref[...] = _silu(zgb[:, :DN_WIDTH])

    gb = zgb[:, DN_WIDTH:]
    sp, _ = _softplus_parts(gb + dt_ref[...])
    gate_ref[...] = jnp.where(lane < DN_HEADS, jax.nn.sigmoid(gb), -jnp.exp(al_ref[...]) * sp)


def _inproj(x, sc, sh, gn, w1, gq, gk, wc, al, dt, bufp, *, tm):
    grouped = bufp is not None
    d = x.shape[-1]
    if grouped:
        n = x.shape[0]
        grid = (n // tm,)
        row = lambda c: pl.BlockSpec((tm, c), lambda i: (i, 0))
        in_specs = [row(d), row(d), row(d)]
        lead = (n,)
        sem = ("arbitrary",)
        nb_shape, nb_spec = (n, 3 * DN_WIDTH), row(3 * DN_WIDTH)
        scratch = []
    else:
        b, t, _ = x.shape
        grid = (b, t // tm)
        row = lambda c: pl.BlockSpec((None, tm, c), lambda i, j: (i, j, 0))
        per_b = pl.BlockSpec((None, 1, d), lambda i, j: (i, 0, 0))
        in_specs = [row(d), per_b, per_b]
        lead = (b, t)
        sem = ("parallel", "arbitrary")
        nb_shape = (b, SUBLANES, 3 * DN_WIDTH)
        nb_spec = pl.BlockSpec((None, SUBLANES, 3 * DN_WIDTH), lambda i, j: (i, 0, 0))
        scratch = [pltpu.VMEM((SUBLANES, 3 * DN_WIDTH), F32)]
    in_specs += [_resident(gn.shape), _resident(w1.shape), _resident(gq.shape), _resident(gk.shape),
                 _resident(wc.shape), _resident(al.shape), _resident(dt.shape)]
    args = [x, sc, sh, gn, w1, gq, gk, wc, al, dt]
    if grouped:
        in_specs.append(row(3 * DN_WIDTH))
        args.append(bufp)
    out_cols = [(SB_WIDTH, BF16)] * 3 + [(SB_WIDTH, F32)] * 2 + [(DN_WIDTH, F32)] * 4 + [(LANES, F32)]
    out_shape = [jax.ShapeDtypeStruct(lead + (c,), dt_) for c, dt_ in out_cols]
    out_specs = [row(c) for c, _ in out_cols]
    out_shape.append(jax.ShapeDtypeStruct(nb_shape, F32))
    out_specs.append(nb_spec)
    return pl.pallas_call(
        functools.partial(_inproj_kernel, grouped=grouped),
        grid=grid, in_specs=in_specs, out_specs=out_specs, out_shape=out_shape,
        scratch_shapes=scratch,
        compiler_params=_cparams(sem),
        name="inproj_grouped" if grouped else "inproj_seq",
    )(*args)


def _sb_prompt_kernel(bias_ref, q_ref, k_ref, v_ref, g_ref, o_ref, acc_ref, run_ref, sp_ref, lb_ref,
                      *, tq, tb, unit, bounded):
    softplus = ((lambda z: jnp.log(1.0 + jnp.exp(z))) if bounded else
                (lambda z: jnp.maximum(z, 0.0) + jnp.log(1.0 + _exp_neg_abs(z))))
    p = pl.program_id(1)
    i = pl.program_id(2)
    lane = lax.broadcasted_iota(jnp.int32, (1, LANES), 1)
    lo = lane < SB_DIM
    q = q_ref[...]
    zero = jnp.zeros_like(q)
    q2 = jnp.concatenate([jnp.where(lo, q, zero), jnp.where(lo, zero, q)], axis=0)
    head_row = lax.broadcasted_iota(jnp.int32, (2 * tq, LANES), 0) < tq
    lane2 = lax.broadcasted_iota(jnp.int32, (2 * tq, LANES), 1)
    b_parts = [m.astype(F32) for m in _split3(jnp.where(head_row, bias_ref[2 * p], bias_ref[2 * p + 1]))]
    bias_lanes = jnp.where(lane2 == 0, b_parts[0],
                           jnp.where(lane2 == 1, b_parts[1], jnp.where(lane2 == 2, b_parts[2], 0.0)))
    q2x = jnp.concatenate([q2, bias_lanes.astype(BF16)], axis=1)
    ones_lanes = jnp.where(lax.broadcasted_iota(jnp.int32, (tb, LANES), 1) < 3, 1.0, 0.0).astype(BF16)

    def with_ones(ks):
        n = ks.shape[0] // tb
        ones = ones_lanes if n == 1 else jnp.concatenate([ones_lanes] * n, axis=0)
        return jnp.concatenate([ks, ones], axis=1)
    row = lax.broadcasted_iota(jnp.int32, (tb, tb), 0)
    col = lax.broadcasted_iota(jnp.int32, (tb, tb), 1)
    later_keys = jnp.where(row > col, 1.0, 0.0).astype(BF16)
    qrow = lax.broadcasted_iota(jnp.int32, (2 * tq, tb), 0) & (tq - 1)
    kcol = lax.broadcasted_iota(jnp.int32, (2 * tq, tb), 1)

    acc_ref[...] = jnp.zeros_like(acc_ref)
    run_ref[...] = jnp.zeros_like(run_ref)

    def block(j, masked):
        start = pl.multiple_of(j * tb, tb)
        ks = k_ref[pl.ds(start, tb), :]
        vs = v_ref[pl.ds(start, tb), :]
        z = _dot_nt(q2x, with_ones(ks))
        sp = softplus(z)
        lb = z - sp
        if masked:
            valid = kcol + (j * tb - i * tq) < qrow
            sp = jnp.where(valid, sp, 0.0)
        later = _dot(sp.astype(BF16), later_keys)
        run = run_ref[...]
        att = jnp.concatenate(
            [jnp.exp(lb[:, c * LANES:(c + 1) * LANES] - later[:, c * LANES:(c + 1) * LANES] - run)
             for c in range(tb // LANES)], axis=1)
        if masked:
            att = jnp.where(valid, att, 0.0)
        acc_ref[...] += _dot(att.astype(BF16), vs)
        run_ref[...] = run + (later[:, 0:1] + sp[:, 0:1])

    def stage1(jhi, slot):
        start = pl.multiple_of((jhi - (unit - 1)) * tb, tb)
        z = _dot_nt(q2x, with_ones(k_ref[pl.ds(start, unit * tb), :]))
        sp = softplus(z)
        lb_ref[slot] = z - sp
        sp_ref[slot] = sp.astype(BF16)

    def stage2(jhi, slot):
        start = pl.multiple_of((jhi - (unit - 1)) * tb, tb)
        sp16 = sp_ref[slot]
        cols = [slice((unit - 1 - s) * tb, (unit - s) * tb) for s in range(unit)]
        later = _dot(jnp.concatenate([sp16[:, cs] for cs in cols], axis=0), later_keys)
        run = run_ref[...]
        att = [None] * unit
        for s, cs in enumerate(cols):
            lat = later[s * 2 * tq:(s + 1) * 2 * tq]
            att[unit - 1 - s] = jnp.concatenate(
                [jnp.exp(lb_ref[slot, :, cs.start + c * LANES:cs.start + (c + 1) * LANES]
                         - lat[:, c * LANES:(c + 1) * LANES] - run) for c in range(tb // LANES)], axis=1)
            run = run + (lat[:, 0:1] + sp16[:, cs.start:cs.start + 1].astype(F32))
        run_ref[...] = run
        acc_ref[...] += _dot(jnp.concatenate(att, axis=1).astype(BF16), v_ref[pl.ds(start, unit * tb), :])

    n_diag = tq // tb
    n_full = n_diag * i
    left = n_full % unit
    n_units = n_full // unit
    top = n_full - 1 - left
    lone = n_units & 1
    first = top - lone * unit
    n_pairs = n_units >> 1

    def diagonal():
        for d in range(n_diag):
            block(n_full + n_diag - 1 - d, True)

    @pl.when(n_pairs >= 1)
    def _():
        stage1(first, 0)
        diagonal()

    @pl.when(n_pairs < 1)
    def _():
        diagonal()

    for r in range(unit - 1):
        @pl.when(left > r)
        def _(r=r):
            block(n_full - 1 - r, False)

    @pl.when(lone == 1)
    def _():
        stage1(top, 1)
        stage2(top, 1)

    @pl.when(n_pairs >= 1)
    def _():
        def body(u, carry):
            jhi = first - 2 * u * unit
            stage2(jhi, 0)
            stage1(jhi - unit, 1)
            stage2(jhi - unit, 1)
            stage1(jhi - 2 * unit, 0)
            return carry

        lax.fori_loop(0, n_pairs - 1, body, 0)
        jhi = first - 2 * (n_pairs - 1) * unit
        stage2(jhi, 0)
        stage1(jhi - unit, 1)
        stage2(jhi - unit, 1)

    o = jnp.where(lo, acc_ref[0:tq], acc_ref[tq:2 * tq])
    o_ref[...] = (_half_lane_rms(o, lo) * g_ref[...]).astype(o_ref.dtype)


def _sb_prompt(qb, kb, vb, bias, g2, *, tq, tb, unit, bounded):
    b, t, w = qb.shape
    pairs = w // LANES
    return pl.pallas_call(
        functools.partial(_sb_prompt_kernel, tq=tq, tb=tb, unit=unit, bounded=bounded),
        grid_spec=pltpu.PrefetchScalarGridSpec(
            num_scalar_prefetch=0,
            grid=(b, pairs, t // tq),
            in_specs=[pl.BlockSpec(memory_space=pltpu.SMEM),
                      pl.BlockSpec((None, tq, LANES), lambda i, p, j: (i, j, p)),
                      pl.BlockSpec((None, t, LANES), lambda i, p, j: (i, 0, p)),
                      pl.BlockSpec((None, t, LANES), lambda i, p, j: (i, 0, p)),
                      pl.BlockSpec((1, LANES), lambda i, p, j: (0, 0))],
            out_specs=pl.BlockSpec((None, tq, LANES), lambda i, p, j: (i, j, p)),
            scratch_shapes=[pltpu.VMEM((2 * tq, LANES), F32), pltpu.VMEM((2 * tq, LANES), F32),
                            pltpu.VMEM((2, 2 * tq, unit * tb), BF16), pltpu.VMEM((2, 2 * tq, unit * tb), F32)]),
        out_shape=jax.ShapeDtypeStruct((b, t, w), BF16),
        compiler_params=_cparams(("parallel", "parallel", "arbitrary")),
        name="sb_attn_prompt_bounded" if bounded else "sb_attn_prompt",
    )(bias, qb, kb, vb, g2)


def _page_copy(pt_ref, pool_ref, buf_ref, sem_ref, seq, slot, pg):
    return pltpu.make_async_copy(pool_ref.at[pt_ref[seq, pg]], buf_ref.at[slot, pg], sem_ref.at[slot])


def _sb_sample_kernel(pt_ref, bias_ref, q_ref, kn_ref, vn_ref, g_ref, kpool_ref, vpool_ref, o_ref,
                      kbuf_ref, vbuf_ref, ksem_ref, vsem_ref, newk_ref, newv_ref, *, n_pages, n_q):
    seq = pl.program_id(0)
    n_seq = pl.num_programs(0)
    slot = seq & 1
    rows = SB_HEADS * n_q
    page = kbuf_ref.shape[3]
    width = SB_WIDTH
    pools = ((kpool_ref, kbuf_ref, ksem_ref), (vpool_ref, vbuf_ref, vsem_ref))

    @pl.when(seq == 0)
    def _():
        for pool, buf, sem in pools:
            for pg in range(n_pages):
                _page_copy(pt_ref, pool, buf, sem, 0, 0, pg).start()

    @pl.when(seq + 1 < n_seq)
    def _():
        for pool, buf, sem in pools:
            for pg in range(n_pages):
                _page_copy(pt_ref, pool, buf, sem, seq + 1, 1 - slot, pg).start()

    r1 = lax.broadcasted_iota(jnp.int32, (rows, 1), 0)
    later_keys = jnp.where(lax.broadcasted_iota(jnp.int32, (page, page), 0)
                           > lax.broadcasted_iota(jnp.int32, (page, page), 1), 1.0, 0.0).astype(BF16)
    head_of_lane = lax.broadcasted_iota(jnp.int32, (rows, width), 1) // SB_DIM
    head_of_row = lax.broadcasted_iota(jnp.int32, (rows, width), 0) // n_q
    qrep = jnp.concatenate([q_ref[...].astype(F32)] * SB_HEADS, axis=0)
    qx = jnp.where(head_of_lane == head_of_row, qrep, 0.0).astype(BF16)
    bias = bias_ref[...]

    newk_ref[...] = jnp.zeros_like(newk_ref)
    newv_ref[...] = jnp.zeros_like(newv_ref)
    newk_ref[0:n_q, :] = kn_ref[...]
    newv_ref[0:n_q, :] = vn_ref[...]
    z = _dot_nt(qx, newk_ref[...].astype(BF16)) + bias
    sp, lb = _softplus_parts(z)
    valid = lax.broadcasted_iota(jnp.int32, (rows, page), 1) < (r1 & (n_q - 1))
    sp = jnp.where(valid, sp, 0.0)
    att = jnp.where(valid, jnp.exp(lb - _dot(sp.astype(BF16), later_keys)), 0.0)
    acc = _dot(att.astype(BF16), newv_ref[...].astype(BF16))
    run = jnp.sum(sp, axis=-1, keepdims=True)

    for pool, buf, sem in pools:
        for pg in range(n_pages):
            _page_copy(pt_ref, pool, buf, sem, seq, slot, pg).wait()

    order = [n_pages - 1 - r for r in range(n_pages)]
    k_all = jnp.concatenate([kbuf_ref[slot, pg].astype(BF16) for pg in order], axis=1)
    z = _dot(qx, k_all) + bias
    sp, lb = _softplus_parts(z)
    sp16 = sp.astype(BF16)
    later = _dot(jnp.concatenate([sp16[:, r * page:(r + 1) * page] for r in range(n_pages)], axis=0),
                 later_keys)
    args = []
    for r in range(n_pages):
        cs = slice(r * page, (r + 1) * page)
        args.append(lb[:, cs] - later[r * rows:(r + 1) * rows] - run)
        run = run + jnp.sum(sp[:, cs], axis=-1, keepdims=True)
    att = jnp.exp(jnp.concatenate(args, axis=1)).astype(BF16)
    v_all = jnp.concatenate([vbuf_ref[slot, pg].astype(BF16) for pg in order], axis=1)
    acc = acc + _dot_nt(att, v_all)

    lane_head = lax.broadcasted_iota(jnp.int32, (n_q, width), 1) // SB_DIM
    o = jnp.zeros((n_q, width), F32)
    for hh in range(SB_HEADS):
        o = o + jnp.where(lane_head == hh, acc[hh * n_q:(hh + 1) * n_q, :], 0.0)
    lo = lax.broadcasted_iota(jnp.int32, (1, LANES), 1) < SB_DIM
    for c in range(width // LANES):
        sl = slice(c * LANES, (c + 1) * LANES)
        o_ref[:, sl] = _half_lane_rms(o[:, sl], lo) * g_ref[:, sl]


def _sb_sample(page_table, bias_rows, q3, kn3, vn3, g_sb, cache_k, cache_v):
    bs, n_q, w = q3.shape
    n_pages = page_table.shape[1]
    page = cache_k.shape[2]
    rows = SB_HEADS * n_q
    tok = lambda: pl.BlockSpec((None, n_q, w), lambda i, pt: (i, 0, 0))
    const = lambda shp: pl.BlockSpec(shp, lambda i, pt: (0,) * len(shp))
    hbm = lambda: pl.BlockSpec(memory_space=pl.ANY)
    return pl.pallas_call(
        functools.partial(_sb_sample_kernel, n_pages=n_pages, n_q=n_q),
        grid_spec=pltpu.PrefetchScalarGridSpec(
            num_scalar_prefetch=1,
            grid=(bs,),
            in_specs=[const((rows, 1)), tok(), tok(), tok(), const((1, w)), hbm(), hbm()],
            out_specs=tok(),
            scratch_shapes=[pltpu.VMEM((2, n_pages, w, page), F32), pltpu.VMEM((2, n_pages, w, page), F32),
                            pltpu.SemaphoreType.DMA((2,)), pltpu.SemaphoreType.DMA((2,)),
                            pltpu.VMEM((page, w), F32), pltpu.VMEM((page, w), F32)]),
        out_shape=jax.ShapeDtypeStruct((bs, n_q, w), F32),
        compiler_params=_cparams(("arbitrary",)),
        name="sb_attn_paged",
    )(page_table, bias_rows, q3, kn3, vn3, g_sb, cache_k, cache_v)


def _split3(x):
    hi = x.astype(BF16)
    r = x - hi.astype(F32)
    mid = r.astype(BF16)
    lo = (r - mid.astype(F32)).astype(BF16)
    return hi, mid, lo


def _dn_local(blocks, group):
    n = blocks[0][3].shape[0]
    units = [(bi, h) for bi in range(len(blocks)) for h in range(DN_HEADS)]
    hs = [slice(h * DN_DIM, (h + 1) * DN_DIM) for h in range(DN_HEADS)]
    row = lax.broadcasted_iota(jnp.int32, (n, n), 0)
    col = lax.broadcasted_iota(jnp.int32, (n, n), 1)
    shift = group.bit_length() - 1
    same = jnp.where((row >> shift) == (col >> shift), 1.0, 0.0)
    low = jnp.where(row >= col, same, 0.0)
    strict = jnp.where(row > col, same, 0.0)
    low16 = low.astype(BF16)
    same16 = same.astype(BF16)
    gparts = [_split3(blk[3]) for blk in blocks]
    g_all = [sum(_dot(low16, m) for m in gp) for gp in gparts]
    t_all = [sum(_dot(same16, m) for m in gp) for gp in gparts]
    g_row = [jnp.broadcast_to(g_all[bi][:, DN_HEADS + h:DN_HEADS + h + 1], (n, n)) for bi, h in units]
    gtot = [jnp.broadcast_to(t_all[bi][:, DN_HEADS + h:DN_HEADS + h + 1], (n, n)) for bi, h in units]
    un = range(len(units))
    decay = [low * jnp.exp(jnp.minimum(g_row[u] - g_row[u].T, 0.0)) for u in un]
    beta = [blocks[bi][3][:, h:h + 1] for bi, h in units]
    k = [blocks[bi][1][:, hs[h]] for bi, h in units]
    k16 = [k[u].astype(BF16) for u in un]
    kb = [k[u] * beta[u] for u in un]
    a_mat = [strict * (_dot_nt(kb[u].astype(BF16), k16[u]) * decay[u]) for u in un]
    eye = jnp.where(row == col, 1.0, 0.0)
    inv = [eye - a_mat[u] * jnp.where((row >> 1) == (col >> 1), 1.0, 0.0) for u in un]
    for lb in range(1, shift):
        sel = jnp.where((row >> (lb + 1)) == (col >> (lb + 1)),
                        jnp.where(((row >> lb) & 1) > ((col >> lb) & 1), 1.0, 0.0), 0.0)
        inv16 = [inv[u].astype(BF16) for u in un]
        x = [_dot((a_mat[u] * sel).astype(BF16), inv16[u]) for u in un]
        inv = [inv[u] - _dot(inv16[u], x[u].astype(BF16)) for u in un]
    eg = [jnp.exp(g_row[u]) for u in un]
    v = [blocks[bi][2][:, hs[h]] for bi, h in units]
    rhs = [jnp.concatenate([v[u] * beta[u], kb[u] * eg[u]], axis=-1).astype(BF16) for u in un]
    sol = [_dot(inv[u].astype(BF16), rhs[u]) for u in un]
    q = [blocks[bi][0][:, hs[h]] for bi, h in units]
    intra = [_dot_nt(q[u].astype(BF16), k16[u]) * decay[u] for u in un]
    usol = [sol[u][:, :DN_DIM] for u in un]
    kcum = [sol[u][:, DN_DIM:] for u in un]
    qdec = [q[u] * eg[u] for u in un]
    kdec = [k[u] * jnp.exp(gtot[u] - g_row[u]) for u in un]
    return usol, kcum, intra, qdec, kdec, gtot


def _dn_seq_kernel(q_ref, k_ref, v_ref, gate_ref, o_ref, sout_ref, s_ref, *, group):
    @pl.when(pl.program_id(0) == 0)
    def _():
        s_ref[...] = jnp.zeros_like(s_ref)

    nb, n = q_ref.shape[0], q_ref.shape[1]
    blocks = [(q_ref.at[bi], k_ref.at[bi], v_ref.at[bi], gate_ref[bi]) for bi in range(nb)]
    usol, kcum, intra, qdec, kdec, gtot = _dn_local(blocks, group)
    units = [(bi, h) for bi in range(nb) for h in range(DN_HEADS)]
    un = range(len(units))
    s = [s_ref[bi, h] for bi, h in units]
    vnews = [[] for _ in un]
    outs = [[] for _ in un]
    for c in range(n // group):
        rs = slice(c * group, (c + 1) * group)
        s16 = [s[u].astype(BF16) for u in un]
        lhs = [jnp.concatenate([kcum[u][rs], qdec[u][rs]], axis=0).astype(BF16) for u in un]
        both = [_dot(lhs[u], s16[u]) for u in un]
        vnew = [usol[u][rs] - both[u][:group] for u in un]
        upd = [_dot_tn(kdec[u][rs].astype(BF16), vnew[u].astype(BF16)) for u in un]
        s = [s[u] * jnp.exp(gtot[u][c * group:c * group + 1, :]) + upd[u] for u in un]
        for u in un:
            vnews[u].append(vnew[u])
            outs[u].append(both[u][group:])
    for u, (bi, h) in enumerate(units):
        s_ref[bi, h] = s[u]
        vfull = jnp.concatenate(vnews[u], axis=0).astype(BF16)
        o_ref[bi, :, h * DN_DIM:(h + 1) * DN_DIM] = (jnp.concatenate(outs[u], axis=0)
                                                     + _dot(intra[u].astype(BF16), vfull))

    @pl.when(pl.program_id(0) == pl.num_programs(0) - 1)
    def _():
        sout_ref[...] = s_ref[...]


def _dn_seq(qd, kd, vd, gates):
    b, t, w = qd.shape
    n = DN_BLOCK
    row = lambda c: pl.BlockSpec((b, n, c), lambda j: (0, j, 0))
    state = (b, DN_HEADS, DN_DIM, DN_DIM)
    return pl.pallas_call(
        functools.partial(_dn_seq_kernel, group=min(64, n)),
        grid=(t // n,),
        in_specs=[row(w), row(w), row(w), row(LANES)],
        out_specs=[row(w), pl.BlockSpec(state, lambda j: (0, 0, 0, 0))],
        out_shape=[jax.ShapeDtypeStruct((b, t, w), F32), jax.ShapeDtypeStruct(state, F32)],
        scratch_shapes=[pltpu.VMEM(state, F32)],
        compiler_params=_cparams(("arbitrary",)),
        name="deltanet_seq",
    )(qd, kd, vd, gates)


def _dn_grouped_kernel(q_ref, k_ref, v_ref, gate_ref, s0_ref, o_ref, sout_ref,
                       u_ref, kc_ref, qd_ref, kd_ref, gt_ref, vn_ref, oq_ref, *, group):
    n = q_ref.shape[0]
    heads = range(DN_HEADS)
    row1 = lax.broadcasted_iota(jnp.int32, (n, 1), 0)
    shift = group.bit_length() - 1
    u, kcum, intra, qdec, kdec, gtot = _dn_local([(q_ref, k_ref, v_ref, gate_ref[...])], group)
    for h in heads:
        u_ref[h] = u[h]
        kc_ref[h] = kcum[h]
        qd_ref[h] = qdec[h]
        kd_ref[h] = kdec[h]
        gt_ref[h] = gtot[h]
    vn_ref[...] = jnp.zeros_like(vn_ref)

    def per_seq(bi, carry):
        rs = pl.ds(pl.multiple_of(bi * group, group), group)
        mine = (row1 >> shift) == bi
        s = [s0_ref[bi, h] for h in heads]
        lhs = [jnp.concatenate([kc_ref[h, rs, :], qd_ref[h, rs, :]], axis=0).astype(BF16) for h in heads]
        both = [_dot(lhs[h], s[h].astype(BF16)) for h in heads]
        for h in heads:
            oq_ref[h, rs, :] = both[h][group:]
            vn_ref[h, rs, :] = u_ref[h, rs, :] - both[h][:group]
        kmask = [jnp.where(mine, kd_ref[h], 0.0).astype(BF16) for h in heads]
        upd = [_dot_tn(kmask[h], vn_ref[h].astype(BF16)) for h in heads]
        for h in heads:
            sout_ref[bi, h] = s[h] * jnp.exp(gt_ref[h, rs, :][0:1, :]) + upd[h]
        return carry

    lax.fori_loop(0, n // group, per_seq, 0)
    for h in heads:
        o_ref[:, h * DN_DIM:(h + 1) * DN_DIM] = (oq_ref[h]
                                                 + _dot(intra[h].astype(BF16), vn_ref[h].astype(BF16)))


def _dn_grouped(qd, kd, vd, gates, s0, *, group):
    nrows, w = qd.shape
    n = DN_BLOCK
    per = n // group
    row = lambda c: pl.BlockSpec((n, c), lambda i: (i, 0))
    st = pl.BlockSpec((per, DN_HEADS, DN_DIM, DN_DIM), lambda i: (i, 0, 0, 0))
    return pl.pallas_call(
        functools.partial(_dn_grouped_kernel, group=group),
        grid=(nrows // n,),
        in_specs=[row(w), row(w), row(w), row(LANES), st],
        out_specs=[row(w), st],
        out_shape=[jax.ShapeDtypeStruct((nrows, w), F32), jax.ShapeDtypeStruct(s0.shape, F32)],
        scratch_shapes=[pltpu.VMEM((DN_HEADS, n, DN_DIM), F32)] * 7,
        compiler_params=_cparams(("parallel",)),
        name="deltanet_grouped",
    )(qd, kd, vd, gates, s0)


def _outproj_kernel(osb_ref, odn_ref, zg_ref, gdn_ref, x_ref, gt_ref, sc_ref, sh_ref, gn_ref, w_ref,
                    x1_ref, h2_ref):
    parts = [osb_ref[...].astype(BF16)]
    for hh in range(DN_HEADS):
        sl = slice(hh * DN_DIM, (hh + 1) * DN_DIM)
        parts.append((_rms_rows(odn_ref[:, sl], gdn_ref[...]) * zg_ref[:, sl]).astype(BF16))
    mix = _dot(jnp.concatenate(parts, axis=1), w_ref[...])
    x1 = x_ref[...] + gt_ref[...] * mix
    x1_ref[...] = x1
    h2_ref[...] = (_rms_rows(x1, gn_ref[...]) * (1.0 + sc_ref[...]) + sh_ref[...]).astype(BF16)


def _mod_specs(x, tm, per_row):
    d = x.shape[-1]
    if per_row:
        n = x.shape[0]
        row = lambda c: pl.BlockSpec((tm, c), lambda i: (i, 0))
        return (n // tm,), row, row(d), (n,), ("arbitrary",)
    b, t, _ = x.shape
    row = lambda c: pl.BlockSpec((None, tm, c), lambda i, j: (i, j, 0))
    per_b = pl.BlockSpec((None, 1, d), lambda i, j: (i, 0, 0))
    return (b, t // tm), row, per_b, (b, t), ("parallel", "arbitrary")


def _outproj(osb, odn, zg, gdn, x, gt, sc, sh, gn, w, *, tm, per_row):
    d = x.shape[-1]
    grid, row, mod, lead, sem = _mod_specs(x, tm, per_row)
    return pl.pallas_call(
        _outproj_kernel,
        grid=grid,
        in_specs=[row(SB_WIDTH), row(DN_WIDTH), row(DN_WIDTH), _resident(gdn.shape), row(d), mod, mod, mod,
                  _resident(gn.shape), _resident(w.shape)],
        out_specs=[row(d), row(d)],
        out_shape=[jax.ShapeDtypeStruct(lead + (d,), F32), jax.ShapeDtypeStruct(lead + (d,), BF16)],
        compiler_params=_cparams(sem),
        name="outproj_rows" if per_row else "outproj_seq",
    )(osb, odn, zg, gdn, x, gt, sc, sh, gn, w)


def _ffn_kernel(*refs, grouped, cw):
    if grouped:
        h_ref, x1_ref, gt_ref, wu_ref, wc_ref, wd_ref, buf_ref, y_ref, nbuf_ref, act_ref = refs
        carry_ref = None
    else:
        h_ref, x1_ref, gt_ref, wu_ref, wc_ref, wd_ref, y_ref, nbuf_ref, act_ref, carry_ref = refs
        buf_ref = None

        @pl.when(pl.program_id(1) == 0)
        def _():
            carry_ref[...] = jnp.zeros_like(carry_ref)

    hb = h_ref[...]
    tm = hb.shape[0]
    dff = wd_ref.shape[0]
    taps = wc_ref.shape[0]
    for c in range(dff // cw):
        halves = []
        for base in (0, dff):
            cs = slice(base + c * cw, base + (c + 1) * cw)
            cur = _dot(hb, wu_ref[:, cs])
            if grouped:
                y = _causal_conv(cur, wc_ref[:, cs], None, True, buf_ref[:, cs])
                nbuf_ref[:, cs] = pltpu.roll(cur, tm - (SUBLANES - (taps - 1)), 0)
            else:
                y = _causal_conv(cur, wc_ref[:, cs], carry_ref[:, cs], False, None)
                carry_ref[:, cs] = cur[tm - SUBLANES:tm]
                nbuf_ref[:, cs] = cur[tm - SUBLANES:tm]
            halves.append(y)
        act_ref[:, c * cw:(c + 1) * cw] = (_silu(halves[1]) * halves[0]).astype(BF16)
    y_ref[...] = x1_ref[...] + gt_ref[...] * _dot(act_ref[...], wd_ref[...])


def _ffn(h2, x1, gt, wu, wc, wd, bufp, *, tm, per_row, cw=256):
    d = x1.shape[-1]
    up = wu.shape[1]
    grid, row, mod, lead, sem = _mod_specs(x1, tm, per_row)
    in_specs = [row(d), row(d), mod, _resident(wu.shape), _resident(wc.shape), _resident(wd.shape)]
    args = [h2, x1, gt, wu, wc, wd]
    if per_row:
        in_specs.append(row(up))
        args.append(bufp)
        nb_shape, nb_spec, scratch = lead + (up,), row(up), [pltpu.VMEM((tm, up // 2), BF16)]
    else:
        nb_shape = (lead[0], SUBLANES, up)
        nb_spec = pl.BlockSpec((None, SUBLANES, up), lambda i, j: (i, 0, 0))
        scratch = [pltpu.VMEM((tm, up // 2), BF16), pltpu.VMEM((SUBLANES, up), F32)]
    return pl.pallas_call(
        functools.partial(_ffn_kernel, grouped=per_row, cw=cw),
        grid=grid, in_specs=in_specs,
        out_specs=[row(d), nb_spec],
        out_shape=[jax.ShapeDtypeStruct(lead + (d,), F32), jax.ShapeDtypeStruct(nb_shape, F32)],
        scratch_shapes=scratch,
        compiler_params=_cparams(sem),
        name="convffn_rows" if per_row else "convffn_seq",
    )(*args)


def _prep_weights(g_attn_norm, w_in, g_q, g_k, sb_bias, g_sb_out, w_dn_conv, a_log, dt_bias, g_dn_out,
                  w_out, g_ffn_norm, w_up, w_ffn_conv, w_down):
    d = w_in.shape[0]
    o1 = 3 * SB_WIDTH
    o2 = o1 + 3 * DN_WIDTH
    o3 = o2 + 2 * DN_HEADS
    pad = jnp.zeros((d, LANES - 2 * DN_HEADS), w_in.dtype)
    w1 = jnp.concatenate([w_in[:, :o2], w_in[:, o3:], w_in[:, o2:o3], pad], axis=1).astype(BF16)
    lane_pad = lambda v: jnp.zeros((1, LANES), F32).at[0, DN_HEADS:2 * DN_HEADS].set(v)
    return dict(
        gn1=g_attn_norm.reshape(1, d), w1=w1,
        gq=jnp.tile(g_q, SB_HEADS).reshape(1, SB_WIDTH), gk=jnp.tile(g_k, SB_HEADS).reshape(1, SB_WIDTH),
        bias=sb_bias.astype(F32), g_sb=jnp.tile(g_sb_out, SB_HEADS).reshape(1, SB_WIDTH),
        wc_dn=w_dn_conv, al=lane_pad(a_log), dt=lane_pad(dt_bias), gdn=g_dn_out.reshape(1, DN_DIM),
        w_out=w_out.astype(BF16), gn2=g_ffn_norm.reshape(1, d), w_up=w_up.astype(BF16),
        wc_ffn=w_ffn_conv, w_down=w_down.astype(BF16))


def _layer_prompt(x, mod, p):
    b, t, d = x.shape
    sh1, sc1, gt1, sh2, sc2, gt2 = [m.reshape(b, 1, d) for m in jnp.split(mod, 6, axis=-1)]
    qb, kb, vb, kf, vf, qd, kd, vd, zg, gates, dnbuf = _inproj(
        x, sc1, sh1, p["gn1"], p["w1"], p["gq"], p["gk"], p["wc_dn"], p["al"], p["dt"], None, tm=512)
    z_bound = 8.1 * jnp.max(jnp.abs(p["gq"])) * jnp.max(jnp.abs(p["gk"])) + jnp.max(jnp.abs(p["bias"]))
    attend = lambda bounded: functools.partial(_sb_prompt, tq=512, tb=256, unit=1, bounded=bounded)
    osb = lax.cond(z_bound < 80.0, attend(True), attend(False),
                   qb, kb, vb, p["bias"], p["g_sb"][:, :LANES])
    odn, s_new = _dn_seq(qd, kd, vd, gates)
    x1, h2 = _outproj(osb, odn, zg, p["gdn"], x, gt1, sc2, sh2, p["gn2"], p["w_out"], tm=512, per_row=False)
    y, ffbuf = _ffn(h2, x1, gt2, p["w_up"], p["wc_ffn"], p["w_down"], None, tm=512, per_row=False)
    k_dn = p["wc_dn"].shape[0] - 1
    k_ff = p["wc_ffn"].shape[0] - 1
    return (y, kf.reshape(b, t, SB_HEADS, SB_DIM), vf.reshape(b, t, SB_HEADS, SB_DIM), s_new,
            dnbuf[:, SUBLANES - k_dn:], ffbuf[:, SUBLANES - k_ff:])


def _layer_sample(x, mod, page_table, cache_k, cache_v, s0, dn_buf, ffn_buf, p):
    bs, n_q, d = x.shape
    assert n_q == SUBLANES
    n = bs * n_q
    rep = lambda m: jnp.broadcast_to(m[:, None, :], (bs, n_q, d)).reshape(n, d)
    sh1, sc1, gt1, sh2, sc2, gt2 = [rep(m) for m in jnp.split(mod, 6, axis=-1)]
    padrows = lambda buf: jnp.pad(buf, ((0, 0), (0, n_q - buf.shape[1]), (0, 0))).reshape(n, buf.shape[2])
    xf = x.reshape(n, d)
    qb, kb, vb, kf, vf, qd, kd, vd, zg, gates, dnbuf = _inproj(
        xf, sc1, sh1, p["gn1"], p["w1"], p["gq"], p["gk"], p["wc_dn"], p["al"], p["dt"], padrows(dn_buf), tm=256)
    del kb, vb
    w = SB_WIDTH
    bias_rows = jnp.repeat(p["bias"], n_q).reshape(SB_HEADS * n_q, 1)
    n_phys, page = cache_k.shape[0], cache_k.shape[1]
    pool_t = lambda c: jnp.transpose(c, (0, 2, 3, 1)).reshape(n_phys, w, page)
    osb = _sb_sample(page_table, bias_rows, qb.reshape(bs, n_q, w), kf.reshape(bs, n_q, w),
                     vf.reshape(bs, n_q, w), p["g_sb"], pool_t(cache_k), pool_t(cache_v))
    odn, s_new = _dn_grouped(qd, kd, vd, gates, s0, group=n_q)
    x1, h2 = _outproj(osb.reshape(n, w), odn, zg, p["gdn"], xf, gt1, sc2, sh2, p["gn2"], p["w_out"],
                      tm=256, per_row=True)
    y, ffbuf = _ffn(h2, x1, gt2, p["w_up"], p["wc_ffn"], p["w_down"], padrows(ffn_buf), tm=256, per_row=True)
    k_dn = p["wc_dn"].shape[0] - 1
    k_ff = p["wc_ffn"].shape[0] - 1
    return (y.reshape(bs, n_q, d), kf.reshape(bs, n_q, SB_HEADS, SB_DIM), vf.reshape(bs, n_q, SB_HEADS, SB_DIM),
            s_new, dnbuf.reshape(bs, n_q, -1)[:, :k_dn], ffbuf.reshape(bs, n_q, -1)[:, :k_ff])


def kernel(x_prompt, x_sample, c_prompt, c_sample, cache_k, cache_v, page_table, state_delta, state_dn_conv, state_ffn_conv, w_ada, b_ada, g_attn_norm, w_in, g_q, g_k, sb_bias, g_sb_out, w_dn_conv, a_log, dt_bias, g_dn_out, w_out, g_ffn_norm, w_up, w_ffn_conv, w_down):
    depth = w_ada.shape[0]
    bp = x_prompt.shape[0]
    yp, ys = x_prompt, x_sample
    outs = [[] for _ in range(10)]
    c_all = jnp.concatenate([c_prompt, c_sample], axis=0)
    pad_rows = (-c_all.shape[0]) % SUBLANES
    c_all = jnp.pad(c_all, ((0, pad_rows), (0, 0)))
    for l in range(depth):
        p = _prep_weights(g_attn_norm[l], w_in[l], g_q[l], g_k[l], sb_bias[l], g_sb_out[l], w_dn_conv[l],
                          a_log[l], dt_bias[l], g_dn_out[l], w_out[l], g_ffn_norm[l], w_up[l],
                          w_ffn_conv[l], w_down[l])
        mod = _ada(c_all, w_ada[l], b_ada[l])
        yp, kp, vp, sp, dcp, fcp = _layer_prompt(yp, mod[:bp], p)
        ys, ks, vs, ss, dcs, fcs = _layer_sample(ys, mod[bp:bp + x_sample.shape[0]], page_table,
                                                 cache_k[l], cache_v[l], state_delta[l],
                                                 state_dn_conv[l], state_ffn_conv[l], p)
        for lst, val in zip(outs, (kp, vp, ks, vs, sp, ss, dcp, dcs, fcp, fcs)):
            lst.append(val)
    return (yp, ys) + tuple(jnp.stack(o) for o in outs)
```

```python
import functools

import jax
import jax.numpy as jnp
from jax import lax
from jax.experimental import pallas as pl
from jax.experimental.pallas import tpu as pltpu

F32 = jnp.float32
BF16 = jnp.bfloat16

NORM_EPS = 1e-6
LOG2E = 1.4426950408889634
LANES = 128
SUBLANES = 8
VMEM_LIMIT = 56 * 1024 * 1024

SB_HEADS = 8
SB_DIM = 64
SB_WIDTH = SB_HEADS * SB_DIM
DN_HEADS = 4
DN_DIM = 128
DN_WIDTH = DN_HEADS * DN_DIM
DN_BLOCK = 128


def _dot(a, b):
    return jnp.dot(a, b, preferred_element_type=F32)


def _dot_nt(a, b):
    return lax.dot_general(a, b, (((1,), (1,)), ((), ())), preferred_element_type=F32)


def _dot_tn(a, b):
    return lax.dot_general(a, b, (((0,), (0,)), ((), ())), preferred_element_type=F32)


def _silu(x):
    return x * jax.nn.sigmoid(x)


def _exp_neg_abs(x):
    return jnp.exp2(jnp.abs(x) * (-LOG2E))


def _softplus_parts(z):
    l = jnp.log(1.0 + jnp.exp(-jnp.abs(z)))
    return jnp.maximum(z, 0.0) + l, jnp.minimum(z, 0.0) - l


def _rms_rows(x, g):
    ms = jnp.mean(x * x, axis=-1, keepdims=True)
    return x * lax.rsqrt(ms + NORM_EPS) * g


def _half_lane_rms(blk, lo):
    sq = blk * blk
    s_lo = jnp.sum(jnp.where(lo, sq, 0.0), axis=-1, keepdims=True)
    s_hi = jnp.sum(jnp.where(lo, 0.0, sq), axis=-1, keepdims=True)
    ms = jnp.where(lo, s_lo, s_hi) * (1.0 / SB_DIM)
    return blk * lax.rsqrt(ms + NORM_EPS)


def _cparams(sem):
    return pltpu.CompilerParams(dimension_semantics=sem, vmem_limit_bytes=VMEM_LIMIT)


def _resident(shape):
    nd = len(shape)
    return pl.BlockSpec(shape, lambda *_: (0,) * nd, pipeline_mode=pl.Buffered(1))


def _ada_kernel(c_ref, w_ref, b_ref, o_ref):
    a = _silu(c_ref[...]).astype(BF16)
    o_ref[...] = _dot(a, w_ref[...].astype(BF16)) + b_ref[...]


def _ada(c, w, b):
    m, d = c.shape
    n = w.shape[1]
    tn = 512
    return pl.pallas_call(
        _ada_kernel,
        grid=(n // tn,),
        in_specs=[pl.BlockSpec((m, d), lambda j: (0, 0)),
                  pl.BlockSpec((d, tn), lambda j: (0, j)),
                  pl.BlockSpec((1, tn), lambda j: (0, j))],
        out_specs=pl.BlockSpec((m, tn), lambda j: (0, j)),
        out_shape=jax.ShapeDtypeStruct((m, n), F32),
        compiler_params=_cparams(("parallel",)),
        name="adaln_mod",
    )(c, w, b.reshape(1, n))


C_SB = 0
C_DN = 3 * SB_WIDTH
C_Z = C_DN + 3 * DN_WIDTH
C_G = C_Z + DN_WIDTH
C_END = C_G + LANES


def _causal_conv(cur, taps, prev_rows, grouped, bufp):
    w = taps.shape[0]
    tm = cur.shape[0]
    y = taps[w - 1:w, :] * cur
    if grouped:
        rowm = lax.broadcasted_iota(jnp.int32, (tm, 1), 0) & (SUBLANES - 1)
        for s in range(1, w):
            r = pltpu.roll(cur, s, 0)
            back = (w - 1) - s
            bs = bufp if back == 0 else pltpu.roll(bufp, tm - back, 0)
            y = y + taps[w - 1 - s:w - s, :] * jnp.where(rowm >= s, r, bs)
        return y
    row8 = lax.broadcasted_iota(jnp.int32, (SUBLANES, 1), 0)
    yh = taps[w - 1:w, :] * cur[0:SUBLANES]
    for s in range(1, w):
        r = pltpu.roll(cur, s, 0)
        y = y + taps[w - 1 - s:w - s, :] * r
        head = jnp.where(row8 < s, pltpu.roll(prev_rows, s, 0), r[0:SUBLANES])
        yh = yh + taps[w - 1 - s:w - s, :] * head
    return jnp.concatenate([yh, y[SUBLANES:]], axis=0)


def _inproj_kernel(*refs, grouped):
    if grouped:
        (x_ref, sc_ref, sh_ref, gn_ref, w_ref, gq_ref, gk_ref, wc_ref, al_ref, dt_ref, buf_ref,
         qb_ref, kb_ref, vb_ref, kf_ref, vf_ref, qd_ref, kd_ref, vd_ref, zg_ref, gate_ref, nbuf_ref) = refs
        carry_ref = None
    else:
        (x_ref, sc_ref, sh_ref, gn_ref, w_ref, gq_ref, gk_ref, wc_ref, al_ref, dt_ref,
         qb_ref, kb_ref, vb_ref, kf_ref, vf_ref, qd_ref, kd_ref, vd_ref, zg_ref, gate_ref, nbuf_ref,
         carry_ref) = refs
        buf_ref = None

        @pl.when(pl.program_id(1) == 0)
        def _():
            carry_ref[...] = jnp.zeros_like(carry_ref)

    x = x_ref[...]
    h = _rms_rows(x, gn_ref[...]) * (1.0 + sc_ref[...]) + sh_ref[...]
    hb = h.astype(BF16)
    tm = x.shape[0]
    lane = lax.broadcasted_iota(jnp.int32, (1, LANES), 1)
    lo = lane < SB_DIM

    pq = _dot(hb, w_ref[:, C_SB:C_SB + SB_WIDTH])
    pk = _dot(hb, w_ref[:, C_SB + SB_WIDTH:C_SB + 2 * SB_WIDTH])
    for c in range(SB_WIDTH // LANES):
        sl = slice(c * LANES, (c + 1) * LANES)
        qn = _half_lane_rms(pq[:, sl], lo) * gq_ref[:, sl]
        qb_ref[:, sl] = (qn * (SB_DIM ** -0.5)).astype(BF16)
        kn = _half_lane_rms(pk[:, sl], lo) * gk_ref[:, sl]
        kf_ref[:, sl] = kn
        kb_ref[:, sl] = kn.astype(BF16)
    pv = _dot(hb, w_ref[:, C_SB + 2 * SB_WIDTH:C_SB + 3 * SB_WIDTH])
    vf_ref[...] = pv
    vb_ref[...] = pv.astype(BF16)

    cur = _dot(hb, w_ref[:, C_DN:C_Z])
    if grouped:
        y = _causal_conv(cur, wc_ref[...], None, True, buf_ref[...])
        nbuf_ref[...] = pltpu.roll(cur, tm - (SUBLANES - (wc_ref.shape[0] - 1)), 0)
    else:
        y = _causal_conv(cur, wc_ref[...], carry_ref[...], False, None)
        carry_ref[...] = cur[tm - SUBLANES:tm]
        nbuf_ref[...] = cur[tm - SUBLANES:tm]
    a = _silu(y)
    for hh in range(DN_HEADS):
        sl = slice(hh * DN_DIM, (hh + 1) * DN_DIM)
        qh = a[:, hh * DN_DIM:(hh + 1) * DN_DIM]
        kh = a[:, DN_WIDTH + hh * DN_DIM:DN_WIDTH + (hh + 1) * DN_DIM]
        qd_ref[:, sl] = qh * (lax.rsqrt(jnp.sum(qh * qh, axis=-1, keepdims=True) + NORM_EPS) * (DN_DIM ** -0.5))
        kd_ref[:, sl] = kh * lax.rsqrt(jnp.sum(kh * kh, axis=-1, keepdims=True) + NORM_EPS)
    vd_ref[...] = a[:, 2 * DN_WIDTH:3 * DN_WIDTH]

    zgb = _dot(hb, w_ref[:, C_Z:C_END])
    zg_ref[...] = _silu(zgb[:, :DN_WIDTH])

    gb = zgb[:, DN_WIDTH:]
    sp, _ = _softplus_parts(gb + dt_ref[...])
    gate_ref[...] = jnp.where(lane < DN_HEADS, jax.nn.sigmoid(gb), -jnp.exp(al_ref[...]) * sp)


def _inproj(x, sc, sh, gn, w1, gq, gk, wc, al, dt, bufp, *, tm):
    grouped = bufp is not None
    d = x.shape[-1]
    if grouped:
        n = x.shape[0]
        grid = (n // tm,)
        row = lambda c: pl.BlockSpec((tm, c), lambda i: (i, 0))
        in_specs = [row(d), row(d), row(d)]
        lead = (n,)
        sem = ("arbitrary",)
        nb_shape, nb_spec = (n, 3 * DN_WIDTH), row(3 * DN_WIDTH)
        scratch = []
    else:
        b, t, _ = x.shape
        grid = (b, t // tm)
        row = lambda c: pl.BlockSpec((None, tm, c), lambda i, j: (i, j, 0))
        per_b = pl.BlockSpec((None, 1, d), lambda i, j: (i, 0, 0))
        in_specs = [row(d), per_b, per_b]
        lead = (b, t)
        sem = ("parallel", "arbitrary")
        nb_shape = (b, SUBLANES, 3 * DN_WIDTH)
        nb_spec = pl.BlockSpec((None, SUBLANES, 3 * DN_WIDTH), lambda i, j: (i, 0, 0))
        scratch = [pltpu.VMEM((SUBLANES, 3 * DN_WIDTH), F32)]
    in_specs += [_resident(gn.shape), _resident(w1.shape), _resident(gq.shape), _resident(gk.shape),
                 _resident(wc.shape), _resident(al.shape), _resident(dt.shape)]
    args = [x, sc, sh, gn, w1, gq, gk, wc, al, dt]
    if grouped:
        in_specs.append(row(3 * DN_WIDTH))
        args.append(bufp)
    out_cols = [(SB_WIDTH, BF16)] * 3 + [(SB_WIDTH, F32)] * 2 + [(DN_WIDTH, F32)] * 4 + [(LANES, F32)]
    out_shape = [jax.ShapeDtypeStruct(lead + (c,), dt_) for c, dt_ in out_cols]
    out_specs = [row(c) for c, _ in out_cols]
    out_shape.append(jax.ShapeDtypeStruct(nb_shape, F32))
    out_specs.append(nb_spec)
    return pl.pallas_call(
        functools.partial(_inproj_kernel, grouped=grouped),
        grid=grid, in_specs=in_specs, out_specs=out_specs, out_shape=out_shape,
        scratch_shapes=scratch,
        compiler_params=_cparams(sem),
        name="inproj_grouped" if grouped else "inproj_seq",
    )(*args)


def _sb_prompt_kernel(bias_ref, q_ref, k_ref, v_ref, g_ref, o_ref, acc_ref, run_ref, sp_ref, lb_ref,
                      *, tq, tb, unit, bounded):
    softplus = ((lambda z: jnp.log(1.0 + jnp.exp(z))) if bounded else
                (lambda z: jnp.maximum(z, 0.0) + jnp.log(1.0 + _exp_neg_abs(z))))
    p = pl.program_id(1)
    i = pl.program_id(2)
    lane = lax.broadcasted_iota(jnp.int32, (1, LANES), 1)
    lo = lane < SB_DIM
    q = q_ref[...]
    zero = jnp.zeros_like(q)
    q2 = jnp.concatenate([jnp.where(lo, q, zero), jnp.where(lo, zero, q)], axis=0)
    head_row = lax.broadcasted_iota(jnp.int32, (2 * tq, LANES), 0) < tq
    lane2 = lax.broadcasted_iota(jnp.int32, (2 * tq, LANES), 1)
    b_parts = [m.astype(F32) for m in _split3(jnp.where(head_row, bias_ref[2 * p], bias_ref[2 * p + 1]))]
    bias_lanes = jnp.where(lane2 == 0, b_parts[0],
                           jnp.where(lane2 == 1, b_parts[1], jnp.where(lane2 == 2, b_parts[2], 0.0)))
    q2x = jnp.concatenate([q2, bias_lanes.astype(BF16)], axis=1)
    ones_lanes = jnp.where(lax.broadcasted_iota(jnp.int32, (tb, LANES), 1) < 3, 1.0, 0.0).astype(BF16)

    def with_ones(ks):
        n = ks.shape[0] // tb
        ones = ones_lanes if n == 1 else jnp.concatenate([ones_lanes] * n, axis=0)
        return jnp.concatenate([ks, ones], axis=1)
    row = lax.broadcasted_iota(jnp.int32, (tb, tb), 0)
    col = lax.broadcasted_iota(jnp.int32, (tb, tb), 1)
    later_keys = jnp.where(row > col, 1.0, 0.0).astype(BF16)
    qrow = lax.broadcasted_iota(jnp.int32, (2 * tq, tb), 0) & (tq - 1)
    kcol = lax.broadcasted_iota(jnp.int32, (2 * tq, tb), 1)

    acc_ref[...] = jnp.zeros_like(acc_ref)
    run_ref[...] = jnp.zeros_like(run_ref)

    def block(j, masked, row0=0):
        start = pl.multiple_of(j * tb, tb)
        ks = k_ref[pl.ds(start, tb), :]
        vs = v_ref[pl.ds(start, tb), :]
        live = tq - row0
        rows = (lambda x: jnp.concatenate([x[row0:tq], x[tq + row0:2 * tq]], axis=0)) if row0 else (lambda x: x)
        z = _dot_nt(rows(q2x), with_ones(ks))
        sp = softplus(z)
        lb = z - sp
        if masked:
            key_pos = lax.broadcasted_iota(jnp.int32, (2 * live, tb), 1) + (j * tb - i * tq)
            q_pos = (lax.broadcasted_iota(jnp.int32, (2 * live, tb), 0) & (live - 1)) + row0
            valid = key_pos < q_pos
            sp = jnp.where(valid, sp, 0.0)
        later = _dot(sp.astype(BF16), later_keys)
        run = rows(run_ref[...])
        att = jnp.concatenate(
            [jnp.exp(lb[:, c * LANES:(c + 1) * LANES] - later[:, c * LANES:(c + 1) * LANES] - run)
             for c in range(tb // LANES)], axis=1)
        if masked:
            att = jnp.where(valid, att, 0.0)
        upd = _dot(att.astype(BF16), vs)
        run = run + (later[:, 0:1] + sp[:, 0:1])
        if row0:
            for src, dst in ((slice(0, live), slice(row0, tq)), (slice(live, 2 * live), slice(tq + row0, 2 * tq))):
                acc_ref[dst] += upd[src]
                run_ref[dst] = run[src]
        else:
            acc_ref[...] += upd
            run_ref[...] = run

    def stage1(jhi, slot):
        start = pl.multiple_of((jhi - (unit - 1)) * tb, tb)
        z = _dot_nt(q2x, with_ones(k_ref[pl.ds(start, unit * tb), :]))
        sp = softplus(z)
        lb_ref[slot] = z - sp
        sp_ref[slot] = sp.astype(BF16)

    def stage2(jhi, slot):
        start = pl.multiple_of((jhi - (unit - 1)) * tb, tb)
        sp16 = sp_ref[slot]
        cols = [slice((unit - 1 - s) * tb, (unit - s) * tb) for s in range(unit)]
        later = _dot(jnp.concatenate([sp16[:, cs] for cs in cols], axis=0), later_keys)
        run = run_ref[...]
        att = [None] * unit
        for s, cs in enumerate(cols):
            lat = later[s * 2 * tq:(s + 1) * 2 * tq]
            att[unit - 1 - s] = jnp.concatenate(
                [jnp.exp(lb_ref[slot, :, cs.start + c * LANES:cs.start + (c + 1) * LANES]
                         - lat[:, c * LANES:(c + 1) * LANES] - run) for c in range(tb // LANES)], axis=1)
            run = run + (lat[:, 0:1] + sp16[:, cs.start:cs.start + 1].astype(F32))
        run_ref[...] = run
        acc_ref[...] += _dot(jnp.concatenate(att, axis=1).astype(BF16), v_ref[pl.ds(start, unit * tb), :])

    n_diag = tq // tb
    n_full = n_diag * i
    left = n_full % unit
    n_units = n_full // unit
    top = n_full - 1 - left
    lone = n_units & 1
    first = top - lone * unit
    n_pairs = n_units >> 1

    def diagonal():
        for d in range(n_diag):
            block(n_full + n_diag - 1 - d, True, row0=(n_diag - 1 - d) * tb)

    @pl.when(n_pairs >= 1)
    def _():
        stage1(first, 0)
        diagonal()

    @pl.when(n_pairs < 1)
    def _():
        diagonal()

    for r in range(unit - 1):
        @pl.when(left > r)
        def _(r=r):
            block(n_full - 1 - r, False)

    @pl.when(lone == 1)
    def _():
        stage1(top, 1)
        stage2(top, 1)

    @pl.when(n_pairs >= 1)
    def _():
        def body(u, carry):
            jhi = first - 2 * u * unit
            stage2(jhi, 0)
            stage1(jhi - unit, 1)
            stage2(jhi - unit, 1)
            stage1(jhi - 2 * unit, 0)
            return carry

        lax.fori_loop(0, n_pairs - 1, body, 0)
        jhi = first - 2 * (n_pairs - 1) * unit
        stage2(jhi, 0)
        stage1(jhi - unit, 1)
        stage2(jhi - unit, 1)

    o = jnp.where(lo, acc_ref[0:tq], acc_ref[tq:2 * tq])
    o_ref[...] = (_half_lane_rms(o, lo) * g_ref[...]).astype(o_ref.dtype)


def _sb_prompt(qb, kb, vb, bias, g2, *, tq, tb, unit, bounded):
    b, t, w = qb.shape
    pairs = w // LANES
    return pl.pallas_call(
        functools.partial(_sb_prompt_kernel, tq=tq, tb=tb, unit=unit, bounded=bounded),
        grid_spec=pltpu.PrefetchScalarGridSpec(
            num_scalar_prefetch=0,
            grid=(b, pairs, t // tq),
            in_specs=[pl.BlockSpec(memory_space=pltpu.SMEM),
                      pl.BlockSpec((None, tq, LANES), lambda i, p, j: (i, j, p)),
                      pl.BlockSpec((None, t, LANES), lambda i, p, j: (i, 0, p)),
                      pl.BlockSpec((None, t, LANES), lambda i, p, j: (i, 0, p)),
                      pl.BlockSpec((1, LANES), lambda i, p, j: (0, 0))],
            out_specs=pl.BlockSpec((None, tq, LANES), lambda i, p, j: (i, j, p)),
            scratch_shapes=[pltpu.VMEM((2 * tq, LANES), F32), pltpu.VMEM((2 * tq, LANES), F32),
                            pltpu.VMEM((2, 2 * tq, unit * tb), BF16), pltpu.VMEM((2, 2 * tq, unit * tb), F32)]),
        out_shape=jax.ShapeDtypeStruct((b, t, w), BF16),
        compiler_params=_cparams(("parallel", "parallel", "arbitrary")),
        name="sb_attn_prompt_bounded" if bounded else "sb_attn_prompt",
    )(bias, qb, kb, vb, g2)


def _page_copy(pt_ref, pool_ref, buf_ref, sem_ref, seq, slot, pg):
    return pltpu.make_async_copy(pool_ref.at[pt_ref[seq, pg]], buf_ref.at[slot, pg], sem_ref.at[slot])


def _sb_sample_kernel(pt_ref, bias_ref, q_ref, kn_ref, vn_ref, g_ref, kpool_ref, vpool_ref, o_ref,
                      kbuf_ref, vbuf_ref, ksem_ref, vsem_ref, newk_ref, newv_ref, *, n_pages, n_q):
    seq = pl.program_id(0)
    n_seq = pl.num_programs(0)
    slot = seq & 1
    rows = SB_HEADS * n_q
    page = kbuf_ref.shape[3]
    width = SB_WIDTH
    pools = ((kpool_ref, kbuf_ref, ksem_ref), (vpool_ref, vbuf_ref, vsem_ref))

    @pl.when(seq == 0)
    def _():
        for pool, buf, sem in pools:
            for pg in range(n_pages):
                _page_copy(pt_ref, pool, buf, sem, 0, 0, pg).start()

    @pl.when(seq + 1 < n_seq)
    def _():
        for pool, buf, sem in pools:
            for pg in range(n_pages):
                _page_copy(pt_ref, pool, buf, sem, seq + 1, 1 - slot, pg).start()

    r1 = lax.broadcasted_iota(jnp.int32, (rows, 1), 0)
    later_keys = jnp.where(lax.broadcasted_iota(jnp.int32, (page, page), 0)
                           > lax.broadcasted_iota(jnp.int32, (page, page), 1), 1.0, 0.0).astype(BF16)
    head_of_lane = lax.broadcasted_iota(jnp.int32, (rows, width), 1) // SB_DIM
    head_of_row = lax.broadcasted_iota(jnp.int32, (rows, width), 0) // n_q
    qrep = jnp.concatenate([q_ref[...].astype(F32)] * SB_HEADS, axis=0)
    qx = jnp.where(head_of_lane == head_of_row, qrep, 0.0).astype(BF16)
    bias = bias_ref[...]

    newk_ref[...] = jnp.zeros_like(newk_ref)
    newv_ref[...] = jnp.zeros_like(newv_ref)
    newk_ref[0:n_q, :] = kn_ref[...]
    newv_ref[0:n_q, :] = vn_ref[...]
    z = _dot_nt(qx, newk_ref[...].astype(BF16)) + bias
    sp, lb = _softplus_parts(z)
    valid = lax.broadcasted_iota(jnp.int32, (rows, page), 1) < (r1 & (n_q - 1))
    sp = jnp.where(valid, sp, 0.0)
    att = jnp.where(valid, jnp.exp(lb - _dot(sp.astype(BF16), later_keys)), 0.0)
    acc = _dot(att.astype(BF16), newv_ref[...].astype(BF16))
    run = jnp.sum(sp, axis=-1, keepdims=True)

    for pool, buf, sem in pools:
        for pg in range(n_pages):
            _page_copy(pt_ref, pool, buf, sem, seq, slot, pg).wait()

    order = [n_pages - 1 - r for r in range(n_pages)]
    k_all = jnp.concatenate([kbuf_ref[slot, pg].astype(BF16) for pg in order], axis=1)
    z = _dot(qx, k_all) + bias
    sp, lb = _softplus_parts(z)
    sp16 = sp.astype(BF16)
    later = _dot(jnp.concatenate([sp16[:, r * page:(r + 1) * page] for r in range(n_pages)], axis=0),
                 later_keys)
    args = []
    for r in range(n_pages):
        cs = slice(r * page, (r + 1) * page)
        args.append(lb[:, cs] - later[r * rows:(r + 1) * rows] - run)
        run = run + jnp.sum(sp[:, cs], axis=-1, keepdims=True)
    att = jnp.exp(jnp.concatenate(args, axis=1)).astype(BF16)
    v_all = jnp.concatenate([vbuf_ref[slot, pg].astype(BF16) for pg in order], axis=1)
    acc = acc + _dot_nt(att, v_all)

    lane_head = lax.broadcasted_iota(jnp.int32, (n_q, width), 1) // SB_DIM
    o = jnp.zeros((n_q, width), F32)
    for hh in range(SB_HEADS):
        o = o + jnp.where(lane_head == hh, acc[hh * n_q:(hh + 1) * n_q, :], 0.0)
    lo = lax.broadcasted_iota(jnp.int32, (1, LANES), 1) < SB_DIM
    for c in range(width // LANES):
        sl = slice(c * LANES, (c + 1) * LANES)
        o_ref[:, sl] = _half_lane_rms(o[:, sl], lo) * g_ref[:, sl]


def _sb_sample(page_table, bias_rows, q3, kn3, vn3, g_sb, cache_k, cache_v):
    bs, n_q, w = q3.shape
    n_pages = page_table.shape[1]
    page = cache_k.shape[2]
    rows = SB_HEADS * n_q
    tok = lambda: pl.BlockSpec((None, n_q, w), lambda i, pt: (i, 0, 0))
    const = lambda shp: pl.BlockSpec(shp, lambda i, pt: (0,) * len(shp))
    hbm = lambda: pl.BlockSpec(memory_space=pl.ANY)
    return pl.pallas_call(
        functools.partial(_sb_sample_kernel, n_pages=n_pages, n_q=n_q),
        grid_spec=pltpu.PrefetchScalarGridSpec(
            num_scalar_prefetch=1,
            grid=(bs,),
            in_specs=[const((rows, 1)), tok(), tok(), tok(), const((1, w)), hbm(), hbm()],
            out_specs=tok(),
            scratch_shapes=[pltpu.VMEM((2, n_pages, w, page), F32), pltpu.VMEM((2, n_pages, w, page), F32),
                            pltpu.SemaphoreType.DMA((2,)), pltpu.SemaphoreType.DMA((2,)),
                            pltpu.VMEM((page, w), F32), pltpu.VMEM((page, w), F32)]),
        out_shape=jax.ShapeDtypeStruct((bs, n_q, w), F32),
        compiler_params=_cparams(("arbitrary",)),
        name="sb_attn_paged",
    )(page_table, bias_rows, q3, kn3, vn3, g_sb, cache_k, cache_v)


def _split3(x):
    hi = x.astype(BF16)
    r = x - hi.astype(F32)
    mid = r.astype(BF16)
    lo = (r - mid.astype(F32)).astype(BF16)
    return hi, mid, lo


def _dn_local(blocks, group):
    n = blocks[0][3].shape[0]
    units = [(bi, h) for bi in range(len(blocks)) for h in range(DN_HEADS)]
    hs = [slice(h * DN_DIM, (h + 1) * DN_DIM) for h in range(DN_HEADS)]
    row = lax.broadcasted_iota(jnp.int32, (n, n), 0)
    col = lax.broadcasted_iota(jnp.int32, (n, n), 1)
    shift = group.bit_length() - 1
    same = jnp.where((row >> shift) == (col >> shift), 1.0, 0.0)
    low = jnp.where(row >= col, same, 0.0)
    strict = jnp.where(row > col, same, 0.0)
    low16 = low.astype(BF16)
    same16 = same.astype(BF16)
    gparts = [_split3(blk[3]) for blk in blocks]
    g_all = [sum(_dot(low16, m) for m in gp) for gp in gparts]
    t_all = [sum(_dot(same16, m) for m in gp) for gp in gparts]
    g_row = [jnp.broadcast_to(g_all[bi][:, DN_HEADS + h:DN_HEADS + h + 1], (n, n)) for bi, h in units]
    gtot = [jnp.broadcast_to(t_all[bi][:, DN_HEADS + h:DN_HEADS + h + 1], (n, n)) for bi, h in units]
    un = range(len(units))
    decay = [low * jnp.exp(jnp.minimum(g_row[u] - g_row[u].T, 0.0)) for u in un]
    beta = [blocks[bi][3][:, h:h + 1] for bi, h in units]
    k = [blocks[bi][1][:, hs[h]] for bi, h in units]
    k16 = [k[u].astype(BF16) for u in un]
    kb = [k[u] * beta[u] for u in un]
    a_mat = [strict * (_dot_nt(kb[u].astype(BF16), k16[u]) * decay[u]) for u in un]
    eye = jnp.where(row == col, 1.0, 0.0)
    inv = [eye - a_mat[u] * jnp.where((row >> 1) == (col >> 1), 1.0, 0.0) for u in un]
    for lb in range(1, shift):
        sel = jnp.where((row >> (lb + 1)) == (col >> (lb + 1)),
                        jnp.where(((row >> lb) & 1) > ((col >> lb) & 1), 1.0, 0.0), 0.0)
        inv16 = [inv[u].astype(BF16) for u in un]
        x = [_dot((a_mat[u] * sel).astype(BF16), inv16[u]) for u in un]
        inv = [inv[u] - _dot(inv16[u], x[u].astype(BF16)) for u in un]
    eg = [jnp.exp(g_row[u]) for u in un]
    v = [blocks[bi][2][:, hs[h]] for bi, h in units]
    rhs = [jnp.concatenate([v[u] * beta[u], kb[u] * eg[u]], axis=-1).astype(BF16) for u in un]
    sol = [_dot(inv[u].astype(BF16), rhs[u]) for u in un]
    q = [blocks[bi][0][:, hs[h]] for bi, h in units]
    intra = [_dot_nt(q[u].astype(BF16), k16[u]) * decay[u] for u in un]
    usol = [sol[u][:, :DN_DIM] for u in un]
    kcum = [sol[u][:, DN_DIM:] for u in un]
    qdec = [q[u] * eg[u] for u in un]
    kdec = [k[u] * jnp.exp(gtot[u] - g_row[u]) for u in un]
    return usol, kcum, intra, qdec, kdec, gtot


def _dn_seq_kernel(q_ref, k_ref, v_ref, gate_ref, o_ref, sout_ref, s_ref, *, group):
    @pl.when(pl.program_id(0) == 0)
    def _():
        s_ref[...] = jnp.zeros_like(s_ref)

    nb, n = q_ref.shape[0], q_ref.shape[1]
    blocks = [(q_ref.at[bi], k_ref.at[bi], v_ref.at[bi], gate_ref[bi]) for bi in range(nb)]
    usol, kcum, intra, qdec, kdec, gtot = _dn_local(blocks, group)
    units = [(bi, h) for bi in range(nb) for h in range(DN_HEADS)]
    un = range(len(units))
    s = [s_ref[bi, h] for bi, h in units]
    vnews = [[] for _ in un]
    outs = [[] for _ in un]
    for c in range(n // group):
        rs = slice(c * group, (c + 1) * group)
        s16 = [s[u].astype(BF16) for u in un]
        lhs = [jnp.concatenate([kcum[u][rs], qdec[u][rs]], axis=0).astype(BF16) for u in un]
        both = [_dot(lhs[u], s16[u]) for u in un]
        vnew = [usol[u][rs] - both[u][:group] for u in un]
        upd = [_dot_tn(kdec[u][rs].astype(BF16), vnew[u].astype(BF16)) for u in un]
        s = [s[u] * jnp.exp(gtot[u][c * group:c * group + 1, :]) + upd[u] for u in un]
        for u in un:
            vnews[u].append(vnew[u])
            outs[u].append(both[u][group:])
    for u, (bi, h) in enumerate(units):
        s_ref[bi, h] = s[u]
        vfull = jnp.concatenate(vnews[u], axis=0).astype(BF16)
        o_ref[bi, :, h * DN_DIM:(h + 1) * DN_DIM] = (jnp.concatenate(outs[u], axis=0)
                                                     + _dot(intra[u].astype(BF16), vfull))

    @pl.when(pl.program_id(0) == pl.num_programs(0) - 1)
    def _():
        sout_ref[...] = s_ref[...]


def _dn_seq(qd, kd, vd, gates):
    b, t, w = qd.shape
    n = DN_BLOCK
    row = lambda c: pl.BlockSpec((b, n, c), lambda j: (0, j, 0))
    state = (b, DN_HEADS, DN_DIM, DN_DIM)
    return pl.pallas_call(
        functools.partial(_dn_seq_kernel, group=min(64, n)),
        grid=(t // n,),
        in_specs=[row(w), row(w), row(w), row(LANES)],
        out_specs=[row(w), pl.BlockSpec(state, lambda j: (0, 0, 0, 0))],
        out_shape=[jax.ShapeDtypeStruct((b, t, w), F32), jax.ShapeDtypeStruct(state, F32)],
        scratch_shapes=[pltpu.VMEM(state, F32)],
        compiler_params=_cparams(("arbitrary",)),
        name="deltanet_seq",
    )(qd, kd, vd, gates)


def _dn_grouped_kernel(q_ref, k_ref, v_ref, gate_ref, s0_ref, o_ref, sout_ref,
                       u_ref, kc_ref, qd_ref, kd_ref, gt_ref, vn_ref, oq_ref, *, group):
    n = q_ref.shape[0]
    heads = range(DN_HEADS)
    row1 = lax.broadcasted_iota(jnp.int32, (n, 1), 0)
    shift = group.bit_length() - 1
    u, kcum, intra, qdec, kdec, gtot = _dn_local([(q_ref, k_ref, v_ref, gate_ref[...])], group)
    for h in heads:
        u_ref[h] = u[h]
        kc_ref[h] = kcum[h]
        qd_ref[h] = qdec[h]
        kd_ref[h] = kdec[h]
        gt_ref[h] = gtot[h]
    vn_ref[...] = jnp.zeros_like(vn_ref)

    def per_seq(bi, carry):
        rs = pl.ds(pl.multiple_of(bi * group, group), group)
        mine = (row1 >> shift) == bi
        s = [s0_ref[bi, h] for h in heads]
        lhs = [jnp.concatenate([kc_ref[h, rs, :], qd_ref[h, rs, :]], axis=0).astype(BF16) for h in heads]
        both = [_dot(lhs[h], s[h].astype(BF16)) for h in heads]
        for h in heads:
            oq_ref[h, rs, :] = both[h][group:]
            vn_ref[h, rs, :] = u_ref[h, rs, :] - both[h][:group]
        kmask = [jnp.where(mine, kd_ref[h], 0.0).astype(BF16) for h in heads]
        upd = [_dot_tn(kmask[h], vn_ref[h].astype(BF16)) for h in heads]
        for h in heads:
            sout_ref[bi, h] = s[h] * jnp.exp(gt_ref[h, rs, :][0:1, :]) + upd[h]
        return carry

    lax.fori_loop(0, n // group, per_seq, 0)
    for h in heads:
        o_ref[:, h * DN_DIM:(h + 1) * DN_DIM] = (oq_ref[h]
                                                 + _dot(intra[h].astype(BF16), vn_ref[h].astype(BF16)))


def _dn_grouped(qd, kd, vd, gates, s0, *, group):
    nrows, w = qd.shape
    n = DN_BLOCK
    per = n // group
    row = lambda c: pl.BlockSpec((n, c), lambda i: (i, 0))
    st = pl.BlockSpec((per, DN_HEADS, DN_DIM, DN_DIM), lambda i: (i, 0, 0, 0))
    return pl.pallas_call(
        functools.partial(_dn_grouped_kernel, group=group),
        grid=(nrows // n,),
        in_specs=[row(w), row(w), row(w), row(LANES), st],
        out_specs=[row(w), st],
        out_shape=[jax.ShapeDtypeStruct((nrows, w), F32), jax.ShapeDtypeStruct(s0.shape, F32)],
        scratch_shapes=[pltpu.VMEM((DN_HEADS, n, DN_DIM), F32)] * 7,
        compiler_params=_cparams(("parallel",)),
        name="deltanet_grouped",
    )(qd, kd, vd, gates, s0)


def _outproj_kernel(osb_ref, odn_ref, zg_ref, gdn_ref, x_ref, gt_ref, sc_ref, sh_ref, gn_ref, w_ref,
                    x1_ref, h2_ref):
    parts = [osb_ref[...].astype(BF16)]
    for hh in range(DN_HEADS):
        sl = slice(hh * DN_DIM, (hh + 1) * DN_DIM)
        parts.append((_rms_rows(odn_ref[:, sl], gdn_ref[...]) * zg_ref[:, sl]).astype(BF16))
    mix = _dot(jnp.concatenate(parts, axis=1), w_ref[...])
    x1 = x_ref[...] + gt_ref[...] * mix
    x1_ref[...] = x1
    h2_ref[...] = (_rms_rows(x1, gn_ref[...]) * (1.0 + sc_ref[...]) + sh_ref[...]).astype(BF16)


def _mod_specs(x, tm, per_row):
    d = x.shape[-1]
    if per_row:
        n = x.shape[0]
        row = lambda c: pl.BlockSpec((tm, c), lambda i: (i, 0))
        return (n // tm,), row, row(d), (n,), ("arbitrary",)
    b, t, _ = x.shape
    row = lambda c: pl.BlockSpec((None, tm, c), lambda i, j: (i, j, 0))
    per_b = pl.BlockSpec((None, 1, d), lambda i, j: (i, 0, 0))
    return (b, t // tm), row, per_b, (b, t), ("parallel", "arbitrary")


def _outproj(osb, odn, zg, gdn, x, gt, sc, sh, gn, w, *, tm, per_row):
    d = x.shape[-1]
    grid, row, mod, lead, sem = _mod_specs(x, tm, per_row)
    return pl.pallas_call(
        _outproj_kernel,
        grid=grid,
        in_specs=[row(SB_WIDTH), row(DN_WIDTH), row(DN_WIDTH), _resident(gdn.shape), row(d), mod, mod, mod,
                  _resident(gn.shape), _resident(w.shape)],
        out_specs=[row(d), row(d)],
        out_shape=[jax.ShapeDtypeStruct(lead + (d,), F32), jax.ShapeDtypeStruct(lead + (d,), BF16)],
        compiler_params=_cparams(sem),
        name="outproj_rows" if per_row else "outproj_seq",
    )(osb, odn, zg, gdn, x, gt, sc, sh, gn, w)


def _ffn_kernel(*refs, grouped, cw):
    if grouped:
        h_ref, x1_ref, gt_ref, wu_ref, wc_ref, wd_ref, buf_ref, y_ref, nbuf_ref, act_ref = refs
        carry_ref = None
    else:
        h_ref, x1_ref, gt_ref, wu_ref, wc_ref, wd_ref, y_ref, nbuf_ref, act_ref, carry_ref = refs
        buf_ref = None

        @pl.when(pl.program_id(1) == 0)
        def _():
            carry_ref[...] = jnp.zeros_like(carry_ref)

    hb = h_ref[...]
    tm = hb.shape[0]
    dff = wd_ref.shape[0]
    taps = wc_ref.shape[0]
    for c in range(dff // cw):
        halves = []
        for base in (0, dff):
            cs = slice(base + c * cw, base + (c + 1) * cw)
            cur = _dot(hb, wu_ref[:, cs])
            if grouped:
                y = _causal_conv(cur, wc_ref[:, cs], None, True, buf_ref[:, cs])
                nbuf_ref[:, cs] = pltpu.roll(cur, tm - (SUBLANES - (taps - 1)), 0)
            else:
                y = _causal_conv(cur, wc_ref[:, cs], carry_ref[:, cs], False, None)
                carry_ref[:, cs] = cur[tm - SUBLANES:tm]
                nbuf_ref[:, cs] = cur[tm - SUBLANES:tm]
            halves.append(y)
        act_ref[:, c * cw:(c + 1) * cw] = (_silu(halves[1]) * halves[0]).astype(BF16)
    y_ref[...] = x1_ref[...] + gt_ref[...] * _dot(act_ref[...], wd_ref[...])


def _ffn(h2, x1, gt, wu, wc, wd, bufp, *, tm, per_row, cw=256):
    d = x1.shape[-1]
    up = wu.shape[1]
    grid, row, mod, lead, sem = _mod_specs(x1, tm, per_row)
    in_specs = [row(d), row(d), mod, _resident(wu.shape), _resident(wc.shape), _resident(wd.shape)]
    args = [h2, x1, gt, wu, wc, wd]
    if per_row:
        in_specs.append(row(up))
        args.append(bufp)
        nb_shape, nb_spec, scratch = lead + (up,), row(up), [pltpu.VMEM((tm, up // 2), BF16)]
    else:
        nb_shape = (lead[0], SUBLANES, up)
        nb_spec = pl.BlockSpec((None, SUBLANES, up), lambda i, j: (i, 0, 0))
        scratch = [pltpu.VMEM((tm, up // 2), BF16), pltpu.VMEM((SUBLANES, up), F32)]
    return pl.pallas_call(
        functools.partial(_ffn_kernel, grouped=per_row, cw=cw),
        grid=grid, in_specs=in_specs,
        out_specs=[row(d), nb_spec],
        out_shape=[jax.ShapeDtypeStruct(lead + (d,), F32), jax.ShapeDtypeStruct(nb_shape, F32)],
        scratch_shapes=scratch,
        compiler_params=_cparams(sem),
        name="convffn_rows" if per_row else "convffn_seq",
    )(*args)


def _prep_weights(g_attn_norm, w_in, g_q, g_k, sb_bias, g_sb_out, w_dn_conv, a_log, dt_bias, g_dn_out,
                  w_out, g_ffn_norm, w_up, w_ffn_conv, w_down):
    d = w_in.shape[0]
    o1 = 3 * SB_WIDTH
    o2 = o1 + 3 * DN_WIDTH
    o3 = o2 + 2 * DN_HEADS
    pad = jnp.zeros((d, LANES - 2 * DN_HEADS), w_in.dtype)
    w1 = jnp.concatenate([w_in[:, :o2], w_in[:, o3:], w_in[:, o2:o3], pad], axis=1).astype(BF16)
    lane_pad = lambda v: jnp.zeros((1, LANES), F32).at[0, DN_HEADS:2 * DN_HEADS].set(v)
    return dict(
        gn1=g_attn_norm.reshape(1, d), w1=w1,
        gq=jnp.tile(g_q, SB_HEADS).reshape(1, SB_WIDTH), gk=jnp.tile(g_k, SB_HEADS).reshape(1, SB_WIDTH),
        bias=sb_bias.astype(F32), g_sb=jnp.tile(g_sb_out, SB_HEADS).reshape(1, SB_WIDTH),
        wc_dn=w_dn_conv, al=lane_pad(a_log), dt=lane_pad(dt_bias), gdn=g_dn_out.reshape(1, DN_DIM),
        w_out=w_out.astype(BF16), gn2=g_ffn_norm.reshape(1, d), w_up=w_up.astype(BF16),
        wc_ffn=w_ffn_conv, w_down=w_down.astype(BF16))


def _layer_prompt(x, mod, p):
    b, t, d = x.shape
    sh1, sc1, gt1, sh2, sc2, gt2 = [m.reshape(b, 1, d) for m in jnp.split(mod, 6, axis=-1)]
    qb, kb, vb, kf, vf, qd, kd, vd, zg, gates, dnbuf = _inproj(
        x, sc1, sh1, p["gn1"], p["w1"], p["gq"], p["gk"], p["wc_dn"], p["al"], p["dt"], None, tm=512)
    z_bound = 8.1 * jnp.max(jnp.abs(p["gq"])) * jnp.max(jnp.abs(p["gk"])) + jnp.max(jnp.abs(p["bias"]))
    attend = lambda bounded: functools.partial(_sb_prompt, tq=512, tb=256, unit=1, bounded=bounded)
    osb = lax.cond(z_bound < 80.0, attend(True), attend(False),
                   qb, kb, vb, p["bias"], p["g_sb"][:, :LANES])
    odn, s_new = _dn_seq(qd, kd, vd, gates)
    x1, h2 = _outproj(osb, odn, zg, p["gdn"], x, gt1, sc2, sh2, p["gn2"], p["w_out"], tm=512, per_row=False)
    y, ffbuf = _ffn(h2, x1, gt2, p["w_up"], p["wc_ffn"], p["w_down"], None, tm=512, per_row=False)
    k_dn = p["wc_dn"].shape[0] - 1
    k_ff = p["wc_ffn"].shape[0] - 1
    return (y, kf.reshape(b, t, SB_HEADS, SB_DIM), vf.reshape(b, t, SB_HEADS, SB_DIM), s_new,
            dnbuf[:, SUBLANES - k_dn:], ffbuf[:, SUBLANES - k_ff:])


def _layer_sample(x, mod, page_table, cache_k, cache_v, s0, dn_buf, ffn_buf, p):
    bs, n_q, d = x.shape
    assert n_q == SUBLANES
    n = bs * n_q
    rep = lambda m: jnp.broadcast_to(m[:, None, :], (bs, n_q, d)).reshape(n, d)
    sh1, sc1, gt1, sh2, sc2, gt2 = [rep(m) for m in jnp.split(mod, 6, axis=-1)]
    padrows = lambda buf: jnp.pad(buf, ((0, 0), (0, n_q - buf.shape[1]), (0, 0))).reshape(n, buf.shape[2])
    xf = x.reshape(n, d)
    qb, kb, vb, kf, vf, qd, kd, vd, zg, gates, dnbuf = _inproj(
        xf, sc1, sh1, p["gn1"], p["w1"], p["gq"], p["gk"], p["wc_dn"], p["al"], p["dt"], padrows(dn_buf), tm=256)
    del kb, vb
    w = SB_WIDTH
    bias_rows = jnp.repeat(p["bias"], n_q).reshape(SB_HEADS * n_q, 1)
    n_phys, page = cache_k.shape[0], cache_k.shape[1]
    pool_t = lambda c: jnp.transpose(c, (0, 2, 3, 1)).reshape(n_phys, w, page)
    osb = _sb_sample(page_table, bias_rows, qb.reshape(bs, n_q, w), kf.reshape(bs, n_q, w),
                     vf.reshape(bs, n_q, w), p["g_sb"], pool_t(cache_k), pool_t(cache_v))
    odn, s_new = _dn_grouped(qd, kd, vd, gates, s0, group=n_q)
    x1, h2 = _outproj(osb.reshape(n, w), odn, zg, p["gdn"], xf, gt1, sc2, sh2, p["gn2"], p["w_out"],
                      tm=256, per_row=True)
    y, ffbuf = _ffn(h2, x1, gt2, p["w_up"], p["wc_ffn"], p["w_down"], padrows(ffn_buf), tm=256, per_row=True)
    k_dn = p["wc_dn"].shape[0] - 1
    k_ff = p["wc_ffn"].shape[0] - 1
    return (y.reshape(bs, n_q, d), kf.reshape(bs, n_q, SB_HEADS, SB_DIM), vf.reshape(bs, n_q, SB_HEADS, SB_DIM),
            s_new, dnbuf.reshape(bs, n_q, -1)[:, :k_dn], ffbuf.reshape(bs, n_q, -1)[:, :k_ff])


def kernel(x_prompt, x_sample, c_prompt, c_sample, cache_k, cache_v, page_table, state_delta, state_dn_conv, state_ffn_conv, w_ada, b_ada, g_attn_norm, w_in, g_q, g_k, sb_bias, g_sb_out, w_dn_conv, a_log, dt_bias, g_dn_out, w_out, g_ffn_norm, w_up, w_ffn_conv, w_down):
    depth = w_ada.shape[0]
    bp = x_prompt.shape[0]
    yp, ys = x_prompt, x_sample
    outs = [[] for _ in range(10)]
    c_all = jnp.concatenate([c_prompt, c_sample], axis=0)
    pad_rows = (-c_all.shape[0]) % SUBLANES
    c_all = jnp.pad(c_all, ((0, pad_rows), (0, 0)))
    for l in range(depth):
        p = _prep_weights(g_attn_norm[l], w_in[l], g_q[l], g_k[l], sb_bias[l], g_sb_out[l], w_dn_conv[l],
                          a_log[l], dt_bias[l], g_dn_out[l], w_out[l], g_ffn_norm[l], w_up[l],
                          w_ffn_conv[l], w_down[l])
        mod = _ada(c_all, w_ada[l], b_ada[l])
        yp, kp, vp, sp, dcp, fcp = _layer_prompt(yp, mod[:bp], p)
        ys, ks, vs, ss, dcs, fcs = _layer_sample(ys, mod[bp:bp + x_sample.shape[0]], page_table,
                                                 cache_k[l], cache_v[l], state_delta[l],
                                                 state_dn_conv[l], state_ffn_conv[l], p)
        for lst, val in zip(outs, (kp, vp, ks, vs, sp, ss, dcp, dcs, fcp, fcs)):
            lst.append(val)
    return (yp, ys) + tuple(jnp.stack(o) for o in outs)
```
